```python
import math
import jax, jax.numpy as jnp
from jax import lax
import numpy as np

D_MODEL = 1024
BATCH = 8
SEQ = 4096
DEPTH = 2

CTX_LEN = 256
GRID_W = 64
Q_BLOCK = 128
ROPE_BASE = 10000.0
NORM_EPS = 1e-6

MLA_HEADS = 8
MLA_NOPE = 64
MLA_ROPE = 32
MLA_V = 64
Q_LORA = 256
KV_LORA = 128
MLA_WIDTH = MLA_HEADS * MLA_V

CONV_CH = 512
CONV_WIDTH = 31

DIFF_HEADS = 4
DIFF_HD = 64
DIFF_WIDTH = DIFF_HEADS * 2 * DIFF_HD

N_BRANCH = 3
IN_SPLITS = (Q_LORA, KV_LORA, MLA_ROPE, 2 * CONV_CH, DIFF_WIDTH, DIFF_WIDTH, DIFF_WIDTH, N_BRANCH * D_MODEL)
IN_COLS = Q_LORA + KV_LORA + MLA_ROPE + 2 * CONV_CH + 3 * DIFF_WIDTH + N_BRANCH * D_MODEL

N_GROUPS = 4
EXPERTS_PER_GROUP = 8
N_EXPERTS = N_GROUPS * EXPERTS_PER_GROUP
TOP_K_IN_GROUP = 2
EXPERT_FF = 256

kernel_name = 'hybrid_mla_conformer_diffattn_hmoe_block'


def rms_norm(x, g):
    xf = x.astype(jnp.float32)
    y = xf * lax.rsqrt(jnp.mean(xf * xf, axis=-1, keepdims=True) + NORM_EPS)
    return (y * g.astype(jnp.float32)).astype(x.dtype)


def layer_norm(x, g, b):
    xf = x.astype(jnp.float32)
    mu = jnp.mean(xf, axis=-1, keepdims=True)
    var = jnp.mean(jnp.square(xf - mu), axis=-1, keepdims=True)
    y = (xf - mu) * lax.rsqrt(var + NORM_EPS) * g.astype(jnp.float32) + b.astype(jnp.float32)
    return y.astype(x.dtype)


def split_cols(p):
    idx = np.cumsum(np.array(IN_SPLITS))[:-1].tolist()
    return jnp.split(p, idx, axis=-1)


def rope_tables(row, col, rot_dim):
    n = rot_dim // 4
    inv = ROPE_BASE ** (-jnp.arange(n, dtype=jnp.float32) / n)
    ang = jnp.concatenate([row[:, None] * inv, col[:, None] * inv], axis=-1)
    return jnp.cos(ang), jnp.sin(ang)


def rope_2d(x, rope):
    cos, sin = rope
    half = x.shape[-1] // 2
    shape = (x.shape[1],) + (1,) * (x.ndim - 3) + (half,)
    cos = cos.reshape(shape)
    sin = sin.reshape(shape)
    x1, x2 = x[..., :half], x[..., half:]
    return jnp.concatenate([x1 * cos - x2 * sin, x1 * sin + x2 * cos], axis=-1).astype(x.dtype)


def attend(q, k, v, scale):
    s = jnp.einsum('bqhd,bkhd->bhqk', q, k).astype(jnp.float32) * scale
    p = jax.nn.softmax(s, axis=-1).astype(v.dtype)
    return jnp.einsum('bhqk,bkhe->bqhe', p, v)


def diff_attend(q, k, v, lam, scale):
    s = jnp.einsum('bqhmd,bkhmd->bhmqk', q, k).astype(jnp.float32) * scale
    p = jax.nn.softmax(s, axis=-1)
    p = p[:, :, 0] - lam * p[:, :, 1]
    return jnp.einsum('bhqk,bkhe->bqhe', p.astype(v.dtype), v)


def sweep_query_blocks(fn, q):
    b, s = q.shape[:2]
    nb = s // Q_BLOCK
    qb = jnp.moveaxis(q.reshape(b, nb, Q_BLOCK, *q.shape[2:]), 1, 0)
    out = jnp.moveaxis(lax.map(fn, qb), 0, 1)
    return out.reshape(b, s, *out.shape[3:])


def mla_qkv(c_q, c_kv, k_r, q_norm_g, w_uq, kv_norm_g, w_ukv, rope):
    b, s = c_q.shape[:2]
    q = (rms_norm(c_q, q_norm_g) @ w_uq).reshape(b, s, MLA_HEADS, MLA_NOPE + MLA_ROPE)
    kv = (rms_norm(c_kv, kv_norm_g) @ w_ukv).reshape(b, s, MLA_HEADS, MLA_NOPE + MLA_V)
    q_nope, q_rope = q[..., :MLA_NOPE], q[..., MLA_NOPE:]
    k_nope, v = kv[..., :MLA_NOPE], kv[..., MLA_NOPE:]
    k_r = k_r[:, :, None, :]
    if rope is not None:
        q_rope = rope_2d(q_rope, rope)
        k_r = rope_2d(k_r, rope)
    q = jnp.concatenate([q_nope, q_rope], axis=-1)
    k = jnp.concatenate([k_nope, jnp.broadcast_to(k_r, (b, s, MLA_HEADS, MLA_ROPE))], axis=-1)
    return q, k, v


def diff_qkv(q, k, v, rope):
    b, s = q.shape[:2]
    q = q.reshape(b, s, DIFF_HEADS, 2, DIFF_HD)
    k = k.reshape(b, s, DIFF_HEADS, 2, DIFF_HD)
    v = v.reshape(b, s, DIFF_HEADS, 2 * DIFF_HD)
    if rope is not None:
        q = rope_2d(q, rope)
        k = rope_2d(k, rope)
    return q, k, v


def project_branches(h, w_in, q_norm_g, w_uq, kv_norm_g, w_ukv, rope_a, rope_d):
    cq, ckv, kr, glu, dq, dk, dv, gates = split_cols(h @ w_in)
    qa, ka, va = mla_qkv(cq, ckv, kr, q_norm_g, w_uq, kv_norm_g, w_ukv, rope_a)
    qd, kd, vd = diff_qkv(dq, dk, dv, rope_d)
    return qa, ka, va, qd, kd, vd, glu, gates


def conformer_conv(glu_in, conv_w, conv_b, ln_g, ln_b):
    a, g = jnp.split(glu_in, 2, axis=-1)
    u = a * jax.nn.sigmoid(g)
    pad = CONV_WIDTH // 2
    u = lax.conv_general_dilated(u, conv_w[:, None, :].astype(u.dtype), (1,), [(pad, pad)],
                                 dimension_numbers=('NWC', 'WIO', 'NWC'),
                                 feature_group_count=CONV_CH) + conv_b
    return jax.nn.silu(layer_norm(u, ln_g, ln_b))


def merge_branches(gates, o_mla, o_conv, o_diff, w_o_mla, w_o_conv, w_o_diff, w_out):
    g = jax.nn.sigmoid(gates).reshape(*gates.shape[:-1], N_BRANCH, D_MODEL)
    y = (g[..., 0, :] * (o_mla @ w_o_mla) + g[..., 1, :] * (o_conv @ w_o_conv)
         + g[..., 2, :] * (o_diff @ w_o_diff))
    return y @ w_out


def hier_moe(h, w_rg, b_rg, w_re, b_re, w_gate, w_up, w_down):
    b, t = h.shape[:2]
    g_logits = (h @ w_rg + b_rg).astype(jnp.float32)
    g_prob = jax.nn.softmax(g_logits, axis=-1)
    g_sel = jnp.argmax(g_logits, axis=-1)
    p_g = jnp.take_along_axis(g_prob, g_sel[..., None], axis=-1)
    e_logits = (h @ w_re + b_re).astype(jnp.float32).reshape(b, t, N_GROUPS, EXPERTS_PER_GROUP)
    e_in = jnp.take_along_axis(e_logits, g_sel[..., None, None], axis=-2)[..., 0, :]
    top_v, top_i = lax.top_k(e_in, TOP_K_IN_GROUP)
    w = jax.nn.softmax(top_v, axis=-1) * p_g
    eid = g_sel[..., None] * EXPERTS_PER_GROUP + top_i
    combine = jnp.einsum('btk,btke->bte', w, jax.nn.one_hot(eid, N_EXPERTS, dtype=jnp.float32))
    combine = combine.astype(h.dtype).reshape(b, t, N_GROUPS, EXPERTS_PER_GROUP)
    y = jnp.zeros(h.shape, h.dtype)
    for g in range(N_GROUPS):
        sl = slice(g * EXPERTS_PER_GROUP, (g + 1) * EXPERTS_PER_GROUP)
        a = jnp.einsum('btd,edf->btef', h, w_gate[sl])
        u = jnp.einsum('btd,edf->btef', h, w_up[sl])
        hid = jax.nn.silu(a) * u * combine[:, :, g, :, None]
        y = y + jnp.einsum('btef,efd->btd', hid, w_down[sl])
    return y


def hybrid_layer(xl, xc, c, c_ctx, layer_idx, update_ctx, rope_a, rope_d,
                 w_ada, b_ada, norm1_g, w_in, q_norm_g, w_uq, kv_norm_g, w_ukv,
                 conv_w, conv_b, conv_ln_g, conv_ln_b, lam_q1, lam_k1, lam_q2, lam_k2, diff_subln_g,
                 w_o_mla, w_o_conv, w_o_diff, w_out, norm2_g, w_rg, b_rg, w_re, b_re,
                 w_gate, w_up, w_down):
    b, s = xl.shape[:2]
    lam_init = 0.8 - 0.6 * math.exp(-0.3 * layer_idx)
    scale_a = 1.0 / math.sqrt(MLA_NOPE + MLA_ROPE)
    scale_d = 1.0 / math.sqrt(DIFF_HD)
    sh1_l, sc1_l, g1_l, sh2_l, sc2_l, g2_l = jnp.split((jax.nn.silu(c) @ w_ada + b_ada)[:, None, :], 6, axis=-1)
    sh1_c, sc1_c, g1_c, sh2_c, sc2_c, g2_c = jnp.split((jax.nn.silu(c_ctx) @ w_ada + b_ada)[None, None, :], 6, axis=-1)

    hl = rms_norm(xl, norm1_g) * (1 + sc1_l) + sh1_l
    hc = rms_norm(xc, norm1_g) * (1 + sc1_c) + sh1_c
    qa_c, ka_c, va_c, qd_c, kd_c, vd_c, glu_c, gates_c = project_branches(
        hc, w_in, q_norm_g, w_uq, kv_norm_g, w_ukv, None, None)
    qa_l, ka_l, va_l, qd_l, kd_l, vd_l, glu_l, gates_l = project_branches(
        hl, w_in, q_norm_g, w_uq, kv_norm_g, w_ukv, rope_a, rope_d)

    lf = lambda p: p.astype(jnp.float32)
    lam = (jnp.exp(jnp.sum(lf(lam_q1) * lf(lam_k1))) - jnp.exp(jnp.sum(lf(lam_q2) * lf(lam_k2)))
           + lam_init)

    def diff_post(o):
        return (rms_norm(o, diff_subln_g) * (1.0 - lam_init)).reshape(*o.shape[:2], DIFF_WIDTH)

    ka_all = jnp.concatenate([ka_c, ka_l], axis=1)
    va_all = jnp.concatenate([va_c, va_l], axis=1)
    kd_all = jnp.concatenate([kd_c, kd_l], axis=1)
    vd_all = jnp.concatenate([vd_c, vd_l], axis=1)
    o_mla_l = sweep_query_blocks(lambda qb: attend(qb, ka_all, va_all, scale_a), qa_l).reshape(b, s, MLA_WIDTH)
    o_diff_l = diff_post(sweep_query_blocks(lambda qb: diff_attend(qb, kd_all, vd_all, lam, scale_d), qd_l))
    o_conv_l = conformer_conv(glu_l, conv_w, conv_b, conv_ln_g, conv_ln_b)
    xl_new = xl + g1_l * merge_branches(gates_l, o_mla_l, o_conv_l, o_diff_l, w_o_mla, w_o_conv, w_o_diff, w_out)
    h2 = rms_norm(xl_new, norm2_g) * (1 + sc2_l) + sh2_l
    xl_new = xl_new + g2_l * hier_moe(h2, w_rg, b_rg, w_re, b_re, w_gate, w_up, w_down)

    if update_ctx:
        o_mla_c = attend(qa_c, ka_c, va_c, scale_a).reshape(b, xc.shape[1], MLA_WIDTH)
        o_diff_c = diff_post(diff_attend(qd_c, kd_c, vd_c, lam, scale_d))
        o_conv_c = conformer_conv(glu_c, conv_w, conv_b, conv_ln_g, conv_ln_b)
        xc = xc + g1_c * merge_branches(gates_c, o_mla_c, o_conv_c, o_diff_c, w_o_mla, w_o_conv, w_o_diff, w_out)
        h2c = rms_norm(xc, norm2_g) * (1 + sc2_c) + sh2_c
        xc = xc + g2_c * hier_moe(h2c, w_rg, b_rg, w_re, b_re, w_gate, w_up, w_down)
    return xl_new, xc


def setup_inputs(seed: int = 0) -> dict:
    key = jax.random.key(seed)
    ks = iter(jax.random.split(key, 48))
    f32 = jnp.float32

    def nrm(shape, scale):
        return jax.random.normal(next(ks), shape, f32) * scale

    L, D = DEPTH, D_MODEL
    return {
        'x': nrm((BATCH, SEQ, D), 1.0),
        'c': nrm((BATCH, D), 1.0),
        'ctx': nrm((BATCH, CTX_LEN, D), 1.0),
        'c_ctx': nrm((D,), 1.0),
        'w_ada': nrm((L, D, 6 * D), 0.5 * D ** -0.5),
        'b_ada': nrm((L, 6 * D), 0.02),
        'norm1_g': 1.0 + nrm((L, D), 0.02),
        'w_in': nrm((L, D, IN_COLS), D ** -0.5),
        'q_norm_g': 1.0 + nrm((L, Q_LORA), 0.02),
        'w_uq': nrm((L, Q_LORA, MLA_HEADS * (MLA_NOPE + MLA_ROPE)), Q_LORA ** -0.5),
        'kv_norm_g': 1.0 + nrm((L, KV_LORA), 0.02),
        'w_ukv': nrm((L, KV_LORA, MLA_HEADS * (MLA_NOPE + MLA_V)), KV_LORA ** -0.5),
        'conv_w': nrm((L, CONV_WIDTH, CONV_CH), CONV_WIDTH ** -0.5),
        'conv_b': nrm((L, CONV_CH), 0.02),
        'conv_ln_g': 1.0 + nrm((L, CONV_CH), 0.02),
        'conv_ln_b': nrm((L, CONV_CH), 0.02),
        'lam_q1': nrm((L, DIFF_HD), 0.1),
        'lam_k1': nrm((L, DIFF_HD), 0.1),
        'lam_q2': nrm((L, DIFF_HD), 0.1),
        'lam_k2': nrm((L, DIFF_HD), 0.1),
        'diff_subln_g': 1.0 + nrm((L, 2 * DIFF_HD), 0.02),
        'w_o_mla': nrm((L, MLA_WIDTH, D), MLA_WIDTH ** -0.5),
        'w_o_conv': nrm((L, CONV_CH, D), CONV_CH ** -0.5),
        'w_o_diff': nrm((L, DIFF_WIDTH, D), DIFF_WIDTH ** -0.5),
        'w_out': nrm((L, D, D), D ** -0.5),
        'norm2_g': 1.0 + nrm((L, D), 0.02),
        'w_rg': nrm((L, D, N_GROUPS), D ** -0.5),
        'b_rg': nrm((L, N_GROUPS), 0.01),
        'w_re': nrm((L, D, N_EXPERTS), D ** -0.5),
        'b_re': nrm((L, N_EXPERTS), 0.01),
        'w_gate': nrm((L, N_EXPERTS, D, EXPERT_FF), D ** -0.5),
        'w_up': nrm((L, N_EXPERTS, D, EXPERT_FF), D ** -0.5),
        'w_down': nrm((L, N_EXPERTS, EXPERT_FF, D), EXPERT_FF ** -0.5),
        'final_g': 1.0 + nrm((D,), 0.02),
    }


def reference(x, c, ctx, c_ctx, w_ada, b_ada, norm1_g, w_in, q_norm_g, w_uq, kv_norm_g, w_ukv,
              conv_w, conv_b, conv_ln_g, conv_ln_b, lam_q1, lam_k1, lam_q2, lam_k2, diff_subln_g,
              w_o_mla, w_o_conv, w_o_diff, w_out, norm2_g, w_rg, b_rg, w_re, b_re,
              w_gate, w_up, w_down, final_g):
    s = x.shape[1]
    rows = s // GRID_W
    row = jnp.repeat(jnp.arange(rows, dtype=jnp.float32), GRID_W)
    col = jnp.tile(jnp.arange(GRID_W, dtype=jnp.float32), rows)
    rope_a = rope_tables(row, col, MLA_ROPE)
    rope_d = rope_tables(row, col, DIFF_HD)
    xl, xc = x, ctx
    for l in range(DEPTH):
        xl, xc = hybrid_layer(
            xl, xc, c, c_ctx, l, l < DEPTH - 1, rope_a, rope_d,
            w_ada[l], b_ada[l], norm1_g[l], w_in[l], q_norm_g[l], w_uq[l], kv_norm_g[l], w_ukv[l],
            conv_w[l], conv_b[l], conv_ln_g[l], conv_ln_b[l], lam_q1[l], lam_k1[l], lam_q2[l], lam_k2[l],
            diff_subln_g[l], w_o_mla[l], w_o_conv[l], w_o_diff[l], w_out[l], norm2_g[l],
            w_rg[l], b_rg[l], w_re[l], b_re[l], w_gate[l], w_up[l], w_down[l])
    return rms_norm(xl, final_g)
```

```python
import functools
import math

import jax
import jax.numpy as jnp
from jax import lax
from jax.experimental import pallas as pl
from jax.experimental.pallas import tpu as pltpu

F32 = jnp.float32
BF16 = jnp.bfloat16

GRID_W = 64
ROPE_BASE = 10000.0
NORM_EPS = 1e-6

MLA_HEADS = 8
MLA_NOPE = 64
MLA_ROPE = 32
MLA_V = 64
Q_LORA = 256
KV_LORA = 128
CONV_CH = 512
CONV_WIDTH = 31
DIFF_HEADS = 4
DIFF_HD = 64
N_GROUPS = 4
EXPERTS_PER_GROUP = 8
N_EXPERTS = N_GROUPS * EXPERTS_PER_GROUP
EXPERT_FF = 256

LANES = 128
HALO = 16
VMEM_LIMIT = 48 * 1024 * 1024

C_CQ = 0
C_CKV = C_CQ + Q_LORA
C_KR = C_CKV + KV_LORA
C_GA = C_KR + LANES
C_GG = C_GA + CONV_CH
C_DQ = C_GG + CONV_CH
C_DK = C_DQ + 2 * DIFF_HEADS * DIFF_HD
C_DV = C_DK + 2 * DIFF_HEADS * DIFF_HD
C_GATES = C_DV + 2 * DIFF_HEADS * DIFF_HD


def _params(*sem):
    return pltpu.CompilerParams(dimension_semantics=sem, vmem_limit_bytes=VMEM_LIMIT)


def _sigmoid(x):
    return 0.5 * jnp.tanh(0.5 * x) + 0.5


def _rms(x, g):
    return x * lax.rsqrt(jnp.mean(x * x, axis=-1, keepdims=True) + NORM_EPS) * g


def _const_spec(shape):
    return pl.BlockSpec(shape, lambda *_: (0,) * len(shape))


def _ada_kernel(c_ref, w_ref, b_ref, o_ref):
    c = c_ref[...]
    h = c * _sigmoid(c)
    o_ref[...] = jnp.dot(h, w_ref[...], preferred_element_type=F32,
                         precision=lax.Precision.HIGHEST) + b_ref[...]


def _ada(cc, w_ada, b_ada):
    depth, d, n = w_ada.shape
    bn = 1536
    rows = cc.shape[0]
    return pl.pallas_call(
        _ada_kernel,
        grid=(depth, n // bn),
        in_specs=[pl.BlockSpec((rows, d), lambda l, j: (0, 0)),
                  pl.BlockSpec((None, d, bn), lambda l, j: (l, 0, j)),
                  pl.BlockSpec((None, 1, bn), lambda l, j: (l, 0, j))],
        out_specs=pl.BlockSpec((None, rows, bn), lambda l, j: (l, 0, j)),
        out_shape=jax.ShapeDtypeStruct((depth, rows, n), F32),
        compiler_params=_params("parallel", "parallel"),
        name="adaln",
    )(cc, w_ada, b_ada.reshape(depth, 1, n))


def _swap_halves(t, half):
    n = t.shape[1]
    lane = lax.broadcasted_iota(jnp.int32, t.shape, 1) % (2 * half)
    return jnp.where(lane < half, pltpu.roll(t, n - half, 1), pltpu.roll(t, half, 1))


def _proj_kernel(*refs, rope, scale_a, scale_d):
    if rope:
        (x_ref, mod_ref, n1g_ref, win_ref, qng_ref, wuq_ref, kvg_ref, wukv_ref,
         ca_ref, sa_ref, cd_ref, sd_ref,
         q_ref, k_ref, v_ref, u_ref, dq_ref, dk_ref, dv_ref, g_ref) = refs
    else:
        (x_ref, mod_ref, n1g_ref, win_ref, qng_ref, wuq_ref, kvg_ref, wukv_ref,
         q_ref, k_ref, v_ref, u_ref, dq_ref, dk_ref, dv_ref, g_ref) = refs

    h = _rms(x_ref[...], n1g_ref[...]) * (1.0 + mod_ref[1:2, :]) + mod_ref[0:1, :]
    hb = h.astype(BF16)

    def proj(a, b):
        return jnp.dot(hb, win_ref[:, a:b], preferred_element_type=F32)

    def rope_a(t):
        return t * ca_ref[...] + _swap_halves(t, MLA_ROPE // 2) * sa_ref[...] if rope else t

    def rope_d(t):
        return t * cd_ref[...] + _swap_halves(t, DIFF_HD // 2) * sd_ref[...] if rope else t

    cq = _rms(proj(C_CQ, C_CKV), qng_ref[...]).astype(BF16)
    ckv = _rms(proj(C_CKV, C_KR), kvg_ref[...]).astype(BF16)
    kr = rope_a(proj(C_KR, C_GA))
    q = jnp.dot(cq, wuq_ref[...], preferred_element_type=F32)
    kv = jnp.dot(ckv, wukv_ref[...], preferred_element_type=F32)
    for hd in range(MLA_HEADS):
        blk = slice(hd * LANES, (hd + 1) * LANES)
        q_ref[:, blk] = (rope_a(q[:, blk]) * scale_a).astype(BF16)
        k_ref[:, blk] = (kv[:, blk] + kr).astype(BF16)
    v_ref[...] = kv[:, MLA_HEADS * LANES:].astype(BF16)

    u_ref[...] = proj(C_GA, C_GG) * _sigmoid(proj(C_GG, C_DQ))

    dq = proj(C_DQ, C_DK)
    dk = proj(C_DK, C_DV)
    for hd in range(DIFF_HEADS):
        blk = slice(hd * LANES, (hd + 1) * LANES)
        dq_ref[:, blk] = (rope_d(dq[:, blk]) * scale_d).astype(BF16)
        dk_ref[:, blk] = rope_d(dk[:, blk]).astype(BF16)
    dv_ref[...] = proj(C_DV, C_GATES).astype(BF16)

    d = x_ref.shape[1]
    for j in range(3):
        g_ref[:, j * d:(j + 1) * d] = _sigmoid(proj(C_GATES + j * d, C_GATES + (j + 1) * d)).astype(BF16)


def _proj(x, mod, mod_row, n1g, win, qng, wuq, kvg, wukv, tables, seq, tm):
    t, d = x.shape
    rope = tables is not None
    tiles_per_seq = seq // tm
    row = lambda i: (i, 0)
    if mod_row is None:
        mod_map = lambda i: (i // tiles_per_seq, 0, 0)
    else:
        mod_map = lambda i: (mod_row, 0, 0)
    in_specs = [pl.BlockSpec((tm, d), row),
                pl.BlockSpec((None, 6, d), mod_map),
                _const_spec(n1g.shape), _const_spec(win.shape), _const_spec(qng.shape),
                _const_spec(wuq.shape), _const_spec(kvg.shape), _const_spec(wukv.shape)]
    args = [x, mod, n1g, win, qng, wuq, kvg, wukv]
    if rope:
        in_specs += [pl.BlockSpec((tm, LANES), lambda i: (i % tiles_per_seq, 0))] * 4
        args += list(tables)
    widths = (MLA_HEADS * LANES, MLA_HEADS * LANES, MLA_HEADS * MLA_V, CONV_CH,
              DIFF_HEADS * LANES, DIFF_HEADS * LANES, DIFF_HEADS * LANES, 3 * d)
    dtypes = (BF16, BF16, BF16, F32, BF16, BF16, BF16, BF16)
    kern = functools.partial(_proj_kernel, rope=rope,
                             scale_a=1.0 / math.sqrt(MLA_NOPE + MLA_ROPE), scale_d=1.0 / math.sqrt(DIFF_HD))
    return pl.pallas_call(
        kern,
        grid=(t // tm,),
        in_specs=in_specs,
        out_specs=[pl.BlockSpec((tm, w), row) for w in widths],
        out_shape=[jax.ShapeDtypeStruct((t, w), dt) for w, dt in zip(widths, dtypes)],
        compiler_params=_params("parallel"),
        name="proj_rope" if rope else "proj_ctx",
    )(*args)


def _qk(q, k):
    return lax.dot_general(q, k, (((1,), (1,)), ((), ())), preferred_element_type=F32)


def _flash(q, chunks):
    m = l = acc = None
    for k, v in chunks:
        s = _qk(q, k)
        if m is None:
            m = jnp.max(s, axis=-1, keepdims=True)
            p = jnp.exp(s - m)
            l = jnp.sum(p, axis=-1, keepdims=True)
            acc = jnp.dot(p.astype(BF16), v, preferred_element_type=F32)
        else:
            m_new = jnp.maximum(m, jnp.max(s, axis=-1, keepdims=True))
            alpha = jnp.exp(m - m_new)
            p = jnp.exp(s - m_new)
            l = alpha * l + jnp.sum(p, axis=-1, keepdims=True)
            acc = alpha * acc + jnp.dot(p.astype(BF16), v, preferred_element_type=F32)
            m = m_new
    return acc, 1.0 / l


def _key_chunks(kc_ref, vc_ref, kl_ref, vl_ref, tk, kcols):
    chunks = [(kc_ref[:, kcols], vc_ref[...])]
    if kl_ref is not None:
        for j in range(kl_ref.shape[0] // tk):
            rows = slice(j * tk, (j + 1) * tk)
            chunks.append((kl_ref[rows, kcols], vl_ref[rows, :]))
    return chunks


def _mla_attn_kernel(*refs, latent, tk):
    if latent:
        q_ref, kc_ref, vc_ref, kl_ref, vl_ref, o_ref = refs
    else:
        q_ref, kc_ref, vc_ref, o_ref = refs
        kl_ref = vl_ref = None
    outs = []
    for hh in range(2):
        cols = slice(hh * LANES, (hh + 1) * LANES)
        acc, inv_l = _flash(q_ref[:, cols], _key_chunks(kc_ref, vc_ref, kl_ref, vl_ref, tk, cols))
        outs.append(acc * inv_l)
    lane = lax.broadcasted_iota(jnp.int32, outs[0].shape, 1)
    o_ref[...] = jnp.where(lane < MLA_V, outs[0], outs[1]).astype(BF16)


def _mla_attn(q, kc, vc, kl, vl, tq, tk):
    b, lq, _ = q.shape
    c = kc.shape[1]
    latent = kl is not None
    in_specs = [pl.BlockSpec((None, tq, 2 * LANES), lambda i, p, j: (i, j, p)),
                pl.BlockSpec((None, c, 2 * LANES), lambda i, p, j: (i, 0, p)),
                pl.BlockSpec((None, c, LANES), lambda i, p, j: (i, 0, p))]
    args = [q, kc, vc]
    if latent:
        s = kl.shape[1]
        in_specs += [pl.BlockSpec((None, s, 2 * LANES), lambda i, p, j: (i, 0, p)),
                     pl.BlockSpec((None, s, LANES), lambda i, p, j: (i, 0, p))]
        args += [kl, vl]
    return pl.pallas_call(
        functools.partial(_mla_attn_kernel, latent=latent, tk=tk),
        grid=(b, MLA_HEADS // 2, lq // tq),
        in_specs=in_specs,
        out_specs=pl.BlockSpec((None, tq, LANES), lambda i, p, j: (i, j, p)),
        out_shape=jax.ShapeDtypeStruct((b, lq, MLA_HEADS * MLA_V), BF16),
        compiler_params=_params("parallel", "parallel", "parallel"),
        name="mla_attn_lat" if latent else "mla_attn_ctx",
    )(*args)


def _diff_attn_kernel(*refs, latent, tk, lam_init):
    if latent:
        lam_ref, g_ref, q_ref, kc_ref, vc_ref, kl_ref, vl_ref, o_ref = refs
    else:
        lam_ref, g_ref, q_ref, kc_ref, vc_ref, o_ref = refs
        kl_ref = vl_ref = None
    q = q_ref[...]
    lane = lax.broadcasted_iota(jnp.int32, q.shape, 1)
    chunks = _key_chunks(kc_ref, vc_ref, kl_ref, vl_ref, tk, slice(None))
    outs = []
    for first in (True, False):
        qm = jnp.where((lane < DIFF_HD) == first, q, jnp.zeros_like(q))
        acc, inv_l = _flash(qm, chunks)
        outs.append(acc * inv_l)
    lam = (jnp.exp(jnp.sum(lam_ref[0:1, :] * lam_ref[1:2, :], axis=-1, keepdims=True))
           - jnp.exp(jnp.sum(lam_ref[2:3, :] * lam_ref[3:4, :], axis=-1, keepdims=True)) + lam_init)
    o = outs[0] - lam * outs[1]
    o_ref[...] = (_rms(o, g_ref[...]) * (1.0 - lam_init)).astype(BF16)


def _diff_attn(lam_p, subln_g, q, kc, vc, kl, vl, tq, tk, lam_init):
    b, lq, _ = q.shape
    c = kc.shape[1]
    latent = kl is not None
    blk = lambda rows: pl.BlockSpec((None, rows, LANES), lambda i, h, j: (i, 0, h))
    in_specs = [_const_spec(lam_p.shape), _const_spec(subln_g.shape),
                pl.BlockSpec((None, tq, LANES), lambda i, h, j: (i, j, h)), blk(c), blk(c)]
    args = [lam_p, subln_g, q, kc, vc]
    if latent:
        in_specs += [blk(kl.shape[1]), blk(kl.shape[1])]
        args += [kl, vl]
    return pl.pallas_call(
        functools.partial(_diff_attn_kernel, latent=latent, tk=tk, lam_init=lam_init),
        grid=(b, DIFF_HEADS, lq // tq),
        in_specs=in_specs,
        out_specs=pl.BlockSpec((None, tq, LANES), lambda i, h, j: (i, j, h)),
        out_shape=jax.ShapeDtypeStruct((b, lq, DIFF_HEADS * LANES), BF16),
        compiler_params=_params("parallel", "parallel", "parallel"),
        name="diff_attn_lat" if latent else "diff_attn_ctx",
    )(*args)


def _conv_kernel(prev_ref, cur_ref, next_ref, w_ref, b_ref, g_ref, beta_ref, o_ref, pad_ref):
    j = pl.program_id(1)
    tr = cur_ref.shape[0]
    zero = jnp.zeros((HALO, cur_ref.shape[1]), F32)
    pad_ref[0:HALO, :] = jnp.where(j == 0, zero, prev_ref[...])
    pad_ref[HALO:HALO + tr, :] = cur_ref[...]
    pad_ref[HALO + tr:, :] = jnp.where(j == pl.num_programs(1) - 1, zero, next_ref[...])
    off = HALO - CONV_WIDTH // 2
    acc = jnp.zeros(cur_ref.shape, F32) + b_ref[...]
    for tap in range(CONV_WIDTH):
        acc = acc + pad_ref[off + tap:off + tap + tr, :] * w_ref[tap:tap + 1, :]
    mu = jnp.mean(acc, axis=-1, keepdims=True)
    cen = acc - mu
    y = cen * lax.rsqrt(jnp.mean(cen * cen, axis=-1, keepdims=True) + NORM_EPS) * g_ref[...] + beta_ref[...]
    o_ref[...] = (y * _sigmoid(y)).astype(BF16)


def _conv(u, w, bias, g, beta, tr):
    b, l, ch = u.shape
    per = tr // HALO
    last = l // HALO - 1
    return pl.pallas_call(
        _conv_kernel,
        grid=(b, l // tr),
        in_specs=[pl.BlockSpec((None, HALO, ch), lambda i, j: (i, jnp.maximum(j * per - 1, 0), 0)),
                  pl.BlockSpec((None, tr, ch), lambda i, j: (i, j, 0)),
                  pl.BlockSpec((None, HALO, ch), lambda i, j: (i, jnp.minimum((j + 1) * per, last), 0)),
                  _const_spec(w.shape), _const_spec(bias.shape), _const_spec(g.shape), _const_spec(beta.shape)],
        out_specs=pl.BlockSpec((None, tr, ch), lambda i, j: (i, j, 0)),
        out_shape=jax.ShapeDtypeStruct((b, l, ch), BF16),
        scratch_shapes=[pltpu.VMEM((tr + 2 * HALO, ch), F32)],
        compiler_params=_params("parallel", "parallel"),
        name="conv_ln_silu",
    )(u, u, u, w, bias, g, beta)


def _route(logits):
    lane = lax.broadcasted_iota(jnp.int32, logits.shape, 1)
    neg = jnp.float32(-jnp.inf)
    big = jnp.int32(LANES)
    is_grp = (lane >= N_EXPERTS) & (lane < N_EXPERTS + N_GROUPS)
    gl = jnp.where(is_grp, logits, neg)
    gmax = jnp.max(gl, axis=-1, keepdims=True)
    gsel = jnp.min(jnp.where(gl == gmax, lane, big), axis=-1, keepdims=True) - N_EXPERTS
    p_g = 1.0 / jnp.sum(jnp.exp(gl - gmax), axis=-1, keepdims=True)
    in_grp = (lane >= gsel * EXPERTS_PER_GROUP) & (lane < (gsel + 1) * EXPERTS_PER_GROUP)
    e1 = jnp.where(in_grp, logits, neg)
    v1 = jnp.max(e1, axis=-1, keepdims=True)
    i1 = jnp.min(jnp.where(e1 == v1, lane, big), axis=-1, keepdims=True)
    e2 = jnp.where(lane == i1, neg, e1)
    v2 = jnp.max(e2, axis=-1, keepdims=True)
    i2 = jnp.min(jnp.where(e2 == v2, lane, big), axis=-1, keepdims=True)
    r = jnp.exp(v2 - v1)
    w1 = p_g / (1.0 + r)
    return jnp.where(lane == i1, w1, jnp.where(lane == i2, w1 * r, 0.0))


def _merge_kernel(x_ref, om_ref, oc_ref, od_ref, gates_ref, mod_ref, wm_ref, wc_ref, wd_ref, wout_ref,
                  n2g_ref, wr_ref, br_ref, xn_ref, h2_ref, comb_ref):
    d = x_ref.shape[1]
    gate = lambda j: gates_ref[:, j * d:(j + 1) * d].astype(F32)
    y = (gate(0) * jnp.dot(om_ref[...], wm_ref[...], preferred_element_type=F32)
         + gate(1) * jnp.dot(oc_ref[...], wc_ref[...], preferred_element_type=F32)
         + gate(2) * jnp.dot(od_ref[...], wd_ref[...], preferred_element_type=F32))
    z = jnp.dot(y.astype(BF16), wout_ref[...], preferred_element_type=F32)
    xn = x_ref[...] + mod_ref[2:3, :] * z
    xn_ref[...] = xn
    h2 = _rms(xn, n2g_ref[...]) * (1.0 + mod_ref[4:5, :]) + mod_ref[3:4, :]
    h_hi = h2.astype(BF16)
    h2_ref[...] = h_hi
    h_lo = (h2 - h_hi.astype(F32)).astype(BF16)
    hw = jnp.dot(h_hi, wr_ref[...], preferred_element_type=F32)
    lw = jnp.dot(h_lo, wr_ref[:, 0:LANES], preferred_element_type=F32)
    comb_ref[...] = _route(hw[:, 0:LANES] + hw[:, LANES:] + lw + br_ref[...])


def _merge(x, om, oc, od, gates, mod, mod_row, wm, wc, wd, wout, n2g, wr, br, seq, tm):
    t, d = x.shape
    tiles_per_seq = seq // tm
    row = lambda i: (i, 0)
    if mod_row is None:
        mod_map = lambda i: (i // tiles_per_seq, 0, 0)
    else:
        mod_map = lambda i: (mod_row, 0, 0)
    rows = lambda a: pl.BlockSpec((tm, a.shape[1]), row)
    return pl.pallas_call(
        _merge_kernel,
        grid=(t // tm,),
        in_specs=[rows(x), rows(om), rows(oc), rows(od), rows(gates),
                  pl.BlockSpec((None, 6, d), mod_map)]
                 + [_const_spec(a.shape) for a in (wm, wc, wd, wout, n2g, wr, br)],
        out_specs=[pl.BlockSpec((tm, d), row), pl.BlockSpec((tm, d), row), pl.BlockSpec((tm, LANES), row)],
        out_shape=[jax.ShapeDtypeStruct((t, d), F32), jax.ShapeDtypeStruct((t, d), BF16),
                   jax.ShapeDtypeStruct((t, LANES), F32)],
        compiler_params=_params("parallel"),
        name="merge_route",
    )(x, om, oc, od, gates, mod, wm, wc, wd, wout, n2g, wr, br)


def _moe_kernel(h_ref, comb_ref, xn_ref, mod_ref, wg_ref, wu_ref, wd_ref, fg_ref, o_ref, acc_ref, *, final):
    e = pl.program_id(1)

    @pl.when(e == 0)
    def _():
        acc_ref[...] = jnp.zeros_like(acc_ref)

    h = h_ref[...]
    a = jnp.dot(h, wg_ref[...], preferred_element_type=F32)
    u = jnp.dot(h, wu_ref[...], preferred_element_type=F32)
    comb = comb_ref[...]
    lane = lax.broadcasted_iota(jnp.int32, comb.shape, 1)
    cw = jnp.sum(jnp.where(lane == e, comb, 0.0), axis=-1, keepdims=True)
    hid = a * _sigmoid(a) * u * cw
    acc_ref[...] += jnp.dot(hid.astype(BF16), wd_ref[...], preferred_element_type=F32)

    @pl.when(e == pl.num_programs(1) - 1)
    def _():
        out = xn_ref[...] + mod_ref[5:6, :] * acc_ref[...]
        o_ref[...] = _rms(out, fg_ref[...]) if final else out


def _moe(h2, comb, xn, mod, mod_row, wg, wu, wd, final_g, seq, tm, final):
    t, d = xn.shape
    tiles_per_seq = seq // tm
    row = lambda i, e: (i, 0)
    if mod_row is None:
        mod_map = lambda i, e: (i // tiles_per_seq, 0, 0)
    else:
        mod_map = lambda i, e: (mod_row, 0, 0)
    ff = wg.shape[2]
    return pl.pallas_call(
        functools.partial(_moe_kernel, final=final),
        grid=(t // tm, N_EXPERTS),
        in_specs=[pl.BlockSpec((tm, d), row), pl.BlockSpec((tm, LANES), row), pl.BlockSpec((tm, d), row),
                  pl.BlockSpec((None, 6, d), mod_map),
                  pl.BlockSpec((None, d, ff), lambda i, e: (e, 0, 0)),
                  pl.BlockSpec((None, d, ff), lambda i, e: (e, 0, 0)),
                  pl.BlockSpec((None, ff, d), lambda i, e: (e, 0, 0)),
                  pl.BlockSpec(final_g.shape, lambda i, e: (0, 0))],
        out_specs=pl.BlockSpec((tm, d), row),
        out_shape=jax.ShapeDtypeStruct((t, d), F32),
        scratch_shapes=[pltpu.VMEM((tm, d), F32)],
        compiler_params=_params("parallel", "arbitrary"),
        name="moe",
    )(h2, comb, xn, mod, wg, wu, wd, final_g)


def _rope_tables(s):
    rows = s // GRID_W
    row = jnp.repeat(jnp.arange(rows, dtype=F32), GRID_W)
    col = jnp.tile(jnp.arange(GRID_W, dtype=F32), rows)

    def cos_sin(rot_dim):
        n = rot_dim // 4
        inv = ROPE_BASE ** (-jnp.arange(n, dtype=F32) / n)
        ang = jnp.concatenate([row[:, None] * inv, col[:, None] * inv], axis=-1)
        return jnp.cos(ang), jnp.sin(ang)

    ca, sa = cos_sin(MLA_ROPE)
    one = jnp.ones((s, MLA_NOPE), F32)
    pad1 = jnp.ones((s, LANES - MLA_NOPE - MLA_ROPE), F32)
    cos_a = jnp.concatenate([one, ca, ca, pad1], axis=-1)
    sin_a = jnp.concatenate([0 * one, -sa, sa, 0 * pad1], axis=-1)
    cd, sd = cos_sin(DIFF_HD)
    cos_d = jnp.concatenate([cd, cd, cd, cd], axis=-1)
    sin_d = jnp.concatenate([-sd, sd, -sd, sd], axis=-1)
    return cos_a, sin_a, cos_d, sin_d


def _layout_w_in(w_in):
    d = w_in.shape[0]
    o = 0
    parts = {}
    for name, width in (("cq", Q_LORA), ("ckv", KV_LORA), ("kr", MLA_ROPE), ("glu", 2 * CONV_CH),
                        ("dq", 2 * DIFF_HEADS * DIFF_HD), ("dk", 2 * DIFF_HEADS * DIFF_HD),
                        ("dv", 2 * DIFF_HEADS * DIFF_HD), ("gates", 3 * d)):
        parts[name] = w_in[:, o:o + width]
        o += width
    kr_blk = jnp.concatenate([jnp.zeros((d, MLA_NOPE), w_in.dtype), parts["kr"],
                              jnp.zeros((d, LANES - MLA_NOPE - MLA_ROPE), w_in.dtype)], axis=1)
    return jnp.concatenate([parts["cq"], parts["ckv"], kr_blk, parts["glu"], parts["dq"], parts["dk"],
                            parts["dv"], parts["gates"]], axis=1).astype(BF16)


def _layout_w_uq(w_uq):
    r = w_uq.shape[0]
    w = w_uq.reshape(r, MLA_HEADS, MLA_NOPE + MLA_ROPE)
    w = jnp.pad(w, ((0, 0), (0, 0), (0, LANES - MLA_NOPE - MLA_ROPE)))
    return w.reshape(r, MLA_HEADS * LANES).astype(BF16)


def _layout_w_ukv(w_ukv):
    r = w_ukv.shape[0]
    w = w_ukv.reshape(r, MLA_HEADS, MLA_NOPE + MLA_V)
    wk = jnp.pad(w[:, :, :MLA_NOPE], ((0, 0), (0, 0), (0, LANES - MLA_NOPE))).reshape(r, MLA_HEADS * LANES)
    wv = w[:, :, MLA_NOPE:].reshape(r, MLA_HEADS * MLA_V)
    return jnp.concatenate([wk, wv], axis=1).astype(BF16)


def _layout_router(w_rg, b_rg, w_re, b_re):
    d = w_rg.shape[0]
    w = jnp.concatenate([w_re, w_rg, jnp.zeros((d, LANES - N_EXPERTS - N_GROUPS), F32)], axis=1)
    hi = w.astype(BF16)
    lo = (w - hi.astype(F32)).astype(BF16)
    b = jnp.concatenate([b_re, b_rg, jnp.zeros((LANES - N_EXPERTS - N_GROUPS,), F32)])[None, :]
    return jnp.concatenate([hi, lo], axis=1), b


def kernel(x, c, ctx, c_ctx, w_ada, b_ada, norm1_g, w_in, q_norm_g, w_uq, kv_norm_g, w_ukv, conv_w, conv_b,
           conv_ln_g, conv_ln_b, lam_q1, lam_k1, lam_q2, lam_k2, diff_subln_g, w_o_mla, w_o_conv, w_o_diff,
           w_out, norm2_g, w_rg, b_rg, w_re, b_re, w_gate, w_up, w_down, final_g):
    b, s, d = x.shape
    n_ctx = ctx.shape[1]
    depth = w_ada.shape[0]
    tm = 256
    tq = 256
    tk = min(512, s)
    tr = 256
    tmoe_l = min(1024, s)
    tmoe_c = min(1024, b * n_ctx)
    ctx_row = b

    rows = -(-(b + 1) // 8) * 8
    cc = jnp.zeros((rows, d), F32).at[:b].set(c).at[b].set(c_ctx)
    mod_all = _ada(cc, w_ada, b_ada).reshape(depth, rows, 6, d)
    tables = _rope_tables(s)
    row2 = lambda a: a[None, :]

    xl = x.reshape(b * s, d)
    xc = ctx.reshape(b * n_ctx, d)
    for l in range(depth):
        update_ctx = l < depth - 1
        final = l == depth - 1
        lam_init = 0.8 - 0.6 * math.exp(-0.3 * l)
        mod = mod_all[l]
        win = _layout_w_in(w_in[l])
        wuq = _layout_w_uq(w_uq[l])
        wukv = _layout_w_ukv(w_ukv[l])
        n1g, qng, kvg = row2(norm1_g[l]), row2(q_norm_g[l]), row2(kv_norm_g[l])
        lam_p = jnp.stack([lam_q1[l], lam_k1[l], lam_q2[l], lam_k2[l]])
        subln = row2(diff_subln_g[l])

        pc = _proj(xc, mod, ctx_row, n1g, win, qng, wuq, kvg, wukv, None, n_ctx, tm)
        pl_ = _proj(xl, mod, None, n1g, win, qng, wuq, kvg, wukv, tables, s, tm)
        qc, kc, vc, uc, dqc, dkc, dvc, gc = [a.reshape(b, n_ctx, -1) for a in pc]
        ql, kl, vl, ul, dql, dkl, dvl, gl = [a.reshape(b, s, -1) for a in pl_]

        wm, wc, wd = w_o_mla[l].astype(BF16), w_o_conv[l].astype(BF16), w_o_diff[l].astype(BF16)
        wout = w_out[l].astype(BF16)
        wr, br = _layout_router(w_rg[l], b_rg[l], w_re[l], b_re[l])
        n2g = row2(norm2_g[l])
        cw, cb = conv_w[l], row2(conv_b[l])
        cg, cbeta = row2(conv_ln_g[l]), row2(conv_ln_b[l])
        wg, wu, wdn = w_gate[l].astype(BF16), w_up[l].astype(BF16), w_down[l].astype(BF16)
        fg = row2(final_g)

        def tail(xs, om, oc, od, gates, mod_row, seq, tmoe, is_final):
            flat = lambda a: a.reshape(-1, a.shape[-1])
            xn, h2, comb = _merge(xs, flat(om), flat(oc), flat(od), flat(gates), mod, mod_row,
                                  wm, wc, wd, wout, n2g, wr, br, seq, tm)
            return _moe(h2, comb, xn, mod, mod_row, wg, wu, wdn, fg, seq, tmoe, is_final)

        om_l = _mla_attn(ql, kc, vc, kl, vl, tq, tk)
        od_l = _diff_attn(lam_p, subln, dql, dkc, dvc, dkl, dvl, tq, tk, lam_init)
        oc_l = _conv(ul, cw, cb, cg, cbeta, tr)
        xl_new = tail(xl, om_l, oc_l, od_l, gl, None, s, tmoe_l, final)
        if update_ctx:
            om_c = _mla_attn(qc, kc, vc, None, None, tq, tk)
            od_c = _diff_attn(lam_p, subln, dqc, dkc, dvc, None, None, tq, tk, lam_init)
            oc_c = _conv(uc, cw, cb, cg, cbeta, tr)
            xc = tail(xc, om_c, oc_c, od_c, gc, ctx_row, b * n_ctx, tmoe_c, False)
        xl = xl_new
    return xl.reshape(b, s, d)
```

```python
import functools
import math

import jax
import jax.numpy as jnp
from jax import lax
from jax.experimental import pallas as pl
from jax.experimental.pallas import tpu as pltpu

F32 = jnp.float32
BF16 = jnp.bfloat16

GRID_W = 64
ROPE_BASE = 10000.0
NORM_EPS = 1e-6

MLA_HEADS = 8
MLA_NOPE = 64
MLA_ROPE = 32
MLA_V = 64
Q_LORA = 256
KV_LORA = 128
CONV_CH = 512
CONV_WIDTH = 31
DIFF_HEADS = 4
DIFF_HD = 64
N_GROUPS = 4
EXPERTS_PER_GROUP = 8
N_EXPERTS = N_GROUPS * EXPERTS_PER_GROUP
EXPERT_FF = 256

LANES = 128
HALO = 16
VMEM_LIMIT = 48 * 1024 * 1024

C_CQ = 0
C_CKV = C_CQ + Q_LORA
C_KR = C_CKV + KV_LORA
C_GA = C_KR + LANES
C_GG = C_GA + CONV_CH
C_DQ = C_GG + CONV_CH
C_DK = C_DQ + 2 * DIFF_HEADS * DIFF_HD
C_GATES = C_DK + 2 * DIFF_HEADS * DIFF_HD

ONES_ROWS = 16
VT_A = MLA_V + ONES_ROWS
VT_D = 2 * DIFF_HD + ONES_ROWS
LOG2E = 1.4426950408889634
QK_LOOKAHEAD = 8


def _params(*sem):
    return pltpu.CompilerParams(dimension_semantics=sem, vmem_limit_bytes=VMEM_LIMIT)


def _sigmoid(x):
    return 0.5 * jnp.tanh(0.5 * x) + 0.5


def _rms(x, g):
    return x * lax.rsqrt(jnp.mean(x * x, axis=-1, keepdims=True) + NORM_EPS) * g


def _dot_nt(a, b):
    return lax.dot_general(a, b, (((1,), (1,)), ((), ())), preferred_element_type=F32)


def _const_spec(shape):
    return pl.BlockSpec(shape, lambda *_: (0,) * len(shape))


def _ada_kernel(c_ref, w_ref, b_ref, o_ref):
    c = c_ref[...]
    h = c * _sigmoid(c)
    o_ref[...] = jnp.dot(h, w_ref[...], preferred_element_type=F32,
                         precision=lax.Precision.HIGHEST) + b_ref[...]


def _ada(cc, w_ada, b_ada):
    depth, d, n = w_ada.shape
    bn = 1536
    rows = cc.shape[0]
    return pl.pallas_call(
        _ada_kernel,
        grid=(depth, n // bn),
        in_specs=[pl.BlockSpec((rows, d), lambda l, j: (0, 0)),
                  pl.BlockSpec((None, d, bn), lambda l, j: (l, 0, j)),
                  pl.BlockSpec((None, 1, bn), lambda l, j: (l, 0, j))],
        out_specs=pl.BlockSpec((None, rows, bn), lambda l, j: (l, 0, j)),
        out_shape=jax.ShapeDtypeStruct((depth, rows, n), F32),
        compiler_params=_params("parallel", "parallel"),
        name="adaln",
    )(cc, w_ada, b_ada.reshape(depth, 1, n))


def _swap_halves(t, half):
    n = t.shape[1]
    lane = lax.broadcasted_iota(jnp.int32, t.shape, 1) % (2 * half)
    return jnp.where(lane < half, pltpu.roll(t, n - half, 1), pltpu.roll(t, half, 1))


def _proj_kernel(*refs, rope, scale_a, scale_d):
    if rope:
        (x_ref, mod_ref, n1g_ref, win_ref, qng_ref, wuq_ref, kvg_ref, wuk_ref, wuvt_ref, wdvt_ref,
         ca_ref, sa_ref, cd_ref, sd_ref,
         q_ref, k_ref, vt_ref, u_ref, dq_ref, dk_ref, dvt_ref, g_ref) = refs
    else:
        (x_ref, mod_ref, n1g_ref, win_ref, qng_ref, wuq_ref, kvg_ref, wuk_ref, wuvt_ref, wdvt_ref,
         q_ref, k_ref, vt_ref, u_ref, dq_ref, dk_ref, dvt_ref, g_ref) = refs

    h = _rms(x_ref[...], n1g_ref[...]) * (1.0 + mod_ref[1:2, :]) + mod_ref[0:1, :]
    hb = h.astype(BF16)

    def proj(a, b):
        return jnp.dot(hb, win_ref[:, a:b], preferred_element_type=F32)

    def rope_a(t):
        return t * ca_ref[...] + _swap_halves(t, MLA_ROPE // 2) * sa_ref[...] if rope else t

    def rope_d(t):
        return t * cd_ref[...] + _swap_halves(t, DIFF_HD // 2) * sd_ref[...] if rope else t

    cq = _rms(proj(C_CQ, C_CKV), qng_ref[...]).astype(BF16)
    ckv = _rms(proj(C_CKV, C_KR), kvg_ref[...]).astype(BF16)
    kr = rope_a(proj(C_KR, C_GA))
    q = jnp.dot(cq, wuq_ref[...], preferred_element_type=F32)
    kn = jnp.dot(ckv, wuk_ref[...], preferred_element_type=F32)
    vt = _dot_nt(wuvt_ref[...], ckv)
    ones = jnp.ones((ONES_ROWS, vt.shape[1]), BF16)
    for hd in range(MLA_HEADS):
        blk = slice(hd * LANES, (hd + 1) * LANES)
        q_ref[:, blk] = (rope_a(q[:, blk]) * scale_a).astype(BF16)
        k_ref[:, blk] = (kn[:, blk] + kr).astype(BF16)
        vt_ref[hd * VT_A:hd * VT_A + MLA_V, :] = vt[hd * MLA_V:(hd + 1) * MLA_V, :].astype(BF16)
        vt_ref[hd * VT_A + MLA_V:(hd + 1) * VT_A, :] = ones

    u_ref[...] = proj(C_GA, C_GG) * _sigmoid(proj(C_GG, C_DQ))

    dq = proj(C_DQ, C_DK)
    dk = proj(C_DK, C_GATES)
    dvt = _dot_nt(wdvt_ref[...], hb)
    for hd in range(DIFF_HEADS):
        blk = slice(hd * LANES, (hd + 1) * LANES)
        dq_ref[:, blk] = (rope_d(dq[:, blk]) * scale_d).astype(BF16)
        dk_ref[:, blk] = rope_d(dk[:, blk]).astype(BF16)
        dvt_ref[hd * VT_D:hd * VT_D + LANES, :] = dvt[blk, :].astype(BF16)
        dvt_ref[hd * VT_D + LANES:(hd + 1) * VT_D, :] = ones

    d = x_ref.shape[1]
    for j in range(3):
        g_ref[:, j * d:(j + 1) * d] = _sigmoid(proj(C_GATES + j * d, C_GATES + (j + 1) * d)).astype(BF16)


def _proj(x, mod, mod_row, n1g, win, qng, wuq, kvg, wuk, wuvt, wdvt, tables, seq, tm):
    t, d = x.shape
    rope = tables is not None
    tiles_per_seq = seq // tm
    row = lambda i: (i, 0)
    col = lambda i: (0, i)
    if mod_row is None:
        mod_map = lambda i: (i // tiles_per_seq, 0, 0)
    else:
        mod_map = lambda i: (mod_row, 0, 0)
    consts = (n1g, win, qng, wuq, kvg, wuk, wuvt, wdvt)
    in_specs = [pl.BlockSpec((tm, d), row), pl.BlockSpec((None, 6, d), mod_map)]
    in_specs += [_const_spec(a.shape) for a in consts]
    args = [x, mod, *consts]
    if rope:
        in_specs += [pl.BlockSpec((tm, LANES), lambda i: (i % tiles_per_seq, 0))] * 4
        args += list(tables)
    outs = [((t, MLA_HEADS * LANES), (tm, MLA_HEADS * LANES), row, BF16),
            ((t, MLA_HEADS * LANES), (tm, MLA_HEADS * LANES), row, BF16),
            ((MLA_HEADS * VT_A, t), (MLA_HEADS * VT_A, tm), col, BF16),
            ((t, CONV_CH), (tm, CONV_CH), row, F32),
            ((t, DIFF_HEADS * LANES), (tm, DIFF_HEADS * LANES), row, BF16),
            ((t, DIFF_HEADS * LANES), (tm, DIFF_HEADS * LANES), row, BF16),
            ((DIFF_HEADS * VT_D, t), (DIFF_HEADS * VT_D, tm), col, BF16),
            ((t, 3 * d), (tm, 3 * d), row, BF16)]
    kern = functools.partial(_proj_kernel, rope=rope,
                             scale_a=LOG2E / math.sqrt(MLA_NOPE + MLA_ROPE), scale_d=LOG2E / math.sqrt(DIFF_HD))
    return pl.pallas_call(
        kern,
        grid=(t // tm,),
        in_specs=in_specs,
        out_specs=[pl.BlockSpec(blk, imap) for _, blk, imap, _ in outs],
        out_shape=[jax.ShapeDtypeStruct(shape, dt) for shape, _, _, dt in outs],
        compiler_params=_params("parallel"),
        name="proj_rope" if rope else "proj_ctx",
    )(*args)


def _flash_t(q, chunks):
    return _flash_streams([(q, chunks)])[0]


def _flash_streams(streams):
    n_chunks = len(streams[0][1])
    items = [(sid, c) for c in range(n_chunks) for sid in range(len(streams))]
    m = [None] * len(streams)
    acc = [None] * len(streams)

    def finish(sid, st, vt):
        cmax = jnp.max(st, axis=0, keepdims=True)
        m_new = cmax if m[sid] is None else jnp.maximum(m[sid], cmax)
        pt = jnp.exp2((st - m_new).astype(BF16))
        pv = jnp.dot(vt, pt, preferred_element_type=F32)
        acc[sid] = pv if m[sid] is None else acc[sid] * jnp.exp2(m[sid] - m_new) + pv
        m[sid] = m_new

    pending = []
    for sid, c in items:
        q, chunks = streams[sid]
        k, vt = chunks[c]
        st = _dot_nt(k, q)
        if len(pending) == QK_LOOKAHEAD:
            finish(*pending.pop(0))
        pending.append((sid, st, vt))
    for item in pending:
        finish(*item)
    return acc


def _key_chunks(kc_ref, vtc_ref, kl_ref, vtl_ref, tk, kcols, vrows):
    chunks = [(kc_ref[:, kcols], vtc_ref[vrows, :])]
    if kl_ref is not None:
        for j in range(kl_ref.shape[0] // tk):
            keys = slice(j * tk, (j + 1) * tk)
            chunks.append((kl_ref[keys, kcols], vtl_ref[vrows, keys]))
    return chunks


def _mla_attn_kernel(*refs, latent, tk):
    if latent:
        q_ref, kc_ref, vtc_ref, kl_ref, vtl_ref, o_ref = refs
    else:
        q_ref, kc_ref, vtc_ref, o_ref = refs
        kl_ref = vtl_ref = None
    streams = []
    for hh in range(2):
        cols = slice(hh * LANES, (hh + 1) * LANES)
        rows = slice(hh * VT_A, (hh + 1) * VT_A)
        streams.append((q_ref[:, cols], _key_chunks(kc_ref, vtc_ref, kl_ref, vtl_ref, tk, cols, rows)))
    outs = [acc[0:MLA_V, :] * (1.0 / acc[MLA_V:MLA_V + 1, :]) for acc in _flash_streams(streams)]
    o_ref[...] = jnp.concatenate(outs, axis=0).T.astype(BF16)


def _mla_attn(q, kc, vtc, kl, vtl, tq, tk):
    b, lq, _ = q.shape
    c = kc.shape[1]
    latent = kl is not None
    in_specs = [pl.BlockSpec((None, tq, 2 * LANES), lambda i, p, j: (i, j, p)),
                pl.BlockSpec((None, c, 2 * LANES), lambda i, p, j: (i, 0, p)),
                pl.BlockSpec((2 * VT_A, c), lambda i, p, j: (p, i))]
    args = [q, kc, vtc]
    if latent:
        s = kl.shape[1]
        in_specs += [pl.BlockSpec((None, s, 2 * LANES), lambda i, p, j: (i, 0, p)),
                     pl.BlockSpec((2 * VT_A, s), lambda i, p, j: (p, i))]
        args += [kl, vtl]
    return pl.pallas_call(
        functools.partial(_mla_attn_kernel, latent=latent, tk=tk),
        grid=(b, MLA_HEADS // 2, lq // tq),
        in_specs=in_specs,
        out_specs=pl.BlockSpec((None, tq, LANES), lambda i, p, j: (i, j, p)),
        out_shape=jax.ShapeDtypeStruct((b, lq, MLA_HEADS * MLA_V), BF16),
        compiler_params=_params("parallel", "parallel", "parallel"),
        name="mla_attn_lat" if latent else "mla_attn_ctx",
    )(*args)


def _diff_attn_kernel(*refs, latent, tk, lam_init):
    if latent:
        lam_ref, g_ref, q_ref, kc_ref, vtc_ref, kl_ref, vtl_ref, o_ref = refs
    else:
        lam_ref, g_ref, q_ref, kc_ref, vtc_ref, o_ref = refs
        kl_ref = vtl_ref = None
    q = q_ref[...]
    lane = lax.broadcasted_iota(jnp.int32, q.shape, 1)
    chunks = _key_chunks(kc_ref, vtc_ref, kl_ref, vtl_ref, tk, slice(None), slice(None))
    streams = [(jnp.where((lane < DIFF_HD) == first, q, jnp.zeros_like(q)), chunks) for first in (True, False)]
    outs = [acc[0:LANES, :] * (1.0 / acc[LANES:LANES + 1, :]) for acc in _flash_streams(streams)]
    lam = (jnp.exp(jnp.sum(lam_ref[0:1, :] * lam_ref[1:2, :], axis=-1, keepdims=True))
           - jnp.exp(jnp.sum(lam_ref[2:3, :] * lam_ref[3:4, :], axis=-1, keepdims=True)) + lam_init)
    o = outs[0] - lam * outs[1]
    o = o * lax.rsqrt(jnp.mean(o * o, axis=0, keepdims=True) + NORM_EPS) * (g_ref[...] * (1.0 - lam_init))
    o_ref[...] = o.T.astype(BF16)


def _diff_attn(lam_p, subln_g, q, kc, vtc, kl, vtl, tq, tk, lam_init):
    b, lq, _ = q.shape
    c = kc.shape[1]
    latent = kl is not None
    kblk = lambda rows: pl.BlockSpec((None, rows, LANES), lambda i, h, j: (i, 0, h))
    vblk = lambda keys: pl.BlockSpec((VT_D, keys), lambda i, h, j: (h, i))
    in_specs = [_const_spec(lam_p.shape), _const_spec(subln_g.shape),
                pl.BlockSpec((None, tq, LANES), lambda i, h, j: (i, j, h)), kblk(c), vblk(c)]
    args = [lam_p, subln_g, q, kc, vtc]
    if latent:
        in_specs += [kblk(kl.shape[1]), vblk(kl.shape[1])]
        args += [kl, vtl]
    return pl.pallas_call(
        functools.partial(_diff_attn_kernel, latent=latent, tk=tk, lam_init=lam_init),
        grid=(b, DIFF_HEADS, lq // tq),
        in_specs=in_specs,
        out_specs=pl.BlockSpec((None, tq, LANES), lambda i, h, j: (i, j, h)),
        out_shape=jax.ShapeDtypeStruct((b, lq, DIFF_HEADS * LANES), BF16),
        compiler_params=_params("parallel", "parallel", "parallel"),
        name="diff_attn_lat" if latent else "diff_attn_ctx",
    )(*args)


def _conv_kernel(prev_ref, cur_ref, next_ref, w_ref, b_ref, g_ref, beta_ref, o_ref, pad_ref):
    j = pl.program_id(1)
    tr = cur_ref.shape[0]
    zero = jnp.zeros((HALO, cur_ref.shape[1]), F32)
    pad_ref[0:HALO, :] = jnp.where(j == 0, zero, prev_ref[...])
    pad_ref[HALO:HALO + tr, :] = cur_ref[...]
    pad_ref[HALO + tr:, :] = jnp.where(j == pl.num_programs(1) - 1, zero, next_ref[...])
    off = HALO - CONV_WIDTH // 2
    acc = jnp.zeros(cur_ref.shape, F32) + b_ref[...]
    for tap in range(CONV_WIDTH):
        acc = acc + pad_ref[off + tap:off + tap + tr, :] * w_ref[tap:tap + 1, :]
    mu = jnp.mean(acc, axis=-1, keepdims=True)
    cen = acc - mu
    y = cen * lax.rsqrt(jnp.mean(cen * cen, axis=-1, keepdims=True) + NORM_EPS) * g_ref[...] + beta_ref[...]
    o_ref[...] = (y * _sigmoid(y)).astype(BF16)


def _conv(u, w, bias, g, beta, tr):
    b, l, ch = u.shape
    per = tr // HALO
    last = l // HALO - 1
    return pl.pallas_call(
        _conv_kernel,
        grid=(b, l // tr),
        in_specs=[pl.BlockSpec((None, HALO, ch), lambda i, j: (i, jnp.maximum(j * per - 1, 0), 0)),
                  pl.BlockSpec((None, tr, ch), lambda i, j: (i, j, 0)),
                  pl.BlockSpec((None, HALO, ch), lambda i, j: (i, jnp.minimum((j + 1) * per, last), 0)),
                  _const_spec(w.shape), _const_spec(bias.shape), _const_spec(g.shape), _const_spec(beta.shape)],
        out_specs=pl.BlockSpec((None, tr, ch), lambda i, j: (i, j, 0)),
        out_shape=jax.ShapeDtypeStruct((b, l, ch), BF16),
        scratch_shapes=[pltpu.VMEM((tr + 2 * HALO, ch), F32)],
        compiler_params=_params("parallel", "parallel"),
        name="conv_ln_silu",
    )(u, u, u, w, bias, g, beta)


def _route(logits):
    lane = lax.broadcasted_iota(jnp.int32, logits.shape, 1)
    neg = jnp.float32(-jnp.inf)
    big = jnp.int32(LANES)
    is_grp = (lane >= N_EXPERTS) & (lane < N_EXPERTS + N_GROUPS)
    gl = jnp.where(is_grp, logits, neg)
    gmax = jnp.max(gl, axis=-1, keepdims=True)
    gsel = jnp.min(jnp.where(gl == gmax, lane, big), axis=-1, keepdims=True) - N_EXPERTS
    p_g = 1.0 / jnp.sum(jnp.exp(gl - gmax), axis=-1, keepdims=True)
    in_grp = (lane >= gsel * EXPERTS_PER_GROUP) & (lane < (gsel + 1) * EXPERTS_PER_GROUP)
    e1 = jnp.where(in_grp, logits, neg)
    v1 = jnp.max(e1, axis=-1, keepdims=True)
    i1 = jnp.min(jnp.where(e1 == v1, lane, big), axis=-1, keepdims=True)
    e2 = jnp.where(lane == i1, neg, e1)
    v2 = jnp.max(e2, axis=-1, keepdims=True)
    i2 = jnp.min(jnp.where(e2 == v2, lane, big), axis=-1, keepdims=True)
    r = jnp.exp(v2 - v1)
    w1 = p_g / (1.0 + r)
    return jnp.where(lane == i1, w1, jnp.where(lane == i2, w1 * r, 0.0))


def _merge_kernel(x_ref, om_ref, oc_ref, od_ref, gates_ref, mod_ref, wm_ref, wc_ref, wd_ref, wout_ref,
                  n2g_ref, wr_ref, br_ref, xn_ref, h2_ref, comb_ref):
    d = x_ref.shape[1]
    gate = lambda j: gates_ref[:, j * d:(j + 1) * d].astype(F32)
    y = (gate(0) * jnp.dot(om_ref[...], wm_ref[...], preferred_element_type=F32)
         + gate(1) * jnp.dot(oc_ref[...], wc_ref[...], preferred_element_type=F32)
         + gate(2) * jnp.dot(od_ref[...], wd_ref[...], preferred_element_type=F32))
    z = jnp.dot(y.astype(BF16), wout_ref[...], preferred_element_type=F32)
    xn = x_ref[...] + mod_ref[2:3, :] * z
    xn_ref[...] = xn
    h2 = _rms(xn, n2g_ref[...]) * (1.0 + mod_ref[4:5, :]) + mod_ref[3:4, :]
    h_hi = h2.astype(BF16)
    h2_ref[...] = h_hi
    h_lo = (h2 - h_hi.astype(F32)).astype(BF16)
    hw = jnp.dot(h_hi, wr_ref[...], preferred_element_type=F32)
    lw = jnp.dot(h_lo, wr_ref[:, 0:LANES], preferred_element_type=F32)
    comb_ref[...] = _route(hw[:, 0:LANES] + hw[:, LANES:] + lw + br_ref[...])


def _merge(x, om, oc, od, gates, mod, mod_row, wm, wc, wd, wout, n2g, wr, br, seq, tm):
    t, d = x.shape
    tiles_per_seq = seq // tm
    row = lambda i: (i, 0)
    if mod_row is None:
        mod_map = lambda i: (i // tiles_per_seq, 0, 0)
    else:
        mod_map = lambda i: (mod_row, 0, 0)
    rows = lambda a: pl.BlockSpec((tm, a.shape[1]), row)
    return pl.pallas_call(
        _merge_kernel,
        grid=(t // tm,),
        in_specs=[rows(x), rows(om), rows(oc), rows(od), rows(gates),
                  pl.BlockSpec((None, 6, d), mod_map)]
                 + [_const_spec(a.shape) for a in (wm, wc, wd, wout, n2g, wr, br)],
        out_specs=[pl.BlockSpec((tm, d), row), pl.BlockSpec((tm, d), row), pl.BlockSpec((tm, LANES), row)],
        out_shape=[jax.ShapeDtypeStruct((t, d), F32), jax.ShapeDtypeStruct((t, d), BF16),
                   jax.ShapeDtypeStruct((t, LANES), F32)],
        compiler_params=_params("parallel"),
        name="merge_route",
    )(x, om, oc, od, gates, mod, wm, wc, wd, wout, n2g, wr, br)


def _moe_kernel(h_ref, comb_ref, xn_ref, mod_ref, wg_ref, wu_ref, wd_ref, fg_ref, o_ref, acc_ref, *, final):
    e = pl.program_id(1)

    @pl.when(e == 0)
    def _():
        acc_ref[...] = jnp.zeros_like(acc_ref)

    h = h_ref[...]
    a = jnp.dot(h, wg_ref[...], preferred_element_type=F32)
    u = jnp.dot(h, wu_ref[...], preferred_element_type=F32)
    comb = comb_ref[...]
    lane = lax.broadcasted_iota(jnp.int32, comb.shape, 1)
    cw = jnp.sum(jnp.where(lane == e, comb, 0.0), axis=-1, keepdims=True)
    hid = a * _sigmoid(a) * u * cw
    acc_ref[...] += jnp.dot(hid.astype(BF16), wd_ref[...], preferred_element_type=F32)

    @pl.when(e == pl.num_programs(1) - 1)
    def _():
        out = xn_ref[...] + mod_ref[5:6, :] * acc_ref[...]
        o_ref[...] = _rms(out, fg_ref[...]) if final else out


def _moe(h2, comb, xn, mod, mod_row, wg, wu, wd, final_g, seq, tm, final):
    t, d = xn.shape
    tiles_per_seq = seq // tm
    row = lambda i, e: (i, 0)
    if mod_row is None:
        mod_map = lambda i, e: (i // tiles_per_seq, 0, 0)
    else:
        mod_map = lambda i, e: (mod_row, 0, 0)
    ff = wg.shape[2]
    return pl.pallas_call(
        functools.partial(_moe_kernel, final=final),
        grid=(t // tm, N_EXPERTS),
        in_specs=[pl.BlockSpec((tm, d), row), pl.BlockSpec((tm, LANES), row), pl.BlockSpec((tm, d), row),
                  pl.BlockSpec((None, 6, d), mod_map),
                  pl.BlockSpec((None, d, ff), lambda i, e: (e, 0, 0)),
                  pl.BlockSpec((None, d, ff), lambda i, e: (e, 0, 0)),
                  pl.BlockSpec((None, ff, d), lambda i, e: (e, 0, 0)),
                  pl.BlockSpec(final_g.shape, lambda i, e: (0, 0))],
        out_specs=pl.BlockSpec((tm, d), row),
        out_shape=jax.ShapeDtypeStruct((t, d), F32),
        scratch_shapes=[pltpu.VMEM((tm, d), F32)],
        compiler_params=_params("parallel", "arbitrary"),
        name="moe",
    )(h2, comb, xn, mod, wg, wu, wd, final_g)


def _rope_tables(s):
    rows = s // GRID_W
    row = jnp.repeat(jnp.arange(rows, dtype=F32), GRID_W)
    col = jnp.tile(jnp.arange(GRID_W, dtype=F32), rows)

    def cos_sin(rot_dim):
        n = rot_dim // 4
        inv = ROPE_BASE ** (-jnp.arange(n, dtype=F32) / n)
        ang = jnp.concatenate([row[:, None] * inv, col[:, None] * inv], axis=-1)
        return jnp.cos(ang), jnp.sin(ang)

    ca, sa = cos_sin(MLA_ROPE)
    one = jnp.ones((s, MLA_NOPE), F32)
    pad1 = jnp.ones((s, LANES - MLA_NOPE - MLA_ROPE), F32)
    cos_a = jnp.concatenate([one, ca, ca, pad1], axis=-1)
    sin_a = jnp.concatenate([0 * one, -sa, sa, 0 * pad1], axis=-1)
    cd, sd = cos_sin(DIFF_HD)
    cos_d = jnp.concatenate([cd, cd, cd, cd], axis=-1)
    sin_d = jnp.concatenate([-sd, sd, -sd, sd], axis=-1)
    return cos_a, sin_a, cos_d, sin_d


def _layout_w_in(w_in):
    d = w_in.shape[0]
    o = 0
    parts = {}
    for name, width in (("cq", Q_LORA), ("ckv", KV_LORA), ("kr", MLA_ROPE), ("glu", 2 * CONV_CH),
                        ("dq", 2 * DIFF_HEADS * DIFF_HD), ("dk", 2 * DIFF_HEADS * DIFF_HD),
                        ("dv", 2 * DIFF_HEADS * DIFF_HD), ("gates", 3 * d)):
        parts[name] = w_in[:, o:o + width]
        o += width
    kr_blk = jnp.concatenate([jnp.zeros((d, MLA_NOPE), w_in.dtype), parts["kr"],
                              jnp.zeros((d, LANES - MLA_NOPE - MLA_ROPE), w_in.dtype)], axis=1)
    win = jnp.concatenate([parts["cq"], parts["ckv"], kr_blk, parts["glu"], parts["dq"], parts["dk"],
                           parts["gates"]], axis=1)
    return win.astype(BF16), parts["dv"].T.astype(BF16)


def _layout_w_uq(w_uq):
    r = w_uq.shape[0]
    w = w_uq.reshape(r, MLA_HEADS, MLA_NOPE + MLA_ROPE)
    w = jnp.pad(w, ((0, 0), (0, 0), (0, LANES - MLA_NOPE - MLA_ROPE)))
    return w.reshape(r, MLA_HEADS * LANES).astype(BF16)


def _layout_w_ukv(w_ukv):
    r = w_ukv.shape[0]
    w = w_ukv.reshape(r, MLA_HEADS, MLA_NOPE + MLA_V)
    wk = jnp.pad(w[:, :, :MLA_NOPE], ((0, 0), (0, 0), (0, LANES - MLA_NOPE))).reshape(r, MLA_HEADS * LANES)
    wvt = w[:, :, MLA_NOPE:].reshape(r, MLA_HEADS * MLA_V).T
    return wk.astype(BF16), wvt.astype(BF16)


def _layout_router(w_rg, b_rg, w_re, b_re):
    d = w_rg.shape[0]
    w = jnp.concatenate([w_re, w_rg, jnp.zeros((d, LANES - N_EXPERTS - N_GROUPS), F32)], axis=1)
    hi = w.astype(BF16)
    lo = (w - hi.astype(F32)).astype(BF16)
    b = jnp.concatenate([b_re, b_rg, jnp.zeros((LANES - N_EXPERTS - N_GROUPS,), F32)])[None, :]
    return jnp.concatenate([hi, lo], axis=1), b


def kernel(x, c, ctx, c_ctx, w_ada, b_ada, norm1_g, w_in, q_norm_g, w_uq, kv_norm_g, w_ukv, conv_w, conv_b,
           conv_ln_g, conv_ln_b, lam_q1, lam_k1, lam_q2, lam_k2, diff_subln_g, w_o_mla, w_o_conv, w_o_diff,
           w_out, norm2_g, w_rg, b_rg, w_re, b_re, w_gate, w_up, w_down, final_g):
    b, s, d = x.shape
    n_ctx = ctx.shape[1]
    depth = w_ada.shape[0]
    tm = 256
    tq = 256
    tk = min(256, s)
    tr = 256
    tmoe_l = min(1024, s)
    tmoe_c = min(1024, b * n_ctx)
    ctx_row = b

    rows = -(-(b + 1) // 8) * 8
    cc = jnp.zeros((rows, d), F32).at[:b].set(c).at[b].set(c_ctx)
    mod_all = _ada(cc, w_ada, b_ada).reshape(depth, rows, 6, d)
    tables = _rope_tables(s)
    row2 = lambda a: a[None, :]

    xl = x.reshape(b * s, d)
    xc = ctx.reshape(b * n_ctx, d)
    for l in range(depth):
        update_ctx = l < depth - 1
        final = l == depth - 1
        lam_init = 0.8 - 0.6 * math.exp(-0.3 * l)
        mod = mod_all[l]
        win, wdvt = _layout_w_in(w_in[l])
        wuq = _layout_w_uq(w_uq[l])
        wuk, wuvt = _layout_w_ukv(w_ukv[l])
        n1g, qng, kvg = row2(norm1_g[l]), row2(q_norm_g[l]), row2(kv_norm_g[l])
        lam_p = jnp.stack([lam_q1[l], lam_k1[l], lam_q2[l], lam_k2[l]])
        subln = diff_subln_g[l][:, None]

        pc = _proj(xc, mod, ctx_row, n1g, win, qng, wuq, kvg, wuk, wuvt, wdvt, None, n_ctx, tm)
        pl_ = _proj(xl, mod, None, n1g, win, qng, wuq, kvg, wuk, wuvt, wdvt, tables, s, tm)
        per_batch = lambda arrs, n: [a if i in (2, 6) else a.reshape(b, n, -1) for i, a in enumerate(arrs)]
        qc, kc, vc, uc, dqc, dkc, dvc, gc = per_batch(pc, n_ctx)
        ql, kl, vl, ul, dql, dkl, dvl, gl = per_batch(pl_, s)

        wm, wc, wd = w_o_mla[l].astype(BF16), w_o_conv[l].astype(BF16), w_o_diff[l].astype(BF16)
        wout = w_out[l].astype(BF16)
        wr, br = _layout_router(w_rg[l], b_rg[l], w_re[l], b_re[l])
        n2g = row2(norm2_g[l])
        cw, cb = conv_w[l], row2(conv_b[l])
        cg, cbeta = row2(conv_ln_g[l]), row2(conv_ln_b[l])
        wg, wu, wdn = w_gate[l].astype(BF16), w_up[l].astype(BF16), w_down[l].astype(BF16)
        fg = row2(final_g)

        def tail(xs, om, oc, od, gates, mod_row, seq, tmoe, is_final):
            flat = lambda a: a.reshape(-1, a.shape[-1])
            xn, h2, comb = _merge(xs, flat(om), flat(oc), flat(od), flat(gates), mod, mod_row,
                                  wm, wc, wd, wout, n2g, wr, br, seq, tm)
            return _moe(h2, comb, xn, mod, mod_row, wg, wu, wdn, fg, seq, tmoe, is_final)

        om_l = _mla_attn(ql, kc, vc, kl, vl, tq, tk)
        od_l = _diff_attn(lam_p, subln, dql, dkc, dvc, dkl, dvl, tq, tk, lam_init)
        oc_l = _conv(ul, cw, cb, cg, cbeta, tr)
        xl_new = tail(xl, om_l, oc_l, od_l, gl, None, s, tmoe_l, final)
        if update_ctx:
            om_c = _mla_attn(qc, kc, vc, None, None, tq, tk)
            od_c = _diff_attn(lam_p, subln, dqc, dkc, dvc, None, None, tq, tk, lam_init)
            oc_c = _conv(uc, cw, cb, cg, cbeta, tr)
            xc = tail(xc, om_c, oc_c, od_c, gc, ctx_row, b * n_ctx, tmoe_c, False)
        xl = xl_new
    return xl.reshape(b, s, d)
```

```python
import functools
import math

import jax
import jax.numpy as jnp
from jax import lax
from jax.experimental import pallas as pl
from jax.experimental.pallas import tpu as pltpu

F32 = jnp.float32
BF16 = jnp.bfloat16

GRID_W = 64
ROPE_BASE = 10000.0
NORM_EPS = 1e-6

MLA_HEADS = 8
MLA_NOPE = 64
MLA_ROPE = 32
MLA_V = 64
Q_LORA = 256
KV_LORA = 128
CONV_CH = 512
CONV_WIDTH = 31
DIFF_HEADS = 4
DIFF_HD = 64
N_GROUPS = 4
EXPERTS_PER_GROUP = 8
N_EXPERTS = N_GROUPS * EXPERTS_PER_GROUP
EXPERT_FF = 256

LANES = 128
SUBLANES = 8
HALO = 16
VMEM_LIMIT = 48 * 1024 * 1024

C_CQ = 0
C_CKV = C_CQ + Q_LORA
C_KR = C_CKV + KV_LORA
C_GA = C_KR + LANES
C_GG = C_GA + CONV_CH
C_DQ = C_GG + CONV_CH
C_DK = C_DQ + 2 * DIFF_HEADS * DIFF_HD
C_GATES = C_DK + 2 * DIFF_HEADS * DIFF_HD

ONES_ROWS = 16
VT_A = MLA_V + ONES_ROWS
VT_D = 2 * DIFF_HD + ONES_ROWS
LOG2E = 1.4426950408889634
GROUP_LANE = N_EXPERTS
MOE_CHUNK = 320
MOE_VMEM_LIMIT = 56 * 1024 * 1024
Q_SUB = 256
QK_LOOKAHEAD = 8


def _params(*sem):
    return pltpu.CompilerParams(dimension_semantics=sem, vmem_limit_bytes=VMEM_LIMIT)


def _sigmoid(x):
    return 0.5 * jnp.tanh(0.5 * x) + 0.5


def _rms(x, g):
    return x * lax.rsqrt(jnp.mean(x * x, axis=-1, keepdims=True) + NORM_EPS) * g


def _dot_nt(a, b):
    return lax.dot_general(a, b, (((1,), (1,)), ((), ())), preferred_element_type=F32)


def _const_spec(shape):
    return pl.BlockSpec(shape, lambda *_: (0,) * len(shape))


def _ada_kernel(c_ref, w_ref, b_ref, o_ref):
    c = c_ref[...]
    h = c * _sigmoid(c)
    o_ref[...] = jnp.dot(h, w_ref[...], preferred_element_type=F32,
                         precision=lax.Precision.HIGHEST) + b_ref[...]


def _ada(cc, w_ada, b_ada):
    depth, d, n = w_ada.shape
    bn = 1536
    rows = cc.shape[0]
    return pl.pallas_call(
        _ada_kernel,
        grid=(depth, n // bn),
        in_specs=[pl.BlockSpec((rows, d), lambda l, j: (0, 0)),
                  pl.BlockSpec((None, d, bn), lambda l, j: (l, 0, j)),
                  pl.BlockSpec((None, 1, bn), lambda l, j: (l, 0, j))],
        out_specs=pl.BlockSpec((None, rows, bn), lambda l, j: (l, 0, j)),
        out_shape=jax.ShapeDtypeStruct((depth, rows, n), F32),
        compiler_params=_params("parallel", "parallel"),
        name="adaln",
    )(cc, w_ada, b_ada.reshape(depth, 1, n))


def _swap_halves(t, half):
    n = t.shape[1]
    lane = lax.broadcasted_iota(jnp.int32, t.shape, 1) % (2 * half)
    return jnp.where(lane < half, pltpu.roll(t, n - half, 1), pltpu.roll(t, half, 1))


def _proj_kernel(*refs, rope, scale_a, scale_d):
    if rope:
        (x_ref, mod_ref, n1g_ref, win_ref, qng_ref, wuq_ref, kvg_ref, wuk_ref, wuvt_ref, wdvt_ref,
         ca_ref, sa_ref, cd_ref, sd_ref,
         q_ref, k_ref, vt_ref, u_ref, dq_ref, dk_ref, dvt_ref, g_ref) = refs
    else:
        (x_ref, mod_ref, n1g_ref, win_ref, qng_ref, wuq_ref, kvg_ref, wuk_ref, wuvt_ref, wdvt_ref,
         q_ref, k_ref, vt_ref, u_ref, dq_ref, dk_ref, dvt_ref, g_ref) = refs

    h = _rms(x_ref[...], n1g_ref[...]) * (1.0 + mod_ref[1:2, :]) + mod_ref[0:1, :]
    hb = h.astype(BF16)

    def proj(a, b):
        return jnp.dot(hb, win_ref[:, a:b], preferred_element_type=F32)

    def rope_a(t):
        return t * ca_ref[...] + _swap_halves(t, MLA_ROPE // 2) * sa_ref[...] if rope else t

    def rope_d(t):
        return t * cd_ref[...] + _swap_halves(t, DIFF_HD // 2) * sd_ref[...] if rope else t

    cq = _rms(proj(C_CQ, C_CKV), qng_ref[...]).astype(BF16)
    ckv = _rms(proj(C_CKV, C_KR), kvg_ref[...]).astype(BF16)
    kr = rope_a(proj(C_KR, C_GA))
    q = jnp.dot(cq, wuq_ref[...], preferred_element_type=F32)
    kn = jnp.dot(ckv, wuk_ref[...], preferred_element_type=F32)
    vt = _dot_nt(wuvt_ref[...], ckv)
    ones = jnp.ones((ONES_ROWS, vt.shape[1]), BF16)
    for hd in range(MLA_HEADS):
        blk = slice(hd * LANES, (hd + 1) * LANES)
        q_ref[:, blk] = (rope_a(q[:, blk]) * scale_a).astype(BF16)
        k_ref[:, blk] = (kn[:, blk] + kr).astype(BF16)
        vt_ref[hd * VT_A:hd * VT_A + MLA_V, :] = vt[hd * MLA_V:(hd + 1) * MLA_V, :].astype(BF16)
        vt_ref[hd * VT_A + MLA_V:(hd + 1) * VT_A, :] = ones

    u_ref[...] = proj(C_GA, C_GG) * _sigmoid(proj(C_GG, C_DQ))

    dq = proj(C_DQ, C_DK)
    dk = proj(C_DK, C_GATES)
    dvt = _dot_nt(wdvt_ref[...], hb)
    for hd in range(DIFF_HEADS):
        blk = slice(hd * LANES, (hd + 1) * LANES)
        dq_ref[:, blk] = (rope_d(dq[:, blk]) * scale_d).astype(BF16)
        dk_ref[:, blk] = rope_d(dk[:, blk]).astype(BF16)
        dvt_ref[hd * VT_D:hd * VT_D + LANES, :] = dvt[blk, :].astype(BF16)
        dvt_ref[hd * VT_D + LANES:(hd + 1) * VT_D, :] = ones

    d = x_ref.shape[1]
    for j in range(3):
        g_ref[:, j * d:(j + 1) * d] = _sigmoid(proj(C_GATES + j * d, C_GATES + (j + 1) * d)).astype(BF16)


def _proj(x, mod, mod_row, n1g, win, qng, wuq, kvg, wuk, wuvt, wdvt, tables, seq, tm):
    t, d = x.shape
    rope = tables is not None
    tiles_per_seq = seq // tm
    row = lambda i: (i, 0)
    col = lambda i: (0, i)
    if mod_row is None:
        mod_map = lambda i: (i // tiles_per_seq, 0, 0)
    else:
        mod_map = lambda i: (mod_row, 0, 0)
    consts = (n1g, win, qng, wuq, kvg, wuk, wuvt, wdvt)
    in_specs = [pl.BlockSpec((tm, d), row), pl.BlockSpec((None, 6, d), mod_map)]
    in_specs += [_const_spec(a.shape) for a in consts]
    args = [x, mod, *consts]
    if rope:
        in_specs += [pl.BlockSpec((tm, LANES), lambda i: (i % tiles_per_seq, 0))] * 4
        args += list(tables)
    outs = [((t, MLA_HEADS * LANES), (tm, MLA_HEADS * LANES), row, BF16),
            ((t, MLA_HEADS * LANES), (tm, MLA_HEADS * LANES), row, BF16),
            ((MLA_HEADS * VT_A, t), (MLA_HEADS * VT_A, tm), col, BF16),
            ((t, CONV_CH), (tm, CONV_CH), row, F32),
            ((t, DIFF_HEADS * LANES), (tm, DIFF_HEADS * LANES), row, BF16),
            ((t, DIFF_HEADS * LANES), (tm, DIFF_HEADS * LANES), row, BF16),
            ((DIFF_HEADS * VT_D, t), (DIFF_HEADS * VT_D, tm), col, BF16),
            ((t, 3 * d), (tm, 3 * d), row, BF16)]
    kern = functools.partial(_proj_kernel, rope=rope,
                             scale_a=LOG2E / math.sqrt(MLA_NOPE + MLA_ROPE), scale_d=LOG2E / math.sqrt(DIFF_HD))
    return pl.pallas_call(
        kern,
        grid=(t // tm,),
        in_specs=in_specs,
        out_specs=[pl.BlockSpec(blk, imap) for _, blk, imap, _ in outs],
        out_shape=[jax.ShapeDtypeStruct(shape, dt) for shape, _, _, dt in outs],
        compiler_params=_params("parallel"),
        name="proj_rope" if rope else "proj_ctx",
    )(*args)


def _flash_t(q, chunks):
    return _flash_streams([(q, chunks)])[0]


def _flash_streams(streams):
    n_chunks = len(streams[0][1])
    items = [(sid, c) for c in range(n_chunks) for sid in range(len(streams))]
    m = [None] * len(streams)
    acc = [None] * len(streams)

    def finish(sid, st, vt):
        cmax = jnp.max(st, axis=0, keepdims=True)
        m_new = cmax if m[sid] is None else jnp.maximum(m[sid], cmax)
        pt = jnp.exp2((st - m_new).astype(BF16))
        pv = jnp.dot(vt, pt, preferred_element_type=F32)
        acc[sid] = pv if m[sid] is None else acc[sid] * jnp.exp2(m[sid] - m_new) + pv
        m[sid] = m_new

    pending = []
    for sid, c in items:
        q, chunks = streams[sid]
        k, vt = chunks[c]
        st = _dot_nt(k, q)
        if len(pending) == QK_LOOKAHEAD:
            finish(*pending.pop(0))
        pending.append((sid, st, vt))
    for item in pending:
        finish(*item)
    return acc


def _key_chunks(kc_ref, vtc_ref, kl_ref, vtl_ref, tk, kcols, vrows):
    chunks = [(kc_ref[:, kcols], vtc_ref[vrows, :])]
    if kl_ref is not None:
        for j in range(kl_ref.shape[0] // tk):
            keys = slice(j * tk, (j + 1) * tk)
            chunks.append((kl_ref[keys, kcols], vtl_ref[vrows, keys]))
    return chunks


def _mla_attn_kernel(*refs, latent, tk):
    if latent:
        q_ref, kc_ref, vtc_ref, kl_ref, vtl_ref, o_ref = refs
    else:
        q_ref, kc_ref, vtc_ref, o_ref = refs
        kl_ref = vtl_ref = None
    subs = [slice(r, r + Q_SUB) for r in range(0, q_ref.shape[0], Q_SUB)]
    streams = []
    for qs in subs:
        for hh in range(2):
            cols = slice(hh * LANES, (hh + 1) * LANES)
            rows = slice(hh * VT_A, (hh + 1) * VT_A)
            streams.append((q_ref[qs, cols], _key_chunks(kc_ref, vtc_ref, kl_ref, vtl_ref, tk, cols, rows)))
    outs = [acc[0:MLA_V, :] * (1.0 / acc[MLA_V:MLA_V + 1, :]) for acc in _flash_streams(streams)]
    for i, qs in enumerate(subs):
        o_ref[qs, :] = jnp.concatenate(outs[2 * i:2 * i + 2], axis=0).T.astype(BF16)


def _mla_attn(q, kc, vtc, kl, vtl, tq, tk):
    b, lq, _ = q.shape
    c = kc.shape[1]
    latent = kl is not None
    in_specs = [pl.BlockSpec((None, tq, 2 * LANES), lambda i, p, j: (i, j, p)),
                pl.BlockSpec((None, c, 2 * LANES), lambda i, p, j: (i, 0, p)),
                pl.BlockSpec((2 * VT_A, c), lambda i, p, j: (p, i))]
    args = [q, kc, vtc]
    if latent:
        s = kl.shape[1]
        in_specs += [pl.BlockSpec((None, s, 2 * LANES), lambda i, p, j: (i, 0, p)),
                     pl.BlockSpec((2 * VT_A, s), lambda i, p, j: (p, i))]
        args += [kl, vtl]
    return pl.pallas_call(
        functools.partial(_mla_attn_kernel, latent=latent, tk=tk),
        grid=(b, MLA_HEADS // 2, lq // tq),
        in_specs=in_specs,
        out_specs=pl.BlockSpec((None, tq, LANES), lambda i, p, j: (i, j, p)),
        out_shape=jax.ShapeDtypeStruct((b, lq, MLA_HEADS * MLA_V), BF16),
        compiler_params=_params("parallel", "parallel", "parallel"),
        name="mla_attn_lat" if latent else "mla_attn_ctx",
    )(*args)


def _diff_attn_kernel(*refs, latent, tk, lam_init):
    if latent:
        lam_ref, g_ref, q_ref, kc_ref, vtc_ref, kl_ref, vtl_ref, o_ref = refs
    else:
        lam_ref, g_ref, q_ref, kc_ref, vtc_ref, o_ref = refs
        kl_ref = vtl_ref = None
    subs = [slice(r, r + Q_SUB) for r in range(0, q_ref.shape[0], Q_SUB)]
    chunks = _key_chunks(kc_ref, vtc_ref, kl_ref, vtl_ref, tk, slice(None), slice(None))
    streams = []
    for qs in subs:
        q = q_ref[qs, :]
        lane = lax.broadcasted_iota(jnp.int32, q.shape, 1)
        for first in (True, False):
            streams.append((jnp.where((lane < DIFF_HD) == first, q, jnp.zeros_like(q)), chunks))
    outs = [acc[0:LANES, :] * (1.0 / acc[LANES:LANES + 1, :]) for acc in _flash_streams(streams)]
    lam = (jnp.exp(jnp.sum(lam_ref[0:1, :] * lam_ref[1:2, :], axis=-1, keepdims=True))
           - jnp.exp(jnp.sum(lam_ref[2:3, :] * lam_ref[3:4, :], axis=-1, keepdims=True)) + lam_init)
    for i, qs in enumerate(subs):
        o = outs[2 * i] - lam * outs[2 * i + 1]
        o = o * lax.rsqrt(jnp.mean(o * o, axis=0, keepdims=True) + NORM_EPS) * (g_ref[...] * (1.0 - lam_init))
        o_ref[qs, :] = o.T.astype(BF16)


def _diff_attn(lam_p, subln_g, q, kc, vtc, kl, vtl, tq, tk, lam_init):
    b, lq, _ = q.shape
    c = kc.shape[1]
    latent = kl is not None
    kblk = lambda rows: pl.BlockSpec((None, rows, LANES), lambda i, h, j: (i, 0, h))
    vblk = lambda keys: pl.BlockSpec((VT_D, keys), lambda i, h, j: (h, i))
    in_specs = [_const_spec(lam_p.shape), _const_spec(subln_g.shape),
                pl.BlockSpec((None, tq, LANES), lambda i, h, j: (i, j, h)), kblk(c), vblk(c)]
    args = [lam_p, subln_g, q, kc, vtc]
    if latent:
        in_specs += [kblk(kl.shape[1]), vblk(kl.shape[1])]
        args += [kl, vtl]
    return pl.pallas_call(
        functools.partial(_diff_attn_kernel, latent=latent, tk=tk, lam_init=lam_init),
        grid=(b, DIFF_HEADS, lq // tq),
        in_specs=in_specs,
        out_specs=pl.BlockSpec((None, tq, LANES), lambda i, h, j: (i, j, h)),
        out_shape=jax.ShapeDtypeStruct((b, lq, DIFF_HEADS * LANES), BF16),
        compiler_params=_params("parallel", "parallel", "parallel"),
        name="diff_attn_lat" if latent else "diff_attn_ctx",
    )(*args)


def _conv_kernel(prev_ref, cur_ref, next_ref, w_ref, b_ref, g_ref, beta_ref, o_ref, pad_ref, shift_ref):
    j = pl.program_id(1)
    tr = cur_ref.shape[0]
    zero = jnp.zeros((HALO, cur_ref.shape[1]), F32)
    pad_ref[0:HALO, :] = jnp.where(j == 0, zero, prev_ref[...])
    pad_ref[HALO:HALO + tr, :] = cur_ref[...]
    pad_ref[HALO + tr:, :] = jnp.where(j == pl.num_programs(1) - 1, zero, next_ref[...])
    span = shift_ref.shape[1]
    for phase in range(1, SUBLANES):
        shift_ref[phase] = pad_ref[phase:phase + span, :]
    off = HALO - CONV_WIDTH // 2
    acc = jnp.zeros(cur_ref.shape, F32) + b_ref[...]
    for tap in range(CONV_WIDTH):
        phase = (off + tap) % SUBLANES
        base = off + tap - phase
        win = pad_ref[base:base + tr, :] if phase == 0 else shift_ref[phase, base:base + tr, :]
        acc = acc + win * w_ref[tap:tap + 1, :]
    mu = jnp.mean(acc, axis=-1, keepdims=True)
    cen = acc - mu
    y = cen * lax.rsqrt(jnp.mean(cen * cen, axis=-1, keepdims=True) + NORM_EPS) * g_ref[...] + beta_ref[...]
    o_ref[...] = (y * _sigmoid(y)).astype(BF16)


def _conv(u, w, bias, g, beta, tr):
    b, l, ch = u.shape
    per = tr // HALO
    last = l // HALO - 1
    return pl.pallas_call(
        _conv_kernel,
        grid=(b, l // tr),
        in_specs=[pl.BlockSpec((None, HALO, ch), lambda i, j: (i, jnp.maximum(j * per - 1, 0), 0)),
                  pl.BlockSpec((None, tr, ch), lambda i, j: (i, j, 0)),
                  pl.BlockSpec((None, HALO, ch), lambda i, j: (i, jnp.minimum((j + 1) * per, last), 0)),
                  _const_spec(w.shape), _const_spec(bias.shape), _const_spec(g.shape), _const_spec(beta.shape)],
        out_specs=pl.BlockSpec((None, tr, ch), lambda i, j: (i, j, 0)),
        out_shape=jax.ShapeDtypeStruct((b, l, ch), BF16),
        scratch_shapes=[pltpu.VMEM((tr + 2 * HALO, ch), F32),
                        pltpu.VMEM((SUBLANES, tr + 2 * HALO - SUBLANES, ch), F32)],
        compiler_params=_params("parallel", "parallel"),
        name="conv_ln_silu",
    )(u, u, u, w, bias, g, beta)


def _route(logits):
    lane = lax.broadcasted_iota(jnp.int32, logits.shape, 1)
    neg = jnp.float32(-jnp.inf)
    big = jnp.int32(LANES)
    is_grp = (lane >= N_EXPERTS) & (lane < N_EXPERTS + N_GROUPS)
    gl = jnp.where(is_grp, logits, neg)
    gmax = jnp.max(gl, axis=-1, keepdims=True)
    gsel = jnp.min(jnp.where(gl == gmax, lane, big), axis=-1, keepdims=True) - N_EXPERTS
    p_g = 1.0 / jnp.sum(jnp.exp(gl - gmax), axis=-1, keepdims=True)
    in_grp = (lane >= gsel * EXPERTS_PER_GROUP) & (lane < (gsel + 1) * EXPERTS_PER_GROUP)
    e1 = jnp.where(in_grp, logits, neg)
    v1 = jnp.max(e1, axis=-1, keepdims=True)
    i1 = jnp.min(jnp.where(e1 == v1, lane, big), axis=-1, keepdims=True)
    e2 = jnp.where(lane == i1, neg, e1)
    v2 = jnp.max(e2, axis=-1, keepdims=True)
    i2 = jnp.min(jnp.where(e2 == v2, lane, big), axis=-1, keepdims=True)
    r = jnp.exp(v2 - v1)
    w1 = p_g / (1.0 + r)
    comb = jnp.where(lane == i1, w1, jnp.where(lane == i2, w1 * r, 0.0))
    return jnp.where(lane == GROUP_LANE, gsel.astype(F32), comb)


def _merge_kernel(x_ref, om_ref, oc_ref, od_ref, gates_ref, mod_ref, wm_ref, wc_ref, wd_ref, wout_ref,
                  n2g_ref, wr_ref, br_ref, xn_ref, h2_ref, comb_ref):
    d = x_ref.shape[1]
    gate = lambda j: gates_ref[:, j * d:(j + 1) * d].astype(F32)
    y = (gate(0) * jnp.dot(om_ref[...], wm_ref[...], preferred_element_type=F32)
         + gate(1) * jnp.dot(oc_ref[...], wc_ref[...], preferred_element_type=F32)
         + gate(2) * jnp.dot(od_ref[...], wd_ref[...], preferred_element_type=F32))
    z = jnp.dot(y.astype(BF16), wout_ref[...], preferred_element_type=F32)
    xn = x_ref[...] + mod_ref[2:3, :] * z
    xn_ref[...] = xn
    h2 = _rms(xn, n2g_ref[...]) * (1.0 + mod_ref[4:5, :]) + mod_ref[3:4, :]
    h_hi = h2.astype(BF16)
    h2_ref[...] = h_hi
    h_lo = (h2 - h_hi.astype(F32)).astype(BF16)
    hw = jnp.dot(h_hi, wr_ref[...], preferred_element_type=F32)
    lw = jnp.dot(h_lo, wr_ref[:, 0:LANES], preferred_element_type=F32)
    comb_ref[...] = _route(hw[:, 0:LANES] + hw[:, LANES:] + lw + br_ref[...])


def _merge(x, om, oc, od, gates, mod, mod_row, wm, wc, wd, wout, n2g, wr, br, seq, tm):
    t, d = x.shape
    tiles_per_seq = seq // tm
    row = lambda i: (i, 0)
    if mod_row is None:
        mod_map = lambda i: (i // tiles_per_seq, 0, 0)
    else:
        mod_map = lambda i: (mod_row, 0, 0)
    rows = lambda a: pl.BlockSpec((tm, a.shape[1]), row)
    return pl.pallas_call(
        _merge_kernel,
        grid=(t // tm,),
        in_specs=[rows(x), rows(om), rows(oc), rows(od), rows(gates),
                  pl.BlockSpec((None, 6, d), mod_map)]
                 + [_const_spec(a.shape) for a in (wm, wc, wd, wout, n2g, wr, br)],
        out_specs=[pl.BlockSpec((tm, d), row), pl.BlockSpec((tm, d), row), pl.BlockSpec((tm, LANES), row)],
        out_shape=[jax.ShapeDtypeStruct((t, d), F32), jax.ShapeDtypeStruct((t, d), BF16),
                   jax.ShapeDtypeStruct((t, LANES), F32)],
        compiler_params=_params("parallel"),
        name="merge_route",
    )(x, om, oc, od, gates, mod, wm, wc, wd, wout, n2g, wr, br)


def _moe_kernel(h_ref, comb_ref, xn_ref, mod_ref, tri_ref, wg_ref, wu_ref, wd_ref, fg_ref, o_ref,
                pos_ref, comb3_ref, *, final):
    g = pl.program_id(1)
    tm = h_ref.shape[0]

    @pl.when(g == 0)
    def _():
        o_ref[...] = jnp.zeros_like(o_ref)
        comb = comb_ref[...]
        hi = comb.astype(BF16)
        r1 = comb - hi.astype(F32)
        mid = r1.astype(BF16)
        lo = (r1 - mid.astype(F32)).astype(BF16)
        comb3_ref[...] = jnp.concatenate([hi, mid, lo], axis=1)
        grow = comb.T[GROUP_LANE:GROUP_LANE + 1, :]
        gid = lax.broadcasted_iota(jnp.int32, (8, tm), 0).astype(F32)
        member = grow == gid
        rank = jnp.dot(jnp.where(member, 1.0, 0.0).astype(BF16), tri_ref[...], preferred_element_type=F32)
        pos_ref[...] = jnp.where(member, rank, -1.0)

    row = lax.broadcasted_iota(jnp.int32, (8, tm), 0)
    posg = jnp.sum(jnp.where(row == g, pos_ref[...], 0.0), axis=0, keepdims=True)
    n_chunks = (jnp.max(posg).astype(jnp.int32) + MOE_CHUNK) // MOE_CHUNK

    def chunk(c, carry):
        slot = (lax.broadcasted_iota(jnp.int32, (MOE_CHUNK, tm), 0) + c * MOE_CHUNK).astype(F32)
        sel = jnp.where(posg == slot, 1.0, 0.0).astype(BF16)
        x = jnp.dot(sel, h_ref[...], preferred_element_type=F32).astype(BF16)
        cw3 = jnp.dot(sel, comb3_ref[...], preferred_element_type=F32)
        cw = cw3[:, 0:LANES] + cw3[:, LANES:2 * LANES] + cw3[:, 2 * LANES:]
        lane = lax.broadcasted_iota(jnp.int32, cw.shape, 1)
        y = jnp.zeros((MOE_CHUNK, o_ref.shape[1]), F32)
        for e in range(EXPERTS_PER_GROUP):
            a = jnp.dot(x, wg_ref[e], preferred_element_type=F32)
            u = jnp.dot(x, wu_ref[e], preferred_element_type=F32)
            w = jnp.sum(jnp.where(lane == g * EXPERTS_PER_GROUP + e, cw, 0.0), axis=1, keepdims=True)
            hid = a * _sigmoid(a) * u * w
            y = y + jnp.dot(hid.astype(BF16), wd_ref[e], preferred_element_type=F32)
        o_ref[...] += lax.dot_general(sel, y.astype(BF16), (((0,), (0,)), ((), ())), preferred_element_type=F32)
        return carry

    lax.fori_loop(0, n_chunks, chunk, 0)

    @pl.when(g == pl.num_programs(1) - 1)
    def _():
        out = xn_ref[...] + mod_ref[5:6, :] * o_ref[...]
        o_ref[...] = _rms(out, fg_ref[...]) if final else out


def _moe(h2, comb, xn, mod, mod_row, wg, wu, wd, final_g, seq, tm, final):
    t, d = xn.shape
    tiles_per_seq = seq // tm
    row = lambda i, g: (i, 0)
    if mod_row is None:
        mod_map = lambda i, g: (i // tiles_per_seq, 0, 0)
    else:
        mod_map = lambda i, g: (mod_row, 0, 0)
    ff = wg.shape[2]
    idx = jnp.arange(tm)
    tri = (idx[:, None] < idx[None, :]).astype(BF16)
    grp = lambda i, g: (g, 0, 0)
    once = pl.Buffered(1)
    return pl.pallas_call(
        functools.partial(_moe_kernel, final=final),
        grid=(t // tm, N_GROUPS),
        in_specs=[pl.BlockSpec((tm, d), row), pl.BlockSpec((tm, LANES), row),
                  pl.BlockSpec((tm, d), row, pipeline_mode=once),
                  pl.BlockSpec((None, 6, d), mod_map),
                  pl.BlockSpec((tm, tm), lambda i, g: (0, 0), pipeline_mode=once),
                  pl.BlockSpec((EXPERTS_PER_GROUP, d, ff), grp),
                  pl.BlockSpec((EXPERTS_PER_GROUP, d, ff), grp),
                  pl.BlockSpec((EXPERTS_PER_GROUP, ff, d), grp),
                  pl.BlockSpec(final_g.shape, lambda i, g: (0, 0))],
        out_specs=pl.BlockSpec((tm, d), row),
        out_shape=jax.ShapeDtypeStruct((t, d), F32),
        scratch_shapes=[pltpu.VMEM((8, tm), F32), pltpu.VMEM((tm, 3 * LANES), BF16)],
        compiler_params=pltpu.CompilerParams(dimension_semantics=("parallel", "arbitrary"),
                                             vmem_limit_bytes=MOE_VMEM_LIMIT),
        name="moe",
    )(h2, comb, xn, mod, tri, wg, wu, wd, final_g)


def _rope_tables(s):
    rows = s // GRID_W
    row = jnp.repeat(jnp.arange(rows, dtype=F32), GRID_W)
    col = jnp.tile(jnp.arange(GRID_W, dtype=F32), rows)

    def cos_sin(rot_dim):
        n = rot_dim // 4
        inv = ROPE_BASE ** (-jnp.arange(n, dtype=F32) / n)
        ang = jnp.concatenate([row[:, None] * inv, col[:, None] * inv], axis=-1)
        return jnp.cos(ang), jnp.sin(ang)

    ca, sa = cos_sin(MLA_ROPE)
    one = jnp.ones((s, MLA_NOPE), F32)
    pad1 = jnp.ones((s, LANES - MLA_NOPE - MLA_ROPE), F32)
    cos_a = jnp.concatenate([one, ca, ca, pad1], axis=-1)
    sin_a = jnp.concatenate([0 * one, -sa, sa, 0 * pad1], axis=-1)
    cd, sd = cos_sin(DIFF_HD)
    cos_d = jnp.concatenate([cd, cd, cd, cd], axis=-1)
    sin_d = jnp.concatenate([-sd, sd, -sd, sd], axis=-1)
    return cos_a, sin_a, cos_d, sin_d


def _layout_w_in(w_in):
    d = w_in.shape[0]
    o = 0
    parts = {}
    for name, width in (("cq", Q_LORA), ("ckv", KV_LORA), ("kr", MLA_ROPE), ("glu", 2 * CONV_CH),
                        ("dq", 2 * DIFF_HEADS * DIFF_HD), ("dk", 2 * DIFF_HEADS * DIFF_HD),
                        ("dv", 2 * DIFF_HEADS * DIFF_HD), ("gates", 3 * d)):
        parts[name] = w_in[:, o:o + width]
        o += width
    kr_blk = jnp.concatenate([jnp.zeros((d, MLA_NOPE), w_in.dtype), parts["kr"],
                              jnp.zeros((d, LANES - MLA_NOPE - MLA_ROPE), w_in.dtype)], axis=1)
    win = jnp.concatenate([parts["cq"], parts["ckv"], kr_blk, parts["glu"], parts["dq"], parts["dk"],
                           parts["gates"]], axis=1)
    return win.astype(BF16), parts["dv"].T.astype(BF16)


def _layout_w_uq(w_uq):
    r = w_uq.shape[0]
    w = w_uq.reshape(r, MLA_HEADS, MLA_NOPE + MLA_ROPE)
    w = jnp.pad(w, ((0, 0), (0, 0), (0, LANES - MLA_NOPE - MLA_ROPE)))
    return w.reshape(r, MLA_HEADS * LANES).astype(BF16)


def _layout_w_ukv(w_ukv):
    r = w_ukv.shape[0]
    w = w_ukv.reshape(r, MLA_HEADS, MLA_NOPE + MLA_V)
    wk = jnp.pad(w[:, :, :MLA_NOPE], ((0, 0), (0, 0), (0, LANES - MLA_NOPE))).reshape(r, MLA_HEADS * LANES)
    wvt = w[:, :, MLA_NOPE:].reshape(r, MLA_HEADS * MLA_V).T
    return wk.astype(BF16), wvt.astype(BF16)


def _layout_router(w_rg, b_rg, w_re, b_re):
    d = w_rg.shape[0]
    w = jnp.concatenate([w_re, w_rg, jnp.zeros((d, LANES - N_EXPERTS - N_GROUPS), F32)], axis=1)
    hi = w.astype(BF16)
    lo = (w - hi.astype(F32)).astype(BF16)
    b = jnp.concatenate([b_re, b_rg, jnp.zeros((LANES - N_EXPERTS - N_GROUPS,), F32)])[None, :]
    return jnp.concatenate([hi, lo], axis=1), b


def kernel(x, c, ctx, c_ctx, w_ada, b_ada, norm1_g, w_in, q_norm_g, w_uq, kv_norm_g, w_ukv, conv_w, conv_b,
           conv_ln_g, conv_ln_b, lam_q1, lam_k1, lam_q2, lam_k2, diff_subln_g, w_o_mla, w_o_conv, w_o_diff,
           w_out, norm2_g, w_rg, b_rg, w_re, b_re, w_gate, w_up, w_down, final_g):
    b, s, d = x.shape
    n_ctx = ctx.shape[1]
    depth = w_ada.shape[0]
    tm = 512
    tq = 2 * Q_SUB
    tq_c = min(tq, n_ctx)
    tk = min(256, s)
    tr = 256
    tmoe_l = min(1024, s)
    tmoe_c = min(1024, b * n_ctx)
    ctx_row = b

    rows = -(-(b + 1) // 8) * 8
    cc = jnp.zeros((rows, d), F32).at[:b].set(c).at[b].set(c_ctx)
    mod_all = _ada(cc, w_ada, b_ada).reshape(depth, rows, 6, d)
    tables = _rope_tables(s)
    row2 = lambda a: a[None, :]

    xl = x.reshape(b * s, d)
    xc = ctx.reshape(b * n_ctx, d)
    for l in range(depth):
        update_ctx = l < depth - 1
        final = l == depth - 1
        lam_init = 0.8 - 0.6 * math.exp(-0.3 * l)
        mod = mod_all[l]
        win, wdvt = _layout_w_in(w_in[l])
        wuq = _layout_w_uq(w_uq[l])
        wuk, wuvt = _layout_w_ukv(w_ukv[l])
        n1g, qng, kvg = row2(norm1_g[l]), row2(q_norm_g[l]), row2(kv_norm_g[l])
        lam_p = jnp.stack([lam_q1[l], lam_k1[l], lam_q2[l], lam_k2[l]])
        subln = diff_subln_g[l][:, None]

        pc = _proj(xc, mod, ctx_row, n1g, win, qng, wuq, kvg, wuk, wuvt, wdvt, None, n_ctx, tm)
        pl_ = _proj(xl, mod, None, n1g, win, qng, wuq, kvg, wuk, wuvt, wdvt, tables, s, tm)
        per_batch = lambda arrs, n: [a if i in (2, 6) else a.reshape(b, n, -1) for i, a in enumerate(arrs)]
        qc, kc, vc, uc, dqc, dkc, dvc, gc = per_batch(pc, n_ctx)
        ql, kl, vl, ul, dql, dkl, dvl, gl = per_batch(pl_, s)

        wm, wc, wd = w_o_mla[l].astype(BF16), w_o_conv[l].astype(BF16), w_o_diff[l].astype(BF16)
        wout = w_out[l].astype(BF16)
        wr, br = _layout_router(w_rg[l], b_rg[l], w_re[l], b_re[l])
        n2g = row2(norm2_g[l])
        cw, cb = conv_w[l], row2(conv_b[l])
        cg, cbeta = row2(conv_ln_g[l]), row2(conv_ln_b[l])
        wg, wu, wdn = w_gate[l].astype(BF16), w_up[l].astype(BF16), w_down[l].astype(BF16)
        fg = row2(final_g)

        def tail(xs, om, oc, od, gates, mod_row, seq, tmoe, is_final):
            flat = lambda a: a.reshape(-1, a.shape[-1])
            xn, h2, comb = _merge(xs, flat(om), flat(oc), flat(od), flat(gates), mod, mod_row,
                                  wm, wc, wd, wout, n2g, wr, br, seq, tm)
            return _moe(h2, comb, xn, mod, mod_row, wg, wu, wdn, fg, seq, tmoe, is_final)

        om_l = _mla_attn(ql, kc, vc, kl, vl, tq, tk)
        od_l = _diff_attn(lam_p, subln, dql, dkc, dvc, dkl, dvl, tq, tk, lam_init)
        oc_l = _conv(ul, cw, cb, cg, cbeta, tr)
        xl_new = tail(xl, om_l, oc_l, od_l, gl, None, s, tmoe_l, final)
        if update_ctx:
            om_c = _mla_attn(qc, kc, vc, None, None, tq_c, tk)
            od_c = _diff_attn(lam_p, subln, dqc, dkc, dvc, None, None, tq_c, tk, lam_init)
            oc_c = _conv(uc, cw, cb, cg, cbeta, tr)
            xc = tail(xc, om_c, oc_c, od_c, gc, ctx_row, b * n_ctx, tmoe_c, False)
        xl = xl_new
    return xl.reshape(b, s, d)
```

```python
import functools
import math

import jax
import jax.numpy as jnp
from jax import lax
from jax.experimental import pallas as pl
from jax.experimental.pallas import tpu as pltpu

F32 = jnp.float32
BF16 = jnp.bfloat16

GRID_W = 64
ROPE_BASE = 10000.0
NORM_EPS = 1e-6

MLA_HEADS = 8
MLA_NOPE = 64
MLA_ROPE = 32
MLA_V = 64
Q_LORA = 256
KV_LORA = 128
CONV_CH = 512
CONV_WIDTH = 31
DIFF_HEADS = 4
DIFF_HD = 64
N_GROUPS = 4
EXPERTS_PER_GROUP = 8
N_EXPERTS = N_GROUPS * EXPERTS_PER_GROUP
EXPERT_FF = 256

LANES = 128
SUBLANES = 8
HALO = 16
VMEM_LIMIT = 48 * 1024 * 1024

C_CQ = 0
C_CKV = C_CQ + Q_LORA
C_KR = C_CKV + KV_LORA
C_GA = C_KR + LANES
C_GG = C_GA + CONV_CH
C_DQ = C_GG + CONV_CH
C_DK = C_DQ + 2 * DIFF_HEADS * DIFF_HD
C_GATES = C_DK + 2 * DIFF_HEADS * DIFF_HD

ONES_ROWS = 16
VT_A = MLA_V + ONES_ROWS
VT_D = 2 * DIFF_HD + ONES_ROWS
LOG2E = 1.4426950408889634
GROUP_LANE = N_EXPERTS
MOE_CHUNK = 320
MOE_VMEM_LIMIT = 56 * 1024 * 1024
Q_SUB = 256
MLA_LOOKAHEAD = 5
DIFF_LOOKAHEAD = 10


def _params(*sem):
    return pltpu.CompilerParams(dimension_semantics=sem, vmem_limit_bytes=VMEM_LIMIT)


def _sigmoid(x):
    return 0.5 * jnp.tanh(0.5 * x) + 0.5


def _rms(x, g):
    return x * lax.rsqrt(jnp.mean(x * x, axis=-1, keepdims=True) + NORM_EPS) * g


def _dot_nt(a, b):
    return lax.dot_general(a, b, (((1,), (1,)), ((), ())), preferred_element_type=F32)


def _const_spec(shape):
    return pl.BlockSpec(shape, lambda *_: (0,) * len(shape))


def _ada_kernel(c_ref, w_ref, b_ref, o_ref):
    c = c_ref[...]
    h = c * _sigmoid(c)
    o_ref[...] = jnp.dot(h, w_ref[...], preferred_element_type=F32,
                         precision=lax.Precision.HIGHEST) + b_ref[...]


def _ada(cc, w_ada, b_ada):
    depth, d, n = w_ada.shape
    bn = 1536
    rows = cc.shape[0]
    return pl.pallas_call(
        _ada_kernel,
        grid=(depth, n // bn),
        in_specs=[pl.BlockSpec((rows, d), lambda l, j: (0, 0)),
                  pl.BlockSpec((None, d, bn), lambda l, j: (l, 0, j)),
                  pl.BlockSpec((None, 1, bn), lambda l, j: (l, 0, j))],
        out_specs=pl.BlockSpec((None, rows, bn), lambda l, j: (l, 0, j)),
        out_shape=jax.ShapeDtypeStruct((depth, rows, n), F32),
        compiler_params=_params("parallel", "parallel"),
        name="adaln",
    )(cc, w_ada, b_ada.reshape(depth, 1, n))


def _swap_halves(t, half):
    n = t.shape[1]
    lane = lax.broadcasted_iota(jnp.int32, t.shape, 1) % (2 * half)
    return jnp.where(lane < half, pltpu.roll(t, n - half, 1), pltpu.roll(t, half, 1))


def _proj_kernel(*refs, rope, scale_a, scale_d):
    if rope:
        (x_ref, mod_ref, n1g_ref, win_ref, qng_ref, wuq_ref, kvg_ref, wuk_ref, wuvt_ref, wdvt_ref,
         ca_ref, sa_ref, cd_ref, sd_ref,
         q_ref, k_ref, vt_ref, u_ref, dq_ref, dk_ref, dvt_ref, g_ref) = refs
    else:
        (x_ref, mod_ref, n1g_ref, win_ref, qng_ref, wuq_ref, kvg_ref, wuk_ref, wuvt_ref, wdvt_ref,
         q_ref, k_ref, vt_ref, u_ref, dq_ref, dk_ref, dvt_ref, g_ref) = refs

    h = _rms(x_ref[...], n1g_ref[...]) * (1.0 + mod_ref[1:2, :]) + mod_ref[0:1, :]
    hb = h.astype(BF16)

    def proj(a, b):
        return jnp.dot(hb, win_ref[:, a:b], preferred_element_type=F32)

    def rope_a(t):
        return t * ca_ref[...] + _swap_halves(t, MLA_ROPE // 2) * sa_ref[...] if rope else t

    def rope_d(t):
        return t * cd_ref[...] + _swap_halves(t, DIFF_HD // 2) * sd_ref[...] if rope else t

    cq = _rms(proj(C_CQ, C_CKV), qng_ref[...]).astype(BF16)
    ckv = _rms(proj(C_CKV, C_KR), kvg_ref[...]).astype(BF16)
    kr = rope_a(proj(C_KR, C_GA))
    q = jnp.dot(cq, wuq_ref[...], preferred_element_type=F32)
    kn = jnp.dot(ckv, wuk_ref[...], preferred_element_type=F32)
    vt = _dot_nt(wuvt_ref[...], ckv)
    ones = jnp.ones((ONES_ROWS, vt.shape[1]), BF16)
    for hd in range(MLA_HEADS):
        blk = slice(hd * LANES, (hd + 1) * LANES)
        q_ref[:, blk] = (rope_a(q[:, blk]) * scale_a).astype(BF16)
        k_ref[:, blk] = (kn[:, blk] + kr).astype(BF16)
        vt_ref[hd * VT_A:hd * VT_A + MLA_V, :] = vt[hd * MLA_V:(hd + 1) * MLA_V, :].astype(BF16)
        vt_ref[hd * VT_A + MLA_V:(hd + 1) * VT_A, :] = ones

    u_ref[...] = proj(C_GA, C_GG) * _sigmoid(proj(C_GG, C_DQ))

    dq = proj(C_DQ, C_DK)
    dk = proj(C_DK, C_GATES)
    dvt = _dot_nt(wdvt_ref[...], hb)
    for hd in range(DIFF_HEADS):
        blk = slice(hd * LANES, (hd + 1) * LANES)
        dq_ref[:, blk] = (rope_d(dq[:, blk]) * scale_d).astype(BF16)
        dk_ref[:, blk] = rope_d(dk[:, blk]).astype(BF16)
        dvt_ref[hd * VT_D:hd * VT_D + LANES, :] = dvt[blk, :].astype(BF16)
        dvt_ref[hd * VT_D + LANES:(hd + 1) * VT_D, :] = ones

    d = x_ref.shape[1]
    for j in range(3):
        g_ref[:, j * d:(j + 1) * d] = _sigmoid(proj(C_GATES + j * d, C_GATES + (j + 1) * d)).astype(BF16)


def _proj(x, mod, mod_row, n1g, win, qng, wuq, kvg, wuk, wuvt, wdvt, tables, seq, tm):
    t, d = x.shape
    rope = tables is not None
    tiles_per_seq = seq // tm
    row = lambda i: (i, 0)
    col = lambda i: (0, i)
    if mod_row is None:
        mod_map = lambda i: (i // tiles_per_seq, 0, 0)
    else:
        mod_map = lambda i: (mod_row, 0, 0)
    consts = (n1g, win, qng, wuq, kvg, wuk, wuvt, wdvt)
    in_specs = [pl.BlockSpec((tm, d), row), pl.BlockSpec((None, 6, d), mod_map)]
    in_specs += [_const_spec(a.shape) for a in consts]
    args = [x, mod, *consts]
    if rope:
        in_specs += [pl.BlockSpec((tm, LANES), lambda i: (i % tiles_per_seq, 0))] * 4
        args += list(tables)
    outs = [((t, MLA_HEADS * LANES), (tm, MLA_HEADS * LANES), row, BF16),
            ((t, MLA_HEADS * LANES), (tm, MLA_HEADS * LANES), row, BF16),
            ((MLA_HEADS * VT_A, t), (MLA_HEADS * VT_A, tm), col, BF16),
            ((t, CONV_CH), (tm, CONV_CH), row, F32),
            ((t, DIFF_HEADS * LANES), (tm, DIFF_HEADS * LANES), row, BF16),
            ((t, DIFF_HEADS * LANES), (tm, DIFF_HEADS * LANES), row, BF16),
            ((DIFF_HEADS * VT_D, t), (DIFF_HEADS * VT_D, tm), col, BF16),
            ((t, 3 * d), (tm, 3 * d), row, BF16)]
    kern = functools.partial(_proj_kernel, rope=rope,
                             scale_a=LOG2E / math.sqrt(MLA_NOPE + MLA_ROPE), scale_d=LOG2E / math.sqrt(DIFF_HD))
    return pl.pallas_call(
        kern,
        grid=(t // tm,),
        in_specs=in_specs,
        out_specs=[pl.BlockSpec(blk, imap) for _, blk, imap, _ in outs],
        out_shape=[jax.ShapeDtypeStruct(shape, dt) for shape, _, _, dt in outs],
        compiler_params=_params("parallel"),
        name="proj_rope" if rope else "proj_ctx",
    )(*args)


def _flash_streams(streams, lookahead):
    n_chunks = len(streams[0][1])
    items = [(sid, c) for c in range(n_chunks) for sid in range(len(streams))]
    m = [None] * len(streams)
    acc = [None] * len(streams)

    def finish(sid, st, vt):
        cmax = jnp.max(st, axis=0, keepdims=True)
        m_new = cmax if m[sid] is None else jnp.maximum(m[sid], cmax)
        pt = jnp.exp2((st - m_new).astype(BF16))
        pv = jnp.dot(vt, pt, preferred_element_type=F32)
        acc[sid] = pv if m[sid] is None else acc[sid] * jnp.exp2(m[sid] - m_new) + pv
        m[sid] = m_new

    pending = []
    for sid, c in items:
        q, chunks = streams[sid]
        k, vt = chunks[c]
        st = _dot_nt(k, q)
        if len(pending) == lookahead:
            finish(*pending.pop(0))
        pending.append((sid, st, vt))
    for item in pending:
        finish(*item)
    return acc


def _key_chunks(kc_ref, vtc_ref, kl_ref, vtl_ref, tk, kcols, vrows):
    chunks = [(kc_ref[:, kcols], vtc_ref[vrows, :])]
    if kl_ref is not None:
        for j in range(kl_ref.shape[0] // tk):
            keys = slice(j * tk, (j + 1) * tk)
            chunks.append((kl_ref[keys, kcols], vtl_ref[vrows, keys]))
    return chunks


def _mla_attn_kernel(*refs, latent, tk):
    if latent:
        q_ref, kc_ref, vtc_ref, kl_ref, vtl_ref, o_ref = refs
    else:
        q_ref, kc_ref, vtc_ref, o_ref = refs
        kl_ref = vtl_ref = None
    subs = [slice(r, r + Q_SUB) for r in range(0, q_ref.shape[0], Q_SUB)]
    streams = []
    for qs in subs:
        for hh in range(2):
            cols = slice(hh * LANES, (hh + 1) * LANES)
            rows = slice(hh * VT_A, (hh + 1) * VT_A)
            streams.append((q_ref[qs, cols], _key_chunks(kc_ref, vtc_ref, kl_ref, vtl_ref, tk, cols, rows)))
    outs = [acc[0:MLA_V, :] * (1.0 / acc[MLA_V:MLA_V + 1, :]) for acc in _flash_streams(streams, MLA_LOOKAHEAD)]
    for i, qs in enumerate(subs):
        o_ref[qs, :] = jnp.concatenate(outs[2 * i:2 * i + 2], axis=0).T.astype(BF16)


def _mla_attn(q, kc, vtc, kl, vtl, tq, tk):
    b, lq, _ = q.shape
    c = kc.shape[1]
    latent = kl is not None
    in_specs = [pl.BlockSpec((None, tq, 2 * LANES), lambda i, p, j: (i, j, p)),
                pl.BlockSpec((None, c, 2 * LANES), lambda i, p, j: (i, 0, p)),
                pl.BlockSpec((2 * VT_A, c), lambda i, p, j: (p, i))]
    args = [q, kc, vtc]
    if latent:
        s = kl.shape[1]
        in_specs += [pl.BlockSpec((None, s, 2 * LANES), lambda i, p, j: (i, 0, p)),
                     pl.BlockSpec((2 * VT_A, s), lambda i, p, j: (p, i))]
        args += [kl, vtl]
    return pl.pallas_call(
        functools.partial(_mla_attn_kernel, latent=latent, tk=tk),
        grid=(b, MLA_HEADS // 2, lq // tq),
        in_specs=in_specs,
        out_specs=pl.BlockSpec((None, tq, LANES), lambda i, p, j: (i, j, p)),
        out_shape=jax.ShapeDtypeStruct((b, lq, MLA_HEADS * MLA_V), BF16),
        compiler_params=_params("parallel", "parallel", "parallel"),
        name="mla_attn_lat" if latent else "mla_attn_ctx",
    )(*args)


def _diff_attn_kernel(*refs, latent, tk, lam_init):
    if latent:
        lam_ref, g_ref, q_ref, kc_ref, vtc_ref, kl_ref, vtl_ref, o_ref = refs
    else:
        lam_ref, g_ref, q_ref, kc_ref, vtc_ref, o_ref = refs
        kl_ref = vtl_ref = None
    subs = [slice(r, r + Q_SUB) for r in range(0, q_ref.shape[0], Q_SUB)]
    chunks = _key_chunks(kc_ref, vtc_ref, kl_ref, vtl_ref, tk, slice(None), slice(None))
    streams = []
    for qs in subs:
        q = q_ref[qs, :]
        lane = lax.broadcasted_iota(jnp.int32, q.shape, 1)
        for first in (True, False):
            streams.append((jnp.where((lane < DIFF_HD) == first, q, jnp.zeros_like(q)), chunks))
    outs = [acc[0:LANES, :] * (1.0 / acc[LANES:LANES + 1, :]) for acc in _flash_streams(streams, DIFF_LOOKAHEAD)]
    lam = (jnp.exp(jnp.sum(lam_ref[0:1, :] * lam_ref[1:2, :], axis=-1, keepdims=True))
           - jnp.exp(jnp.sum(lam_ref[2:3, :] * lam_ref[3:4, :], axis=-1, keepdims=True)) + lam_init)
    for i, qs in enumerate(subs):
        o = outs[2 * i] - lam * outs[2 * i + 1]
        o = o * lax.rsqrt(jnp.mean(o * o, axis=0, keepdims=True) + NORM_EPS) * (g_ref[...] * (1.0 - lam_init))
        o_ref[qs, :] = o.T.astype(BF16)


def _diff_attn(lam_p, subln_g, q, kc, vtc, kl, vtl, tq, tk, lam_init):
    b, lq, _ = q.shape
    c = kc.shape[1]
    latent = kl is not None
    kblk = lambda rows: pl.BlockSpec((None, rows, LANES), lambda i, h, j: (i, 0, h))
    vblk = lambda keys: pl.BlockSpec((VT_D, keys), lambda i, h, j: (h, i))
    in_specs = [_const_spec(lam_p.shape), _const_spec(subln_g.shape),
                pl.BlockSpec((None, tq, LANES), lambda i, h, j: (i, j, h)), kblk(c), vblk(c)]
    args = [lam_p, subln_g, q, kc, vtc]
    if latent:
        in_specs += [kblk(kl.shape[1]), vblk(kl.shape[1])]
        args += [kl, vtl]
    return pl.pallas_call(
        functools.partial(_diff_attn_kernel, latent=latent, tk=tk, lam_init=lam_init),
        grid=(b, DIFF_HEADS, lq // tq),
        in_specs=in_specs,
        out_specs=pl.BlockSpec((None, tq, LANES), lambda i, h, j: (i, j, h)),
        out_shape=jax.ShapeDtypeStruct((b, lq, DIFF_HEADS * LANES), BF16),
        compiler_params=_params("parallel", "parallel", "parallel"),
        name="diff_attn_lat" if latent else "diff_attn_ctx",
    )(*args)


def _conv_kernel(prev_ref, cur_ref, next_ref, w_ref, b_ref, g_ref, beta_ref, o_ref, pad_ref, shift_ref):
    j = pl.program_id(1)
    tr = cur_ref.shape[0]
    zero = jnp.zeros((HALO, cur_ref.shape[1]), F32)
    pad_ref[0:HALO, :] = jnp.where(j == 0, zero, prev_ref[...])
    pad_ref[HALO:HALO + tr, :] = cur_ref[...]
    pad_ref[HALO + tr:, :] = jnp.where(j == pl.num_programs(1) - 1, zero, next_ref[...])
    span = shift_ref.shape[1]
    for phase in range(1, SUBLANES):
        shift_ref[phase] = pad_ref[phase:phase + span, :]
    off = HALO - CONV_WIDTH // 2
    acc = jnp.zeros(cur_ref.shape, F32) + b_ref[...]
    for tap in range(CONV_WIDTH):
        phase = (off + tap) % SUBLANES
        base = off + tap - phase
        win = pad_ref[base:base + tr, :] if phase == 0 else shift_ref[phase, base:base + tr, :]
        acc = acc + win * w_ref[tap:tap + 1, :]
    mu = jnp.mean(acc, axis=-1, keepdims=True)
    cen = acc - mu
    y = cen * lax.rsqrt(jnp.mean(cen * cen, axis=-1, keepdims=True) + NORM_EPS) * g_ref[...] + beta_ref[...]
    o_ref[...] = (y * _sigmoid(y)).astype(BF16)


def _conv(u, w, bias, g, beta, tr):
    b, l, ch = u.shape
    per = tr // HALO
    last = l // HALO - 1
    return pl.pallas_call(
        _conv_kernel,
        grid=(b, l // tr),
        in_specs=[pl.BlockSpec((None, HALO, ch), lambda i, j: (i, jnp.maximum(j * per - 1, 0), 0)),
                  pl.BlockSpec((None, tr, ch), lambda i, j: (i, j, 0)),
                  pl.BlockSpec((None, HALO, ch), lambda i, j: (i, jnp.minimum((j + 1) * per, last), 0)),
                  _const_spec(w.shape), _const_spec(bias.shape), _const_spec(g.shape), _const_spec(beta.shape)],
        out_specs=pl.BlockSpec((None, tr, ch), lambda i, j: (i, j, 0)),
        out_shape=jax.ShapeDtypeStruct((b, l, ch), BF16),
        scratch_shapes=[pltpu.VMEM((tr + 2 * HALO, ch), F32),
                        pltpu.VMEM((SUBLANES, tr + 2 * HALO - SUBLANES, ch), F32)],
        compiler_params=_params("parallel", "parallel"),
        name="conv_ln_silu",
    )(u, u, u, w, bias, g, beta)


def _route(logits):
    lane = lax.broadcasted_iota(jnp.int32, logits.shape, 1)
    neg = jnp.float32(-jnp.inf)
    big = jnp.int32(LANES)
    is_grp = (lane >= N_EXPERTS) & (lane < N_EXPERTS + N_GROUPS)
    gl = jnp.where(is_grp, logits, neg)
    gmax = jnp.max(gl, axis=-1, keepdims=True)
    gsel = jnp.min(jnp.where(gl == gmax, lane, big), axis=-1, keepdims=True) - N_EXPERTS
    p_g = 1.0 / jnp.sum(jnp.exp(gl - gmax), axis=-1, keepdims=True)
    in_grp = (lane >= gsel * EXPERTS_PER_GROUP) & (lane < (gsel + 1) * EXPERTS_PER_GROUP)
    e1 = jnp.where(in_grp, logits, neg)
    v1 = jnp.max(e1, axis=-1, keepdims=True)
    i1 = jnp.min(jnp.where(e1 == v1, lane, big), axis=-1, keepdims=True)
    e2 = jnp.where(lane == i1, neg, e1)
    v2 = jnp.max(e2, axis=-1, keepdims=True)
    i2 = jnp.min(jnp.where(e2 == v2, lane, big), axis=-1, keepdims=True)
    r = jnp.exp(v2 - v1)
    w1 = p_g / (1.0 + r)
    comb = jnp.where(lane == i1, w1, jnp.where(lane == i2, w1 * r, 0.0))
    return jnp.where(lane == GROUP_LANE, gsel.astype(F32), comb)


def _merge_kernel(x_ref, om_ref, oc_ref, od_ref, gates_ref, mod_ref, wm_ref, wc_ref, wd_ref, wout_ref,
                  n2g_ref, wr_ref, br_ref, xn_ref, h2_ref, comb_ref):
    d = x_ref.shape[1]
    gate = lambda j: gates_ref[:, j * d:(j + 1) * d].astype(F32)
    y = (gate(0) * jnp.dot(om_ref[...], wm_ref[...], preferred_element_type=F32)
         + gate(1) * jnp.dot(oc_ref[...], wc_ref[...], preferred_element_type=F32)
         + gate(2) * jnp.dot(od_ref[...], wd_ref[...], preferred_element_type=F32))
    z = jnp.dot(y.astype(BF16), wout_ref[...], preferred_element_type=F32)
    xn = x_ref[...] + mod_ref[2:3, :] * z
    xn_ref[...] = xn
    h2 = _rms(xn, n2g_ref[...]) * (1.0 + mod_ref[4:5, :]) + mod_ref[3:4, :]
    h_hi = h2.astype(BF16)
    h2_ref[...] = h_hi
    h_lo = (h2 - h_hi.astype(F32)).astype(BF16)
    hw = jnp.dot(h_hi, wr_ref[...], preferred_element_type=F32)
    lw = jnp.dot(h_lo, wr_ref[:, 0:LANES], preferred_element_type=F32)
    comb_ref[...] = _route(hw[:, 0:LANES] + hw[:, LANES:] + lw + br_ref[...])


def _merge(x, om, oc, od, gates, mod, mod_row, wm, wc, wd, wout, n2g, wr, br, seq, tm):
    t, d = x.shape
    tiles_per_seq = seq // tm
    row = lambda i: (i, 0)
    if mod_row is None:
        mod_map = lambda i: (i // tiles_per_seq, 0, 0)
    else:
        mod_map = lambda i: (mod_row, 0, 0)
    rows = lambda a: pl.BlockSpec((tm, a.shape[1]), row)
    return pl.pallas_call(
        _merge_kernel,
        grid=(t // tm,),
        in_specs=[rows(x), rows(om), rows(oc), rows(od), rows(gates),
                  pl.BlockSpec((None, 6, d), mod_map)]
                 + [_const_spec(a.shape) for a in (wm, wc, wd, wout, n2g, wr, br)],
        out_specs=[pl.BlockSpec((tm, d), row), pl.BlockSpec((tm, d), row), pl.BlockSpec((tm, LANES), row)],
        out_shape=[jax.ShapeDtypeStruct((t, d), F32), jax.ShapeDtypeStruct((t, d), BF16),
                   jax.ShapeDtypeStruct((t, LANES), F32)],
        compiler_params=_params("parallel"),
        name="merge_route",
    )(x, om, oc, od, gates, mod, wm, wc, wd, wout, n2g, wr, br)


def _moe_kernel(h_ref, comb_ref, xn_ref, mod_ref, tri_ref, wg_ref, wu_ref, wd_ref, fg_ref, o_ref,
                pos_ref, comb3_ref, *, final, mod_rows):
    g = pl.program_id(1)
    slabs, slab_rows, d = o_ref.shape
    tm = slabs * slab_rows

    @pl.when(g == 0)
    def _():
        o_ref[...] = jnp.zeros_like(o_ref)
        comb = comb_ref[...].reshape(tm, LANES)
        hi = comb.astype(BF16)
        r1 = comb - hi.astype(F32)
        mid = r1.astype(BF16)
        lo = (r1 - mid.astype(F32)).astype(BF16)
        comb3_ref[...] = jnp.concatenate([hi, mid, lo], axis=1)
        grow = comb.T[GROUP_LANE:GROUP_LANE + 1, :]
        gid = lax.broadcasted_iota(jnp.int32, (8, tm), 0).astype(F32)
        member = grow == gid
        rank = jnp.dot(jnp.where(member, 1.0, 0.0).astype(BF16), tri_ref[...], preferred_element_type=F32)
        pos_ref[...] = jnp.where(member, rank, -1.0)

    row = lax.broadcasted_iota(jnp.int32, (8, tm), 0)
    posg = jnp.sum(jnp.where(row == g, pos_ref[...], 0.0), axis=0, keepdims=True)
    n_chunks = (jnp.max(posg).astype(jnp.int32) + MOE_CHUNK) // MOE_CHUNK

    def chunk(c, carry):
        slot = (lax.broadcasted_iota(jnp.int32, (MOE_CHUNK, tm), 0) + c * MOE_CHUNK).astype(F32)
        sel = jnp.where(posg == slot, 1.0, 0.0).astype(BF16)
        h = h_ref[...].reshape(tm, d)
        x = jnp.dot(sel, h, preferred_element_type=F32).astype(BF16)
        cw3 = jnp.dot(sel, comb3_ref[...], preferred_element_type=F32)
        cw = cw3[:, 0:LANES] + cw3[:, LANES:2 * LANES] + cw3[:, 2 * LANES:]
        lane = lax.broadcasted_iota(jnp.int32, cw.shape, 1)
        y = jnp.zeros((MOE_CHUNK, d), F32)
        for e in range(EXPERTS_PER_GROUP):
            a = jnp.dot(x, wg_ref[e], preferred_element_type=F32)
            u = jnp.dot(x, wu_ref[e], preferred_element_type=F32)
            w = jnp.sum(jnp.where(lane == g * EXPERTS_PER_GROUP + e, cw, 0.0), axis=1, keepdims=True)
            hid = a * _sigmoid(a) * u * w
            y = y + jnp.dot(hid.astype(BF16), wd_ref[e], preferred_element_type=F32)
        back = lax.dot_general(sel, y.astype(BF16), (((0,), (0,)), ((), ())), preferred_element_type=F32)
        o_ref[...] += back.reshape(slabs, slab_rows, d)
        return carry

    lax.fori_loop(0, n_chunks, chunk, 0)

    @pl.when(g == pl.num_programs(1) - 1)
    def _():
        for sl, mod_row in enumerate(mod_rows):
            out = xn_ref[sl] + mod_ref[mod_row, 5:6, :] * o_ref[sl]
            o_ref[sl] = _rms(out, fg_ref[...]) if final else out


def _moe(h2, comb, xn, mod, mod_rows, wg, wu, wd, final_g, tm, final):
    t, d = xn.shape
    slabs = len(mod_rows)
    slab_rows = tm // slabs
    n_tiles = t // tm
    view = lambda a: a.reshape(slabs, n_tiles, slab_rows, a.shape[-1])
    tile = lambda a: pl.BlockSpec((slabs, None, slab_rows, a.shape[-1]), lambda i, g: (0, i, 0, 0))
    ff = wg.shape[2]
    idx = jnp.arange(tm)
    tri = (idx[:, None] < idx[None, :]).astype(BF16)
    grp = lambda i, g: (g, 0, 0)
    once = pl.Buffered(1)
    xn_spec = pl.BlockSpec((slabs, None, slab_rows, d), lambda i, g: (0, i, 0, 0))
    out = pl.pallas_call(
        functools.partial(_moe_kernel, final=final, mod_rows=tuple(mod_rows)),
        grid=(n_tiles, N_GROUPS),
        in_specs=[tile(h2), tile(comb), xn_spec,
                  _const_spec(mod.shape),
                  pl.BlockSpec((tm, tm), lambda i, g: (0, 0), pipeline_mode=once),
                  pl.BlockSpec((EXPERTS_PER_GROUP, d, ff), grp),
                  pl.BlockSpec((EXPERTS_PER_GROUP, d, ff), grp),
                  pl.BlockSpec((EXPERTS_PER_GROUP, ff, d), grp),
                  pl.BlockSpec(final_g.shape, lambda i, g: (0, 0))],
        out_specs=tile(xn),
        out_shape=jax.ShapeDtypeStruct((slabs, n_tiles, slab_rows, d), F32),
        scratch_shapes=[pltpu.VMEM((8, tm), F32), pltpu.VMEM((tm, 3 * LANES), BF16)],
        compiler_params=pltpu.CompilerParams(dimension_semantics=("parallel", "arbitrary"),
                                             vmem_limit_bytes=MOE_VMEM_LIMIT),
        name="moe",
    )(view(h2), view(comb), view(xn), mod, tri, wg, wu, wd, final_g)
    return out.reshape(t, d)


def _rope_tables(s):
    rows = s // GRID_W
    row = jnp.repeat(jnp.arange(rows, dtype=F32), GRID_W)
    col = jnp.tile(jnp.arange(GRID_W, dtype=F32), rows)

    def cos_sin(rot_dim):
        n = rot_dim // 4
        inv = ROPE_BASE ** (-jnp.arange(n, dtype=F32) / n)
        ang = jnp.concatenate([row[:, None] * inv, col[:, None] * inv], axis=-1)
        return jnp.cos(ang), jnp.sin(ang)

    ca, sa = cos_sin(MLA_ROPE)
    one = jnp.ones((s, MLA_NOPE), F32)
    pad1 = jnp.ones((s, LANES - MLA_NOPE - MLA_ROPE), F32)
    cos_a = jnp.concatenate([one, ca, ca, pad1], axis=-1)
    sin_a = jnp.concatenate([0 * one, -sa, sa, 0 * pad1], axis=-1)
    cd, sd = cos_sin(DIFF_HD)
    cos_d = jnp.concatenate([cd, cd, cd, cd], axis=-1)
    sin_d = jnp.concatenate([-sd, sd, -sd, sd], axis=-1)
    return cos_a, sin_a, cos_d, sin_d


def _layout_w_in(w_in):
    d = w_in.shape[0]
    o = 0
    parts = {}
    for name, width in (("cq", Q_LORA), ("ckv", KV_LORA), ("kr", MLA_ROPE), ("glu", 2 * CONV_CH),
                        ("dq", 2 * DIFF_HEADS * DIFF_HD), ("dk", 2 * DIFF_HEADS * DIFF_HD),
                        ("dv", 2 * DIFF_HEADS * DIFF_HD), ("gates", 3 * d)):
        parts[name] = w_in[:, o:o + width]
        o += width
    kr_blk = jnp.concatenate([jnp.zeros((d, MLA_NOPE), w_in.dtype), parts["kr"],
                              jnp.zeros((d, LANES - MLA_NOPE - MLA_ROPE), w_in.dtype)], axis=1)
    win = jnp.concatenate([parts["cq"], parts["ckv"], kr_blk, parts["glu"], parts["dq"], parts["dk"],
                           parts["gates"]], axis=1)
    return win.astype(BF16), parts["dv"].T.astype(BF16)


def _layout_w_uq(w_uq):
    r = w_uq.shape[0]
    w = w_uq.reshape(r, MLA_HEADS, MLA_NOPE + MLA_ROPE)
    w = jnp.pad(w, ((0, 0), (0, 0), (0, LANES - MLA_NOPE - MLA_ROPE)))
    return w.reshape(r, MLA_HEADS * LANES).astype(BF16)


def _layout_w_ukv(w_ukv):
    r = w_ukv.shape[0]
    w = w_ukv.reshape(r, MLA_HEADS, MLA_NOPE + MLA_V)
    wk = jnp.pad(w[:, :, :MLA_NOPE], ((0, 0), (0, 0), (0, LANES - MLA_NOPE))).reshape(r, MLA_HEADS * LANES)
    wvt = w[:, :, MLA_NOPE:].reshape(r, MLA_HEADS * MLA_V).T
    return wk.astype(BF16), wvt.astype(BF16)


def _layout_router(w_rg, b_rg, w_re, b_re):
    d = w_rg.shape[0]
    w = jnp.concatenate([w_re, w_rg, jnp.zeros((d, LANES - N_EXPERTS - N_GROUPS), F32)], axis=1)
    hi = w.astype(BF16)
    lo = (w - hi.astype(F32)).astype(BF16)
    b = jnp.concatenate([b_re, b_rg, jnp.zeros((LANES - N_EXPERTS - N_GROUPS,), F32)])[None, :]
    return jnp.concatenate([hi, lo], axis=1), b


def kernel(x, c, ctx, c_ctx, w_ada, b_ada, norm1_g, w_in, q_norm_g, w_uq, kv_norm_g, w_ukv, conv_w, conv_b,
           conv_ln_g, conv_ln_b, lam_q1, lam_k1, lam_q2, lam_k2, diff_subln_g, w_o_mla, w_o_conv, w_o_diff,
           w_out, norm2_g, w_rg, b_rg, w_re, b_re, w_gate, w_up, w_down, final_g):
    b, s, d = x.shape
    n_ctx = ctx.shape[1]
    depth = w_ada.shape[0]
    tm = 512
    tq = 4 * Q_SUB
    tq_c = min(tq, n_ctx)
    tk = min(256, s)
    tr = 256
    tmoe_l = min(1024, b * s)
    tmoe_c = min(1024, b * n_ctx)
    ctx_row = b

    rows = -(-(b + 1) // 8) * 8
    cc = jnp.zeros((rows, d), F32).at[:b].set(c).at[b].set(c_ctx)
    mod_all = _ada(cc, w_ada, b_ada).reshape(depth, rows, 6, d)
    tables = _rope_tables(s)
    row2 = lambda a: a[None, :]

    xl = x.reshape(b * s, d)
    xc = ctx.reshape(b * n_ctx, d)
    for l in range(depth):
        update_ctx = l < depth - 1
        final = l == depth - 1
        lam_init = 0.8 - 0.6 * math.exp(-0.3 * l)
        mod = mod_all[l]
        win, wdvt = _layout_w_in(w_in[l])
        wuq = _layout_w_uq(w_uq[l])
        wuk, wuvt = _layout_w_ukv(w_ukv[l])
        n1g, qng, kvg = row2(norm1_g[l]), row2(q_norm_g[l]), row2(kv_norm_g[l])
        lam_p = jnp.stack([lam_q1[l], lam_k1[l], lam_q2[l], lam_k2[l]])
        subln = diff_subln_g[l][:, None]

        pc = _proj(xc, mod, ctx_row, n1g, win, qng, wuq, kvg, wuk, wuvt, wdvt, None, n_ctx, tm)
        pl_ = _proj(xl, mod, None, n1g, win, qng, wuq, kvg, wuk, wuvt, wdvt, tables, s, tm)
        per_batch = lambda arrs, n: [a if i in (2, 6) else a.reshape(b, n, -1) for i, a in enumerate(arrs)]
        qc, kc, vc, uc, dqc, dkc, dvc, gc = per_batch(pc, n_ctx)
        ql, kl, vl, ul, dql, dkl, dvl, gl = per_batch(pl_, s)

        wm, wc, wd = w_o_mla[l].astype(BF16), w_o_conv[l].astype(BF16), w_o_diff[l].astype(BF16)
        wout = w_out[l].astype(BF16)
        wr, br = _layout_router(w_rg[l], b_rg[l], w_re[l], b_re[l])
        n2g = row2(norm2_g[l])
        cw, cb = conv_w[l], row2(conv_b[l])
        cg, cbeta = row2(conv_ln_g[l]), row2(conv_ln_b[l])
        wg, wu, wdn = w_gate[l].astype(BF16), w_up[l].astype(BF16), w_down[l].astype(BF16)
        fg = row2(final_g)

        def tail(xs, om, oc, od, gates, mod_row, seq, tmoe, is_final):
            flat = lambda a: a.reshape(-1, a.shape[-1])
            xn, h2, comb = _merge(xs, flat(om), flat(oc), flat(od), flat(gates), mod, mod_row,
                                  wm, wc, wd, wout, n2g, wr, br, seq, tm)
            moe_rows = tuple(range(b)) if mod_row is None else (mod_row,)
            return _moe(h2, comb, xn, mod, moe_rows, wg, wu, wdn, fg, tmoe, is_final)

        om_l = _mla_attn(ql, kc, vc, kl, vl, tq, tk)
        od_l = _diff_attn(lam_p, subln, dql, dkc, dvc, dkl, dvl, tq, tk, lam_init)
        oc_l = _conv(ul, cw, cb, cg, cbeta, tr)
        xl_new = tail(xl, om_l, oc_l, od_l, gl, None, s, tmoe_l, final)
        if update_ctx:
            om_c = _mla_attn(qc, kc, vc, None, None, tq_c, tk)
            od_c = _diff_attn(lam_p, subln, dqc, dkc, dvc, None, None, tq_c, tk, lam_init)
            oc_c = _conv(uc, cw, cb, cg, cbeta, tr)
            xc = tail(xc, om_c, oc_c, od_c, gc, ctx_row, b * n_ctx, tmoe_c, False)
        xl = xl_new
    return xl.reshape(b, s, d)
```

```python
import functools
import math

import jax
import jax.numpy as jnp
from jax import lax
from jax.experimental import pallas as pl
from jax.experimental.pallas import tpu as pltpu

F32 = jnp.float32
BF16 = jnp.bfloat16

GRID_W = 64
ROPE_BASE = 10000.0
NORM_EPS = 1e-6

MLA_HEADS = 8
MLA_NOPE = 64
MLA_ROPE = 32
MLA_V = 64
Q_LORA = 256
KV_LORA = 128
CONV_CH = 512
CONV_WIDTH = 31
DIFF_HEADS = 4
DIFF_HD = 64
N_GROUPS = 4
EXPERTS_PER_GROUP = 8
N_EXPERTS = N_GROUPS * EXPERTS_PER_GROUP
EXPERT_FF = 256

LANES = 128
SUBLANES = 8
HALO = 16
VMEM_LIMIT = 48 * 1024 * 1024

C_CQ = 0
C_CKV = C_CQ + Q_LORA
C_KR = C_CKV + KV_LORA
C_GA = C_KR + LANES
C_GG = C_GA + CONV_CH
C_DQ = C_GG + CONV_CH
C_DK = C_DQ + 2 * DIFF_HEADS * DIFF_HD
C_GATES = C_DK + 2 * DIFF_HEADS * DIFF_HD

ONES_ROWS = 16
VT_A = MLA_V + ONES_ROWS
VT_D = 2 * DIFF_HD + ONES_ROWS
LOG2E = 1.4426950408889634
GROUP_LANE = N_EXPERTS
MOE_CHUNK = 320
MERGE_PART = 256
MOE_VMEM_LIMIT = 56 * 1024 * 1024
Q_SUB = 256
MLA_LOOKAHEAD = 5
DIFF_LOOKAHEAD = 10


def _params(*sem):
    return pltpu.CompilerParams(dimension_semantics=sem, vmem_limit_bytes=VMEM_LIMIT)


def _sigmoid(x):
    return 0.5 * jnp.tanh(0.5 * x) + 0.5


def _rms(x, g):
    return x * lax.rsqrt(jnp.mean(x * x, axis=-1, keepdims=True) + NORM_EPS) * g


def _dot_nt(a, b):
    return lax.dot_general(a, b, (((1,), (1,)), ((), ())), preferred_element_type=F32)


def _const_spec(shape):
    return pl.BlockSpec(shape, lambda *_: (0,) * len(shape))


def _ada_kernel(c_ref, w_ref, b_ref, o_ref):
    c = c_ref[...]
    h = c * _sigmoid(c)
    o_ref[...] = jnp.dot(h, w_ref[...], preferred_element_type=F32,
                         precision=lax.Precision.HIGHEST) + b_ref[...]


def _ada(cc, w_ada, b_ada):
    depth, d, n = w_ada.shape
    bn = 1536
    rows = cc.shape[0]
    return pl.pallas_call(
        _ada_kernel,
        grid=(depth, n // bn),
        in_specs=[pl.BlockSpec((rows, d), lambda l, j: (0, 0)),
                  pl.BlockSpec((None, d, bn), lambda l, j: (l, 0, j)),
                  pl.BlockSpec((None, 1, bn), lambda l, j: (l, 0, j))],
        out_specs=pl.BlockSpec((None, rows, bn), lambda l, j: (l, 0, j)),
        out_shape=jax.ShapeDtypeStruct((depth, rows, n), F32),
        compiler_params=_params("parallel", "parallel"),
        name="adaln",
    )(cc, w_ada, b_ada.reshape(depth, 1, n))


def _swap_halves(t, half):
    n = t.shape[1]
    lane = lax.broadcasted_iota(jnp.int32, t.shape, 1) % (2 * half)
    return jnp.where(lane < half, pltpu.roll(t, n - half, 1), pltpu.roll(t, half, 1))


def _proj_kernel(*refs, rope, scale_a, scale_d):
    if rope:
        (x_ref, mod_ref, n1g_ref, win_ref, qng_ref, wuq_ref, kvg_ref, wuk_ref, wuvt_ref, wdvt_ref,
         ca_ref, sa_ref, cd_ref, sd_ref,
         q_ref, k_ref, vt_ref, u_ref, dq_ref, dk_ref, dvt_ref, g_ref) = refs
    else:
        (x_ref, mod_ref, n1g_ref, win_ref, qng_ref, wuq_ref, kvg_ref, wuk_ref, wuvt_ref, wdvt_ref,
         q_ref, k_ref, vt_ref, u_ref, dq_ref, dk_ref, dvt_ref, g_ref) = refs

    h = _rms(x_ref[...], n1g_ref[...]) * (1.0 + mod_ref[1:2, :]) + mod_ref[0:1, :]
    hb = h.astype(BF16)

    def proj(a, b):
        return jnp.dot(hb, win_ref[:, a:b], preferred_element_type=F32)

    def rope_a(t):
        return t * ca_ref[...] + _swap_halves(t, MLA_ROPE // 2) * sa_ref[...] if rope else t

    def rope_d(t):
        return t * cd_ref[...] + _swap_halves(t, DIFF_HD // 2) * sd_ref[...] if rope else t

    cq = _rms(proj(C_CQ, C_CKV), qng_ref[...]).astype(BF16)
    ckv = _rms(proj(C_CKV, C_KR), kvg_ref[...]).astype(BF16)
    kr = rope_a(proj(C_KR, C_GA))
    q = jnp.dot(cq, wuq_ref[...], preferred_element_type=F32)
    kn = jnp.dot(ckv, wuk_ref[...], preferred_element_type=F32)
    vt = _dot_nt(wuvt_ref[...], ckv)
    ones = jnp.ones((ONES_ROWS, vt.shape[1]), BF16)
    for hd in range(MLA_HEADS):
        blk = slice(hd * LANES, (hd + 1) * LANES)
        q_ref[:, blk] = (rope_a(q[:, blk]) * scale_a).astype(BF16)
        k_ref[:, blk] = (kn[:, blk] + kr).astype(BF16)
        vt_ref[hd * VT_A:hd * VT_A + MLA_V, :] = vt[hd * MLA_V:(hd + 1) * MLA_V, :].astype(BF16)
        vt_ref[hd * VT_A + MLA_V:(hd + 1) * VT_A, :] = ones

    u_ref[...] = proj(C_GA, C_GG) * _sigmoid(proj(C_GG, C_DQ))

    dq = proj(C_DQ, C_DK)
    dk = proj(C_DK, C_GATES)
    dvt = _dot_nt(wdvt_ref[...], hb)
    for hd in range(DIFF_HEADS):
        blk = slice(hd * LANES, (hd + 1) * LANES)
        dq_ref[:, blk] = (rope_d(dq[:, blk]) * scale_d).astype(BF16)
        dk_ref[:, blk] = rope_d(dk[:, blk]).astype(BF16)
        dvt_ref[hd * VT_D:hd * VT_D + LANES, :] = dvt[blk, :].astype(BF16)
        dvt_ref[hd * VT_D + LANES:(hd + 1) * VT_D, :] = ones

    d = x_ref.shape[1]
    for j in range(3):
        g_ref[:, j * d:(j + 1) * d] = _sigmoid(proj(C_GATES + j * d, C_GATES + (j + 1) * d)).astype(BF16)


def _proj(x, mod, mod_row, n1g, win, qng, wuq, kvg, wuk, wuvt, wdvt, tables, seq, tm):
    t, d = x.shape
    rope = tables is not None
    tiles_per_seq = seq // tm
    row = lambda i: (i, 0)
    col = lambda i: (0, i)
    if mod_row is None:
        mod_map = lambda i: (i // tiles_per_seq, 0, 0)
    else:
        mod_map = lambda i: (mod_row, 0, 0)
    consts = (n1g, win, qng, wuq, kvg, wuk, wuvt, wdvt)
    in_specs = [pl.BlockSpec((tm, d), row), pl.BlockSpec((None, 6, d), mod_map)]
    in_specs += [_const_spec(a.shape) for a in consts]
    args = [x, mod, *consts]
    if rope:
        in_specs += [pl.BlockSpec((tm, LANES), lambda i: (i % tiles_per_seq, 0))] * 4
        args += list(tables)
    outs = [((t, MLA_HEADS * LANES), (tm, MLA_HEADS * LANES), row, BF16),
            ((t, MLA_HEADS * LANES), (tm, MLA_HEADS * LANES), row, BF16),
            ((MLA_HEADS * VT_A, t), (MLA_HEADS * VT_A, tm), col, BF16),
            ((t, CONV_CH), (tm, CONV_CH), row, F32),
            ((t, DIFF_HEADS * LANES), (tm, DIFF_HEADS * LANES), row, BF16),
            ((t, DIFF_HEADS * LANES), (tm, DIFF_HEADS * LANES), row, BF16),
            ((DIFF_HEADS * VT_D, t), (DIFF_HEADS * VT_D, tm), col, BF16),
            ((t, 3 * d), (tm, 3 * d), row, BF16)]
    kern = functools.partial(_proj_kernel, rope=rope,
                             scale_a=LOG2E / math.sqrt(MLA_NOPE + MLA_ROPE), scale_d=LOG2E / math.sqrt(DIFF_HD))
    return pl.pallas_call(
        kern,
        grid=(t // tm,),
        in_specs=in_specs,
        out_specs=[pl.BlockSpec(blk, imap) for _, blk, imap, _ in outs],
        out_shape=[jax.ShapeDtypeStruct(shape, dt) for shape, _, _, dt in outs],
        compiler_params=_params("parallel"),
        name="proj_rope" if rope else "proj_ctx",
    )(*args)


def _flash_streams(streams, lookahead):
    n_chunks = len(streams[0][1])
    items = [(sid, c) for c in range(n_chunks) for sid in range(len(streams))]
    m = [None] * len(streams)
    acc = [None] * len(streams)

    def finish(sid, st, vt):
        cmax = jnp.max(st, axis=0, keepdims=True)
        m_new = cmax if m[sid] is None else jnp.maximum(m[sid], cmax)
        pt = jnp.exp2((st - m_new).astype(BF16))
        pv = jnp.dot(vt, pt, preferred_element_type=F32)
        acc[sid] = pv if m[sid] is None else acc[sid] * jnp.exp2(m[sid] - m_new) + pv
        m[sid] = m_new

    qts = [q.astype(F32).T.astype(BF16) for q, _ in streams]
    pending = []
    for sid, c in items:
        k, vt = streams[sid][1][c]
        st = jnp.dot(k, qts[sid], preferred_element_type=F32)
        if len(pending) == lookahead:
            finish(*pending.pop(0))
        pending.append((sid, st, vt))
    for item in pending:
        finish(*item)
    return acc


def _key_chunks(kc_ref, vtc_ref, kl_ref, vtl_ref, tk, kcols, vrows):
    chunks = [(kc_ref[:, kcols], vtc_ref[vrows, :])]
    if kl_ref is not None:
        for j in range(kl_ref.shape[0] // tk):
            keys = slice(j * tk, (j + 1) * tk)
            chunks.append((kl_ref[keys, kcols], vtl_ref[vrows, keys]))
    return chunks


def _mla_attn_kernel(*refs, latent, tk):
    if latent:
        q_ref, kc_ref, vtc_ref, kl_ref, vtl_ref, o_ref = refs
    else:
        q_ref, kc_ref, vtc_ref, o_ref = refs
        kl_ref = vtl_ref = None
    subs = [slice(r, r + Q_SUB) for r in range(0, q_ref.shape[0], Q_SUB)]
    streams = []
    for qs in subs:
        for hh in range(2):
            cols = slice(hh * LANES, (hh + 1) * LANES)
            rows = slice(hh * VT_A, (hh + 1) * VT_A)
            streams.append((q_ref[qs, cols], _key_chunks(kc_ref, vtc_ref, kl_ref, vtl_ref, tk, cols, rows)))
    outs = [acc[0:MLA_V, :] * (1.0 / acc[MLA_V:MLA_V + 1, :]) for acc in _flash_streams(streams, MLA_LOOKAHEAD)]
    for i, qs in enumerate(subs):
        o_ref[qs, :] = jnp.concatenate(outs[2 * i:2 * i + 2], axis=0).T.astype(BF16)


def _mla_attn(q, kc, vtc, kl, vtl, tq, tk):
    b, lq, _ = q.shape
    c = kc.shape[1]
    latent = kl is not None
    in_specs = [pl.BlockSpec((None, tq, 2 * LANES), lambda i, p, j: (i, j, p)),
                pl.BlockSpec((None, c, 2 * LANES), lambda i, p, j: (i, 0, p)),
                pl.BlockSpec((2 * VT_A, c), lambda i, p, j: (p, i))]
    args = [q, kc, vtc]
    if latent:
        s = kl.shape[1]
        in_specs += [pl.BlockSpec((None, s, 2 * LANES), lambda i, p, j: (i, 0, p)),
                     pl.BlockSpec((2 * VT_A, s), lambda i, p, j: (p, i))]
        args += [kl, vtl]
    return pl.pallas_call(
        functools.partial(_mla_attn_kernel, latent=latent, tk=tk),
        grid=(b, MLA_HEADS // 2, lq // tq),
        in_specs=in_specs,
        out_specs=pl.BlockSpec((None, tq, LANES), lambda i, p, j: (i, j, p)),
        out_shape=jax.ShapeDtypeStruct((b, lq, MLA_HEADS * MLA_V), BF16),
        compiler_params=_params("parallel", "parallel", "parallel"),
        name="mla_attn_lat" if latent else "mla_attn_ctx",
    )(*args)


def _diff_attn_kernel(*refs, latent, tk, lam_init):
    if latent:
        lam_ref, g_ref, q_ref, kc_ref, vtc_ref, kl_ref, vtl_ref, o_ref = refs
    else:
        lam_ref, g_ref, q_ref, kc_ref, vtc_ref, o_ref = refs
        kl_ref = vtl_ref = None
    subs = [slice(r, r + Q_SUB) for r in range(0, q_ref.shape[0], Q_SUB)]
    chunks = _key_chunks(kc_ref, vtc_ref, kl_ref, vtl_ref, tk, slice(None), slice(None))
    streams = []
    for qs in subs:
        q = q_ref[qs, :]
        lane = lax.broadcasted_iota(jnp.int32, q.shape, 1)
        for first in (True, False):
            streams.append((jnp.where((lane < DIFF_HD) == first, q, jnp.zeros_like(q)), chunks))
    outs = [acc[0:LANES, :] * (1.0 / acc[LANES:LANES + 1, :]) for acc in _flash_streams(streams, DIFF_LOOKAHEAD)]
    lam = (jnp.exp(jnp.sum(lam_ref[0:1, :] * lam_ref[1:2, :], axis=-1, keepdims=True))
           - jnp.exp(jnp.sum(lam_ref[2:3, :] * lam_ref[3:4, :], axis=-1, keepdims=True)) + lam_init)
    for i, qs in enumerate(subs):
        o = outs[2 * i] - lam * outs[2 * i + 1]
        o = o * lax.rsqrt(jnp.mean(o * o, axis=0, keepdims=True) + NORM_EPS) * (g_ref[...] * (1.0 - lam_init))
        o_ref[qs, :] = o.T.astype(BF16)


def _diff_attn(lam_p, subln_g, q, kc, vtc, kl, vtl, tq, tk, lam_init):
    b, lq, _ = q.shape
    c = kc.shape[1]
    latent = kl is not None
    kblk = lambda rows: pl.BlockSpec((None, rows, LANES), lambda i, h, j: (i, 0, h))
    vblk = lambda keys: pl.BlockSpec((VT_D, keys), lambda i, h, j: (h, i))
    in_specs = [_const_spec(lam_p.shape), _const_spec(subln_g.shape),
                pl.BlockSpec((None, tq, LANES), lambda i, h, j: (i, j, h)), kblk(c), vblk(c)]
    args = [lam_p, subln_g, q, kc, vtc]
    if latent:
        in_specs += [kblk(kl.shape[1]), vblk(kl.shape[1])]
        args += [kl, vtl]
    return pl.pallas_call(
        functools.partial(_diff_attn_kernel, latent=latent, tk=tk, lam_init=lam_init),
        grid=(b, DIFF_HEADS, lq // tq),
        in_specs=in_specs,
        out_specs=pl.BlockSpec((None, tq, LANES), lambda i, h, j: (i, j, h)),
        out_shape=jax.ShapeDtypeStruct((b, lq, DIFF_HEADS * LANES), BF16),
        compiler_params=_params("parallel", "parallel", "parallel"),
        name="diff_attn_lat" if latent else "diff_attn_ctx",
    )(*args)


def _conv_kernel(prev_ref, cur_ref, next_ref, w_ref, b_ref, g_ref, beta_ref, o_ref, pad_ref, shift_ref):
    j = pl.program_id(1)
    tr = cur_ref.shape[0]
    zero = jnp.zeros((HALO, cur_ref.shape[1]), F32)
    pad_ref[0:HALO, :] = jnp.where(j == 0, zero, prev_ref[...])
    pad_ref[HALO:HALO + tr, :] = cur_ref[...]
    pad_ref[HALO + tr:, :] = jnp.where(j == pl.num_programs(1) - 1, zero, next_ref[...])
    span = shift_ref.shape[1]
    for phase in range(1, SUBLANES):
        shift_ref[phase] = pad_ref[phase:phase + span, :]
    off = HALO - CONV_WIDTH // 2
    acc = jnp.zeros(cur_ref.shape, F32) + b_ref[...]
    for tap in range(CONV_WIDTH):
        phase = (off + tap) % SUBLANES
        base = off + tap - phase
        win = pad_ref[base:base + tr, :] if phase == 0 else shift_ref[phase, base:base + tr, :]
        acc = acc + win * w_ref[tap:tap + 1, :]
    mu = jnp.mean(acc, axis=-1, keepdims=True)
    cen = acc - mu
    y = cen * lax.rsqrt(jnp.mean(cen * cen, axis=-1, keepdims=True) + NORM_EPS) * g_ref[...] + beta_ref[...]
    o_ref[...] = (y * _sigmoid(y)).astype(BF16)


def _conv(u, w, bias, g, beta, tr):
    b, l, ch = u.shape
    per = tr // HALO
    last = l // HALO - 1
    return pl.pallas_call(
        _conv_kernel,
        grid=(b, l // tr),
        in_specs=[pl.BlockSpec((None, HALO, ch), lambda i, j: (i, jnp.maximum(j * per - 1, 0), 0)),
                  pl.BlockSpec((None, tr, ch), lambda i, j: (i, j, 0)),
                  pl.BlockSpec((None, HALO, ch), lambda i, j: (i, jnp.minimum((j + 1) * per, last), 0)),
                  _const_spec(w.shape), _const_spec(bias.shape), _const_spec(g.shape), _const_spec(beta.shape)],
        out_specs=pl.BlockSpec((None, tr, ch), lambda i, j: (i, j, 0)),
        out_shape=jax.ShapeDtypeStruct((b, l, ch), BF16),
        scratch_shapes=[pltpu.VMEM((tr + 2 * HALO, ch), F32),
                        pltpu.VMEM((SUBLANES, tr + 2 * HALO - SUBLANES, ch), F32)],
        compiler_params=_params("parallel", "parallel"),
        name="conv_ln_silu",
    )(u, u, u, w, bias, g, beta)


def _route(logits):
    lane = lax.broadcasted_iota(jnp.int32, logits.shape, 1)
    neg = jnp.float32(-jnp.inf)
    big = jnp.int32(LANES)
    is_grp = (lane >= N_EXPERTS) & (lane < N_EXPERTS + N_GROUPS)
    gl = jnp.where(is_grp, logits, neg)
    gmax = jnp.max(gl, axis=-1, keepdims=True)
    gsel = jnp.min(jnp.where(gl == gmax, lane, big), axis=-1, keepdims=True) - N_EXPERTS
    p_g = 1.0 / jnp.sum(jnp.exp(gl - gmax), axis=-1, keepdims=True)
    in_grp = (lane >= gsel * EXPERTS_PER_GROUP) & (lane < (gsel + 1) * EXPERTS_PER_GROUP)
    e1 = jnp.where(in_grp, logits, neg)
    v1 = jnp.max(e1, axis=-1, keepdims=True)
    i1 = jnp.min(jnp.where(e1 == v1, lane, big), axis=-1, keepdims=True)
    e2 = jnp.where(lane == i1, neg, e1)
    v2 = jnp.max(e2, axis=-1, keepdims=True)
    i2 = jnp.min(jnp.where(e2 == v2, lane, big), axis=-1, keepdims=True)
    r = jnp.exp(v2 - v1)
    w1 = p_g / (1.0 + r)
    comb = jnp.where(lane == i1, w1, jnp.where(lane == i2, w1 * r, 0.0))
    return jnp.where(lane == GROUP_LANE, gsel.astype(F32), comb)


def _merge_kernel(x_ref, om_ref, oc_ref, od_ref, gates_ref, mod_ref, wm_ref, wc_ref, wd_ref, wout_ref,
                  n2g_ref, wr_ref, br_ref, xn_ref, h2_ref, comb_ref):
    tm, d = x_ref.shape
    parts = [slice(r, r + MERGE_PART) for r in range(0, tm, MERGE_PART)]

    def branch_sum(rows):
        gate = lambda j: gates_ref[rows, j * d:(j + 1) * d].astype(F32)
        return (gate(0) * jnp.dot(om_ref[rows, :], wm_ref[...], preferred_element_type=F32)
                + gate(1) * jnp.dot(oc_ref[rows, :], wc_ref[...], preferred_element_type=F32)
                + gate(2) * jnp.dot(od_ref[rows, :], wd_ref[...], preferred_element_type=F32))

    def residual_norm(rows, y):
        z = jnp.dot(y.astype(BF16), wout_ref[...], preferred_element_type=F32)
        xn = x_ref[rows, :] + mod_ref[2:3, :] * z
        xn_ref[rows, :] = xn
        h2 = _rms(xn, n2g_ref[...]) * (1.0 + mod_ref[4:5, :]) + mod_ref[3:4, :]
        h_hi = h2.astype(BF16)
        h2_ref[rows, :] = h_hi
        return h_hi, (h2 - h_hi.astype(F32)).astype(BF16)

    def router(rows, h_hi, h_lo):
        hw = jnp.dot(h_hi, wr_ref[...], preferred_element_type=F32)
        lw = jnp.dot(h_lo, wr_ref[:, 0:LANES], preferred_element_type=F32)
        comb_ref[rows, :] = _route(hw[:, 0:LANES] + hw[:, LANES:] + lw + br_ref[...])

    ys = [branch_sum(rows) for rows in parts]
    hs = [residual_norm(rows, y) for rows, y in zip(parts, ys)]
    for rows, (h_hi, h_lo) in zip(parts, hs):
        router(rows, h_hi, h_lo)


def _merge(x, om, oc, od, gates, mod, mod_row, wm, wc, wd, wout, n2g, wr, br, seq, tm):
    t, d = x.shape
    tiles_per_seq = seq // tm
    row = lambda i: (i, 0)
    if mod_row is None:
        mod_map = lambda i: (i // tiles_per_seq, 0, 0)
    else:
        mod_map = lambda i: (mod_row, 0, 0)
    rows = lambda a: pl.BlockSpec((tm, a.shape[1]), row)
    return pl.pallas_call(
        _merge_kernel,
        grid=(t // tm,),
        in_specs=[rows(x), rows(om), rows(oc), rows(od), rows(gates),
                  pl.BlockSpec((None, 6, d), mod_map)]
                 + [_const_spec(a.shape) for a in (wm, wc, wd, wout, n2g, wr, br)],
        out_specs=[pl.BlockSpec((tm, d), row), pl.BlockSpec((tm, d), row), pl.BlockSpec((tm, LANES), row)],
        out_shape=[jax.ShapeDtypeStruct((t, d), F32), jax.ShapeDtypeStruct((t, d), BF16),
                   jax.ShapeDtypeStruct((t, LANES), F32)],
        compiler_params=_params("parallel"),
        name="merge_route",
    )(x, om, oc, od, gates, mod, wm, wc, wd, wout, n2g, wr, br)


def _moe_kernel(h_ref, comb_ref, xn_ref, mod_ref, tri_ref, wg_ref, wu_ref, wd_ref, fg_ref, o_ref,
                pos_ref, comb3_ref, *, final, mod_rows):
    g = pl.program_id(1)
    slabs, slab_rows, d = o_ref.shape
    tm = slabs * slab_rows

    @pl.when(g == 0)
    def _():
        o_ref[...] = jnp.zeros_like(o_ref)
        comb = comb_ref[...].reshape(tm, LANES)
        hi = comb.astype(BF16)
        r1 = comb - hi.astype(F32)
        mid = r1.astype(BF16)
        lo = (r1 - mid.astype(F32)).astype(BF16)
        comb3_ref[...] = jnp.concatenate([hi, mid, lo], axis=1)
        grow = comb.T[GROUP_LANE:GROUP_LANE + 1, :]
        gid = lax.broadcasted_iota(jnp.int32, (8, tm), 0).astype(F32)
        member = grow == gid
        rank = jnp.dot(jnp.where(member, 1.0, 0.0).astype(BF16), tri_ref[...], preferred_element_type=F32)
        pos_ref[...] = jnp.where(member, rank, -1.0)

    row = lax.broadcasted_iota(jnp.int32, (8, tm), 0)
    posg = jnp.sum(jnp.where(row == g, pos_ref[...], 0.0), axis=0, keepdims=True)
    n_chunks = (jnp.max(posg).astype(jnp.int32) + MOE_CHUNK) // MOE_CHUNK

    def chunk(c, carry):
        slot = (lax.broadcasted_iota(jnp.int32, (MOE_CHUNK, tm), 0) + c * MOE_CHUNK).astype(F32)
        sel = jnp.where(posg == slot, 1.0, 0.0).astype(BF16)
        h = h_ref[...].reshape(tm, d)
        x = jnp.dot(sel, h, preferred_element_type=F32).astype(BF16)
        cw3 = jnp.dot(sel, comb3_ref[...], preferred_element_type=F32)
        cw = cw3[:, 0:LANES] + cw3[:, LANES:2 * LANES] + cw3[:, 2 * LANES:]
        lane = lax.broadcasted_iota(jnp.int32, cw.shape, 1)
        y = jnp.zeros((MOE_CHUNK, d), F32)
        for e in range(EXPERTS_PER_GROUP):
            a = jnp.dot(x, wg_ref[e], preferred_element_type=F32)
            u = jnp.dot(x, wu_ref[e], preferred_element_type=F32)
            w = jnp.sum(jnp.where(lane == g * EXPERTS_PER_GROUP + e, cw, 0.0), axis=1, keepdims=True)
            hid = a * _sigmoid(a) * u * w
            y = y + jnp.dot(hid.astype(BF16), wd_ref[e], preferred_element_type=F32)
        back = lax.dot_general(sel, y.astype(BF16), (((0,), (0,)), ((), ())), preferred_element_type=F32)
        o_ref[...] += back.reshape(slabs, slab_rows, d)
        return carry

    lax.fori_loop(0, n_chunks, chunk, 0)

    @pl.when(g == pl.num_programs(1) - 1)
    def _():
        for sl, mod_row in enumerate(mod_rows):
            out = xn_ref[sl] + mod_ref[mod_row, 5:6, :] * o_ref[sl]
            o_ref[sl] = _rms(out, fg_ref[...]) if final else out


def _moe(h2, comb, xn, mod, mod_rows, wg, wu, wd, final_g, tm, final):
    t, d = xn.shape
    slabs = len(mod_rows)
    slab_rows = tm // slabs
    n_tiles = t // tm
    view = lambda a: a.reshape(slabs, n_tiles, slab_rows, a.shape[-1])
    tile = lambda a: pl.BlockSpec((slabs, None, slab_rows, a.shape[-1]), lambda i, g: (0, i, 0, 0))
    ff = wg.shape[2]
    idx = jnp.arange(tm)
    tri = (idx[:, None] < idx[None, :]).astype(BF16)
    grp = lambda i, g: (g, 0, 0)
    once = pl.Buffered(1)
    xn_spec = pl.BlockSpec((slabs, None, slab_rows, d), lambda i, g: (0, i, 0, 0))
    out = pl.pallas_call(
        functools.partial(_moe_kernel, final=final, mod_rows=tuple(mod_rows)),
        grid=(n_tiles, N_GROUPS),
        in_specs=[tile(h2), tile(comb), xn_spec,
                  _const_spec(mod.shape),
                  pl.BlockSpec((tm, tm), lambda i, g: (0, 0), pipeline_mode=once),
                  pl.BlockSpec((EXPERTS_PER_GROUP, d, ff), grp),
                  pl.BlockSpec((EXPERTS_PER_GROUP, d, ff), grp),
                  pl.BlockSpec((EXPERTS_PER_GROUP, ff, d), grp),
                  pl.BlockSpec(final_g.shape, lambda i, g: (0, 0))],
        out_specs=tile(xn),
        out_shape=jax.ShapeDtypeStruct((slabs, n_tiles, slab_rows, d), F32),
        scratch_shapes=[pltpu.VMEM((8, tm), F32), pltpu.VMEM((tm, 3 * LANES), BF16)],
        compiler_params=pltpu.CompilerParams(dimension_semantics=("parallel", "arbitrary"),
                                             vmem_limit_bytes=MOE_VMEM_LIMIT),
        name="moe",
    )(view(h2), view(comb), view(xn), mod, tri, wg, wu, wd, final_g)
    return out.reshape(t, d)


def _rope_tables(s):
    rows = s // GRID_W
    row = jnp.repeat(jnp.arange(rows, dtype=F32), GRID_W)
    col = jnp.tile(jnp.arange(GRID_W, dtype=F32), rows)

    def cos_sin(rot_dim):
        n = rot_dim // 4
        inv = ROPE_BASE ** (-jnp.arange(n, dtype=F32) / n)
        ang = jnp.concatenate([row[:, None] * inv, col[:, None] * inv], axis=-1)
        return jnp.cos(ang), jnp.sin(ang)

    ca, sa = cos_sin(MLA_ROPE)
    one = jnp.ones((s, MLA_NOPE), F32)
    pad1 = jnp.ones((s, LANES - MLA_NOPE - MLA_ROPE), F32)
    cos_a = jnp.concatenate([one, ca, ca, pad1], axis=-1)
    sin_a = jnp.concatenate([0 * one, -sa, sa, 0 * pad1], axis=-1)
    cd, sd = cos_sin(DIFF_HD)
    cos_d = jnp.concatenate([cd, cd, cd, cd], axis=-1)
    sin_d = jnp.concatenate([-sd, sd, -sd, sd], axis=-1)
    return cos_a, sin_a, cos_d, sin_d


def _layout_w_in(w_in):
    d = w_in.shape[0]
    o = 0
    parts = {}
    for name, width in (("cq", Q_LORA), ("ckv", KV_LORA), ("kr", MLA_ROPE), ("glu", 2 * CONV_CH),
                        ("dq", 2 * DIFF_HEADS * DIFF_HD), ("dk", 2 * DIFF_HEADS * DIFF_HD),
                        ("dv", 2 * DIFF_HEADS * DIFF_HD), ("gates", 3 * d)):
        parts[name] = w_in[:, o:o + width]
        o += width
    kr_blk = jnp.concatenate([jnp.zeros((d, MLA_NOPE), w_in.dtype), parts["kr"],
                              jnp.zeros((d, LANES - MLA_NOPE - MLA_ROPE), w_in.dtype)], axis=1)
    win = jnp.concatenate([parts["cq"], parts["ckv"], kr_blk, parts["glu"], parts["dq"], parts["dk"],
                           parts["gates"]], axis=1)
    return win.astype(BF16), parts["dv"].T.astype(BF16)


def _layout_w_uq(w_uq):
    r = w_uq.shape[0]
    w = w_uq.reshape(r, MLA_HEADS, MLA_NOPE + MLA_ROPE)
    w = jnp.pad(w, ((0, 0), (0, 0), (0, LANES - MLA_NOPE - MLA_ROPE)))
    return w.reshape(r, MLA_HEADS * LANES).astype(BF16)


def _layout_w_ukv(w_ukv):
    r = w_ukv.shape[0]
    w = w_ukv.reshape(r, MLA_HEADS, MLA_NOPE + MLA_V)
    wk = jnp.pad(w[:, :, :MLA_NOPE], ((0, 0), (0, 0), (0, LANES - MLA_NOPE))).reshape(r, MLA_HEADS * LANES)
    wvt = w[:, :, MLA_NOPE:].reshape(r, MLA_HEADS * MLA_V).T
    return wk.astype(BF16), wvt.astype(BF16)


def _layout_router(w_rg, b_rg, w_re, b_re):
    d = w_rg.shape[0]
    w = jnp.concatenate([w_re, w_rg, jnp.zeros((d, LANES - N_EXPERTS - N_GROUPS), F32)], axis=1)
    hi = w.astype(BF16)
    lo = (w - hi.astype(F32)).astype(BF16)
    b = jnp.concatenate([b_re, b_rg, jnp.zeros((LANES - N_EXPERTS - N_GROUPS,), F32)])[None, :]
    return jnp.concatenate([hi, lo], axis=1), b


def kernel(x, c, ctx, c_ctx, w_ada, b_ada, norm1_g, w_in, q_norm_g, w_uq, kv_norm_g, w_ukv, conv_w, conv_b,
           conv_ln_g, conv_ln_b, lam_q1, lam_k1, lam_q2, lam_k2, diff_subln_g, w_o_mla, w_o_conv, w_o_diff,
           w_out, norm2_g, w_rg, b_rg, w_re, b_re, w_gate, w_up, w_down, final_g):
    b, s, d = x.shape
    n_ctx = ctx.shape[1]
    depth = w_ada.shape[0]
    tm = 512
    tq = 4 * Q_SUB
    tq_c = min(tq, n_ctx)
    tk = min(256, s)
    tr = 256
    tmoe_l = min(1024, b * s)
    tmoe_c = min(1024, b * n_ctx)
    ctx_row = b

    rows = -(-(b + 1) // 8) * 8
    cc = jnp.zeros((rows, d), F32).at[:b].set(c).at[b].set(c_ctx)
    mod_all = _ada(cc, w_ada, b_ada).reshape(depth, rows, 6, d)
    tables = _rope_tables(s)
    row2 = lambda a: a[None, :]

    xl = x.reshape(b * s, d)
    xc = ctx.reshape(b * n_ctx, d)
    for l in range(depth):
        update_ctx = l < depth - 1
        final = l == depth - 1
        lam_init = 0.8 - 0.6 * math.exp(-0.3 * l)
        mod = mod_all[l]
        win, wdvt = _layout_w_in(w_in[l])
        wuq = _layout_w_uq(w_uq[l])
        wuk, wuvt = _layout_w_ukv(w_ukv[l])
        n1g, qng, kvg = row2(norm1_g[l]), row2(q_norm_g[l]), row2(kv_norm_g[l])
        lam_p = jnp.stack([lam_q1[l], lam_k1[l], lam_q2[l], lam_k2[l]])
        subln = diff_subln_g[l][:, None]

        pc = _proj(xc, mod, ctx_row, n1g, win, qng, wuq, kvg, wuk, wuvt, wdvt, None, n_ctx, tm)
        pl_ = _proj(xl, mod, None, n1g, win, qng, wuq, kvg, wuk, wuvt, wdvt, tables, s, tm)
        per_batch = lambda arrs, n: [a if i in (2, 6) else a.reshape(b, n, -1) for i, a in enumerate(arrs)]
        qc, kc, vc, uc, dqc, dkc, dvc, gc = per_batch(pc, n_ctx)
        ql, kl, vl, ul, dql, dkl, dvl, gl = per_batch(pl_, s)

        wm, wc, wd = w_o_mla[l].astype(BF16), w_o_conv[l].astype(BF16), w_o_diff[l].astype(BF16)
        wout = w_out[l].astype(BF16)
        wr, br = _layout_router(w_rg[l], b_rg[l], w_re[l], b_re[l])
        n2g = row2(norm2_g[l])
        cw, cb = conv_w[l], row2(conv_b[l])
        cg, cbeta = row2(conv_ln_g[l]), row2(conv_ln_b[l])
        wg, wu, wdn = w_gate[l].astype(BF16), w_up[l].astype(BF16), w_down[l].astype(BF16)
        fg = row2(final_g)

        def tail(xs, om, oc, od, gates, mod_row, seq, tmoe, is_final):
            flat = lambda a: a.reshape(-1, a.shape[-1])
            xn, h2, comb = _merge(xs, flat(om), flat(oc), flat(od), flat(gates), mod, mod_row,
                                  wm, wc, wd, wout, n2g, wr, br, seq, tm)
            moe_rows = tuple(range(b)) if mod_row is None else (mod_row,)
            return _moe(h2, comb, xn, mod, moe_rows, wg, wu, wdn, fg, tmoe, is_final)

        om_l = _mla_attn(ql, kc, vc, kl, vl, tq, tk)
        od_l = _diff_attn(lam_p, subln, dql, dkc, dvc, dkl, dvl, tq, tk, lam_init)
        oc_l = _conv(ul, cw, cb, cg, cbeta, tr)
        xl_new = tail(xl, om_l, oc_l, od_l, gl, None, s, tmoe_l, final)
        if update_ctx:
            om_c = _mla_attn(qc, kc, vc, None, None, tq_c, tk)
            od_c = _diff_attn(lam_p, subln, dqc, dkc, dvc, None, None, tq_c, tk, lam_init)
            oc_c = _conv(uc, cw, cb, cg, cbeta, tr)
            xc = tail(xc, om_c, oc_c, od_c, gc, ctx_row, b * n_ctx, tmoe_c, False)
        xl = xl_new
    return xl.reshape(b, s, d)
```

```python
import functools
import math

import jax
import jax.numpy as jnp
from jax import lax
from jax.experimental import pallas as pl
from jax.experimental.pallas import tpu as pltpu

F32 = jnp.float32
BF16 = jnp.bfloat16

GRID_W = 64
ROPE_BASE = 10000.0
NORM_EPS = 1e-6

MLA_HEADS = 8
MLA_NOPE = 64
MLA_ROPE = 32
MLA_V = 64
Q_LORA = 256
KV_LORA = 128
CONV_CH = 512
CONV_WIDTH = 31
DIFF_HEADS = 4
DIFF_HD = 64
N_GROUPS = 4
EXPERTS_PER_GROUP = 8
N_EXPERTS = N_GROUPS * EXPERTS_PER_GROUP
EXPERT_FF = 256

LANES = 128
SUBLANES = 8
HALO = 16
VMEM_LIMIT = 48 * 1024 * 1024

C_CQ = 0
C_CKV = C_CQ + Q_LORA
C_KR = C_CKV + KV_LORA
C_GA = C_KR + LANES
C_GG = C_GA + CONV_CH
C_DQ = C_GG + CONV_CH
C_DK = C_DQ + 2 * DIFF_HEADS * DIFF_HD
C_GATES = C_DK + 2 * DIFF_HEADS * DIFF_HD

ONES_ROWS = 16
VT_A = MLA_V + ONES_ROWS
VT_D = 2 * DIFF_HD + ONES_ROWS
LOG2E = 1.4426950408889634
GROUP_LANE = N_EXPERTS
MOE_CHUNK = 320
MOE_SMALL_CHUNK = 256
MERGE_PART = 256
MOE_VMEM_LIMIT = 56 * 1024 * 1024
Q_SUB = 256
MLA_LOOKAHEAD = 5
DIFF_LOOKAHEAD = 10


def _params(*sem):
    return pltpu.CompilerParams(dimension_semantics=sem, vmem_limit_bytes=VMEM_LIMIT)


def _sigmoid(x):
    return 0.5 * jnp.tanh(0.5 * x) + 0.5


def _rms(x, g):
    return x * lax.rsqrt(jnp.mean(x * x, axis=-1, keepdims=True) + NORM_EPS) * g


def _dot_nt(a, b):
    return lax.dot_general(a, b, (((1,), (1,)), ((), ())), preferred_element_type=F32)


def _const_spec(shape):
    return pl.BlockSpec(shape, lambda *_: (0,) * len(shape))


def _ada_kernel(c_ref, w_ref, b_ref, o_ref):
    c = c_ref[...]
    h = c * _sigmoid(c)
    o_ref[...] = jnp.dot(h, w_ref[...], preferred_element_type=F32,
                         precision=lax.Precision.HIGHEST) + b_ref[...]


def _ada(cc, w_ada, b_ada):
    depth, d, n = w_ada.shape
    bn = 1536
    rows = cc.shape[0]
    return pl.pallas_call(
        _ada_kernel,
        grid=(depth, n // bn),
        in_specs=[pl.BlockSpec((rows, d), lambda l, j: (0, 0)),
                  pl.BlockSpec((None, d, bn), lambda l, j: (l, 0, j)),
                  pl.BlockSpec((None, 1, bn), lambda l, j: (l, 0, j))],
        out_specs=pl.BlockSpec((None, rows, bn), lambda l, j: (l, 0, j)),
        out_shape=jax.ShapeDtypeStruct((depth, rows, n), F32),
        compiler_params=_params("parallel", "parallel"),
        name="adaln",
    )(cc, w_ada, b_ada.reshape(depth, 1, n))


def _swap_halves(t, half):
    n = t.shape[1]
    lane = lax.broadcasted_iota(jnp.int32, t.shape, 1) % (2 * half)
    return jnp.where(lane < half, pltpu.roll(t, n - half, 1), pltpu.roll(t, half, 1))


def _proj_kernel(*refs, rope, scale_a, scale_d):
    if rope:
        (x_ref, mod_ref, n1g_ref, win_ref, qng_ref, wuq_ref, kvg_ref, wuk_ref, wuvt_ref, wdvt_ref,
         ca_ref, sa_ref, cd_ref, sd_ref,
         q_ref, k_ref, vt_ref, u_ref, dq_ref, dk_ref, dvt_ref, g_ref) = refs
    else:
        (x_ref, mod_ref, n1g_ref, win_ref, qng_ref, wuq_ref, kvg_ref, wuk_ref, wuvt_ref, wdvt_ref,
         q_ref, k_ref, vt_ref, u_ref, dq_ref, dk_ref, dvt_ref, g_ref) = refs

    h = _rms(x_ref[...], n1g_ref[...]) * (1.0 + mod_ref[1:2, :]) + mod_ref[0:1, :]
    hb = h.astype(BF16)

    def proj(a, b):
        return jnp.dot(hb, win_ref[:, a:b], preferred_element_type=F32)

    def rope_a(t):
        return t * ca_ref[...] + _swap_halves(t, MLA_ROPE // 2) * sa_ref[...] if rope else t

    def rope_d(t):
        return t * cd_ref[...] + _swap_halves(t, DIFF_HD // 2) * sd_ref[...] if rope else t

    cq = _rms(proj(C_CQ, C_CKV), qng_ref[...]).astype(BF16)
    ckv = _rms(proj(C_CKV, C_KR), kvg_ref[...]).astype(BF16)
    kr = rope_a(proj(C_KR, C_GA))
    q = jnp.dot(cq, wuq_ref[...], preferred_element_type=F32)
    kn = jnp.dot(ckv, wuk_ref[...], preferred_element_type=F32)
    vt = _dot_nt(wuvt_ref[...], ckv)
    ones = jnp.ones((ONES_ROWS, vt.shape[1]), BF16)
    for hd in range(MLA_HEADS):
        blk = slice(hd * LANES, (hd + 1) * LANES)
        q_ref[:, blk] = (rope_a(q[:, blk]) * scale_a).astype(BF16)
        k_ref[:, blk] = (kn[:, blk] + kr).astype(BF16)
        vt_ref[hd * VT_A:hd * VT_A + MLA_V, :] = vt[hd * MLA_V:(hd + 1) * MLA_V, :].astype(BF16)
        vt_ref[hd * VT_A + MLA_V:(hd + 1) * VT_A, :] = ones

    u_ref[...] = proj(C_GA, C_GG) * _sigmoid(proj(C_GG, C_DQ))

    dq = proj(C_DQ, C_DK)
    dk = proj(C_DK, C_GATES)
    dvt = _dot_nt(wdvt_ref[...], hb)
    for hd in range(DIFF_HEADS):
        blk = slice(hd * LANES, (hd + 1) * LANES)
        dq_ref[:, blk] = (rope_d(dq[:, blk]) * scale_d).astype(BF16)
        dk_ref[:, blk] = rope_d(dk[:, blk]).astype(BF16)
        dvt_ref[hd * VT_D:hd * VT_D + LANES, :] = dvt[blk, :].astype(BF16)
        dvt_ref[hd * VT_D + LANES:(hd + 1) * VT_D, :] = ones

    d = x_ref.shape[1]
    for j in range(3):
        g_ref[:, j * d:(j + 1) * d] = _sigmoid(proj(C_GATES + j * d, C_GATES + (j + 1) * d)).astype(BF16)


def _proj(x, mod, mod_row, n1g, win, qng, wuq, kvg, wuk, wuvt, wdvt, tables, seq, tm):
    t, d = x.shape
    rope = tables is not None
    tiles_per_seq = seq // tm
    row = lambda i: (i, 0)
    col = lambda i: (0, i)
    if mod_row is None:
        mod_map = lambda i: (i // tiles_per_seq, 0, 0)
    else:
        mod_map = lambda i: (mod_row, 0, 0)
    consts = (n1g, win, qng, wuq, kvg, wuk, wuvt, wdvt)
    in_specs = [pl.BlockSpec((tm, d), row), pl.BlockSpec((None, 6, d), mod_map)]
    in_specs += [_const_spec(a.shape) for a in consts]
    args = [x, mod, *consts]
    if rope:
        in_specs += [pl.BlockSpec((tm, LANES), lambda i: (i % tiles_per_seq, 0))] * 4
        args += list(tables)
    outs = [((t, MLA_HEADS * LANES), (tm, MLA_HEADS * LANES), row, BF16),
            ((t, MLA_HEADS * LANES), (tm, MLA_HEADS * LANES), row, BF16),
            ((MLA_HEADS * VT_A, t), (MLA_HEADS * VT_A, tm), col, BF16),
            ((t, CONV_CH), (tm, CONV_CH), row, F32),
            ((t, DIFF_HEADS * LANES), (tm, DIFF_HEADS * LANES), row, BF16),
            ((t, DIFF_HEADS * LANES), (tm, DIFF_HEADS * LANES), row, BF16),
            ((DIFF_HEADS * VT_D, t), (DIFF_HEADS * VT_D, tm), col, BF16),
            ((t, 3 * d), (tm, 3 * d), row, BF16)]
    kern = functools.partial(_proj_kernel, rope=rope,
                             scale_a=LOG2E / math.sqrt(MLA_NOPE + MLA_ROPE), scale_d=LOG2E / math.sqrt(DIFF_HD))
    return pl.pallas_call(
        kern,
        grid=(t // tm,),
        in_specs=in_specs,
        out_specs=[pl.BlockSpec(blk, imap) for _, blk, imap, _ in outs],
        out_shape=[jax.ShapeDtypeStruct(shape, dt) for shape, _, _, dt in outs],
        compiler_params=_params("parallel"),
        name="proj_rope" if rope else "proj_ctx",
    )(*args)


def _flash_streams(streams, lookahead):
    n_chunks = len(streams[0][1])
    items = [(sid, c) for c in range(n_chunks) for sid in range(len(streams))]
    m = [None] * len(streams)
    acc = [None] * len(streams)

    def finish(sid, st, vt):
        cmax = jnp.max(st, axis=0, keepdims=True)
        m_new = cmax if m[sid] is None else jnp.maximum(m[sid], cmax)
        pt = jnp.exp2((st - m_new).astype(BF16))
        pv = jnp.dot(vt, pt, preferred_element_type=F32)
        acc[sid] = pv if m[sid] is None else acc[sid] * jnp.exp2(m[sid] - m_new) + pv
        m[sid] = m_new

    qts = [q.astype(F32).T.astype(BF16) for q, _ in streams]
    pending = []
    for sid, c in items:
        k, vt = streams[sid][1][c]
        st = jnp.dot(k, qts[sid], preferred_element_type=F32)
        if len(pending) == lookahead:
            finish(*pending.pop(0))
        pending.append((sid, st, vt))
    for item in pending:
        finish(*item)
    return acc


def _key_chunks(kc_ref, vtc_ref, kl_ref, vtl_ref, tk, kcols, vrows):
    chunks = [(kc_ref[:, kcols], vtc_ref[vrows, :])]
    if kl_ref is not None:
        for j in range(kl_ref.shape[0] // tk):
            keys = slice(j * tk, (j + 1) * tk)
            chunks.append((kl_ref[keys, kcols], vtl_ref[vrows, keys]))
    return chunks


def _mla_attn_kernel(*refs, latent, tk):
    if latent:
        q_ref, kc_ref, vtc_ref, kl_ref, vtl_ref, o_ref = refs
    else:
        q_ref, kc_ref, vtc_ref, o_ref = refs
        kl_ref = vtl_ref = None
    subs = [slice(r, r + Q_SUB) for r in range(0, q_ref.shape[0], Q_SUB)]
    streams = []
    for qs in subs:
        for hh in range(2):
            cols = slice(hh * LANES, (hh + 1) * LANES)
            rows = slice(hh * VT_A, (hh + 1) * VT_A)
            streams.append((q_ref[qs, cols], _key_chunks(kc_ref, vtc_ref, kl_ref, vtl_ref, tk, cols, rows)))
    outs = [acc[0:MLA_V, :] * (1.0 / acc[MLA_V:MLA_V + 1, :]) for acc in _flash_streams(streams, MLA_LOOKAHEAD)]
    for i, qs in enumerate(subs):
        o_ref[qs, :] = jnp.concatenate(outs[2 * i:2 * i + 2], axis=0).T.astype(BF16)


def _mla_attn(q, kc, vtc, kl, vtl, tq, tk):
    b, lq, _ = q.shape
    c = kc.shape[1]
    latent = kl is not None
    in_specs = [pl.BlockSpec((None, tq, 2 * LANES), lambda i, p, j: (i, j, p)),
                pl.BlockSpec((None, c, 2 * LANES), lambda i, p, j: (i, 0, p)),
                pl.BlockSpec((2 * VT_A, c), lambda i, p, j: (p, i))]
    args = [q, kc, vtc]
    if latent:
        s = kl.shape[1]
        in_specs += [pl.BlockSpec((None, s, 2 * LANES), lambda i, p, j: (i, 0, p)),
                     pl.BlockSpec((2 * VT_A, s), lambda i, p, j: (p, i))]
        args += [kl, vtl]
    return pl.pallas_call(
        functools.partial(_mla_attn_kernel, latent=latent, tk=tk),
        grid=(b, MLA_HEADS // 2, lq // tq),
        in_specs=in_specs,
        out_specs=pl.BlockSpec((None, tq, LANES), lambda i, p, j: (i, j, p)),
        out_shape=jax.ShapeDtypeStruct((b, lq, MLA_HEADS * MLA_V), BF16),
        compiler_params=_params("parallel", "parallel", "parallel"),
        name="mla_attn_lat" if latent else "mla_attn_ctx",
    )(*args)


def _diff_attn_kernel(*refs, latent, tk, lam_init):
    if latent:
        lam_ref, g_ref, q_ref, kc_ref, vtc_ref, kl_ref, vtl_ref, o_ref = refs
    else:
        lam_ref, g_ref, q_ref, kc_ref, vtc_ref, o_ref = refs
        kl_ref = vtl_ref = None
    subs = [slice(r, r + Q_SUB) for r in range(0, q_ref.shape[0], Q_SUB)]
    chunks = _key_chunks(kc_ref, vtc_ref, kl_ref, vtl_ref, tk, slice(None), slice(None))
    streams = []
    for qs in subs:
        q = q_ref[qs, :]
        lane = lax.broadcasted_iota(jnp.int32, q.shape, 1)
        for first in (True, False):
            streams.append((jnp.where((lane < DIFF_HD) == first, q, jnp.zeros_like(q)), chunks))
    outs = [acc[0:LANES, :] * (1.0 / acc[LANES:LANES + 1, :]) for acc in _flash_streams(streams, DIFF_LOOKAHEAD)]
    lam = (jnp.exp(jnp.sum(lam_ref[0:1, :] * lam_ref[1:2, :], axis=-1, keepdims=True))
           - jnp.exp(jnp.sum(lam_ref[2:3, :] * lam_ref[3:4, :], axis=-1, keepdims=True)) + lam_init)
    for i, qs in enumerate(subs):
        o = outs[2 * i] - lam * outs[2 * i + 1]
        o = o * lax.rsqrt(jnp.mean(o * o, axis=0, keepdims=True) + NORM_EPS) * (g_ref[...] * (1.0 - lam_init))
        o_ref[qs, :] = o.T.astype(BF16)


def _diff_attn(lam_p, subln_g, q, kc, vtc, kl, vtl, tq, tk, lam_init):
    b, lq, _ = q.shape
    c = kc.shape[1]
    latent = kl is not None
    kblk = lambda rows: pl.BlockSpec((None, rows, LANES), lambda i, h, j: (i, 0, h))
    vblk = lambda keys: pl.BlockSpec((VT_D, keys), lambda i, h, j: (h, i))
    in_specs = [_const_spec(lam_p.shape), _const_spec(subln_g.shape),
                pl.BlockSpec((None, tq, LANES), lambda i, h, j: (i, j, h)), kblk(c), vblk(c)]
    args = [lam_p, subln_g, q, kc, vtc]
    if latent:
        in_specs += [kblk(kl.shape[1]), vblk(kl.shape[1])]
        args += [kl, vtl]
    return pl.pallas_call(
        functools.partial(_diff_attn_kernel, latent=latent, tk=tk, lam_init=lam_init),
        grid=(b, DIFF_HEADS, lq // tq),
        in_specs=in_specs,
        out_specs=pl.BlockSpec((None, tq, LANES), lambda i, h, j: (i, j, h)),
        out_shape=jax.ShapeDtypeStruct((b, lq, DIFF_HEADS * LANES), BF16),
        compiler_params=_params("parallel", "parallel", "parallel"),
        name="diff_attn_lat" if latent else "diff_attn_ctx",
    )(*args)


def _conv_kernel(prev_ref, cur_ref, next_ref, w_ref, b_ref, g_ref, beta_ref, o_ref, pad_ref, shift_ref):
    j = pl.program_id(1)
    tr = cur_ref.shape[0]
    zero = jnp.zeros((HALO, cur_ref.shape[1]), F32)
    pad_ref[0:HALO, :] = jnp.where(j == 0, zero, prev_ref[...])
    pad_ref[HALO:HALO + tr, :] = cur_ref[...]
    pad_ref[HALO + tr:, :] = jnp.where(j == pl.num_programs(1) - 1, zero, next_ref[...])
    span = shift_ref.shape[1]
    for phase in range(1, SUBLANES):
        shift_ref[phase] = pad_ref[phase:phase + span, :]
    off = HALO - CONV_WIDTH // 2
    acc = jnp.zeros(cur_ref.shape, F32) + b_ref[...]
    for tap in range(CONV_WIDTH):
        phase = (off + tap) % SUBLANES
        base = off + tap - phase
        win = pad_ref[base:base + tr, :] if phase == 0 else shift_ref[phase, base:base + tr, :]
        acc = acc + win * w_ref[tap:tap + 1, :]
    mu = jnp.mean(acc, axis=-1, keepdims=True)
    cen = acc - mu
    y = cen * lax.rsqrt(jnp.mean(cen * cen, axis=-1, keepdims=True) + NORM_EPS) * g_ref[...] + beta_ref[...]
    o_ref[...] = (y * _sigmoid(y)).astype(BF16)


def _conv(u, w, bias, g, beta, tr):
    b, l, ch = u.shape
    per = tr // HALO
    last = l // HALO - 1
    return pl.pallas_call(
        _conv_kernel,
        grid=(b, l // tr),
        in_specs=[pl.BlockSpec((None, HALO, ch), lambda i, j: (i, jnp.maximum(j * per - 1, 0), 0)),
                  pl.BlockSpec((None, tr, ch), lambda i, j: (i, j, 0)),
                  pl.BlockSpec((None, HALO, ch), lambda i, j: (i, jnp.minimum((j + 1) * per, last), 0)),
                  _const_spec(w.shape), _const_spec(bias.shape), _const_spec(g.shape), _const_spec(beta.shape)],
        out_specs=pl.BlockSpec((None, tr, ch), lambda i, j: (i, j, 0)),
        out_shape=jax.ShapeDtypeStruct((b, l, ch), BF16),
        scratch_shapes=[pltpu.VMEM((tr + 2 * HALO, ch), F32),
                        pltpu.VMEM((SUBLANES, tr + 2 * HALO - SUBLANES, ch), F32)],
        compiler_params=_params("parallel", "parallel"),
        name="conv_ln_silu",
    )(u, u, u, w, bias, g, beta)


def _route(logits):
    lane = lax.broadcasted_iota(jnp.int32, logits.shape, 1)
    neg = jnp.float32(-jnp.inf)
    big = jnp.int32(LANES)
    is_grp = (lane >= N_EXPERTS) & (lane < N_EXPERTS + N_GROUPS)
    gl = jnp.where(is_grp, logits, neg)
    gmax = jnp.max(gl, axis=-1, keepdims=True)
    gsel = jnp.min(jnp.where(gl == gmax, lane, big), axis=-1, keepdims=True) - N_EXPERTS
    p_g = 1.0 / jnp.sum(jnp.exp(gl - gmax), axis=-1, keepdims=True)
    in_grp = (lane >= gsel * EXPERTS_PER_GROUP) & (lane < (gsel + 1) * EXPERTS_PER_GROUP)
    e1 = jnp.where(in_grp, logits, neg)
    v1 = jnp.max(e1, axis=-1, keepdims=True)
    i1 = jnp.min(jnp.where(e1 == v1, lane, big), axis=-1, keepdims=True)
    e2 = jnp.where(lane == i1, neg, e1)
    v2 = jnp.max(e2, axis=-1, keepdims=True)
    i2 = jnp.min(jnp.where(e2 == v2, lane, big), axis=-1, keepdims=True)
    r = jnp.exp(v2 - v1)
    w1 = p_g / (1.0 + r)
    comb = jnp.where(lane == i1, w1, jnp.where(lane == i2, w1 * r, 0.0))
    return jnp.where(lane == GROUP_LANE, gsel.astype(F32), comb)


def _merge_kernel(x_ref, om_ref, oc_ref, od_ref, gates_ref, mod_ref, wm_ref, wc_ref, wd_ref, wout_ref,
                  n2g_ref, wr_ref, br_ref, xn_ref, h2_ref, comb_ref):
    tm, d = x_ref.shape
    parts = [slice(r, r + MERGE_PART) for r in range(0, tm, MERGE_PART)]

    def branch_sum(rows):
        gate = lambda j: gates_ref[rows, j * d:(j + 1) * d].astype(F32)
        return (gate(0) * jnp.dot(om_ref[rows, :], wm_ref[...], preferred_element_type=F32)
                + gate(1) * jnp.dot(oc_ref[rows, :], wc_ref[...], preferred_element_type=F32)
                + gate(2) * jnp.dot(od_ref[rows, :], wd_ref[...], preferred_element_type=F32))

    def residual_norm(rows, y):
        z = jnp.dot(y.astype(BF16), wout_ref[...], preferred_element_type=F32)
        xn = x_ref[rows, :] + mod_ref[2:3, :] * z
        xn_ref[rows, :] = xn
        h2 = _rms(xn, n2g_ref[...]) * (1.0 + mod_ref[4:5, :]) + mod_ref[3:4, :]
        h_hi = h2.astype(BF16)
        h2_ref[rows, :] = h_hi
        return h_hi, (h2 - h_hi.astype(F32)).astype(BF16)

    def router(rows, h_hi, h_lo):
        hw = jnp.dot(h_hi, wr_ref[...], preferred_element_type=F32)
        lw = jnp.dot(h_lo, wr_ref[:, 0:LANES], preferred_element_type=F32)
        comb_ref[rows, :] = _route(hw[:, 0:LANES] + hw[:, LANES:] + lw + br_ref[...])

    ys = [branch_sum(rows) for rows in parts]
    hs = [residual_norm(rows, y) for rows, y in zip(parts, ys)]
    for rows, (h_hi, h_lo) in zip(parts, hs):
        router(rows, h_hi, h_lo)


def _merge(x, om, oc, od, gates, mod, mod_row, wm, wc, wd, wout, n2g, wr, br, seq, tm):
    t, d = x.shape
    tiles_per_seq = seq // tm
    row = lambda i: (i, 0)
    if mod_row is None:
        mod_map = lambda i: (i // tiles_per_seq, 0, 0)
    else:
        mod_map = lambda i: (mod_row, 0, 0)
    rows = lambda a: pl.BlockSpec((tm, a.shape[1]), row)
    return pl.pallas_call(
        _merge_kernel,
        grid=(t // tm,),
        in_specs=[rows(x), rows(om), rows(oc), rows(od), rows(gates),
                  pl.BlockSpec((None, 6, d), mod_map)]
                 + [_const_spec(a.shape) for a in (wm, wc, wd, wout, n2g, wr, br)],
        out_specs=[pl.BlockSpec((tm, d), row), pl.BlockSpec((tm, d), row), pl.BlockSpec((tm, LANES), row)],
        out_shape=[jax.ShapeDtypeStruct((t, d), F32), jax.ShapeDtypeStruct((t, d), BF16),
                   jax.ShapeDtypeStruct((t, LANES), F32)],
        compiler_params=_params("parallel"),
        name="merge_route",
    )(x, om, oc, od, gates, mod, wm, wc, wd, wout, n2g, wr, br)


def _moe_kernel(h_ref, comb_ref, xn_ref, mod_ref, tri_ref, wg_ref, wu_ref, wd_ref, fg_ref, o_ref,
                pos_ref, comb3_ref, *, final, mod_rows):
    g = pl.program_id(1)
    slabs, slab_rows, d = o_ref.shape
    tm = slabs * slab_rows

    @pl.when(g == 0)
    def _():
        o_ref[...] = jnp.zeros_like(o_ref)
        comb = comb_ref[...].reshape(tm, LANES)
        hi = comb.astype(BF16)
        r1 = comb - hi.astype(F32)
        mid = r1.astype(BF16)
        lo = (r1 - mid.astype(F32)).astype(BF16)
        comb3_ref[...] = jnp.concatenate([hi, mid, lo], axis=1)
        grow = comb.T[GROUP_LANE:GROUP_LANE + 1, :]
        gid = lax.broadcasted_iota(jnp.int32, (8, tm), 0).astype(F32)
        member = grow == gid
        rank = jnp.dot(jnp.where(member, 1.0, 0.0).astype(BF16), tri_ref[...], preferred_element_type=F32)
        pos_ref[...] = jnp.where(member, rank, -1.0)

    row = lax.broadcasted_iota(jnp.int32, (8, tm), 0)
    posg = jnp.sum(jnp.where(row == g, pos_ref[...], 0.0), axis=0, keepdims=True)
    count = jnp.max(posg).astype(jnp.int32) + 1

    def chunk(c, rows):
        slot = (lax.broadcasted_iota(jnp.int32, (rows, tm), 0) + c * rows).astype(F32)
        sel = jnp.where(posg == slot, 1.0, 0.0).astype(BF16)
        h = h_ref[...].reshape(tm, d)
        x = jnp.dot(sel, h, preferred_element_type=F32).astype(BF16)
        cw3 = jnp.dot(sel, comb3_ref[...], preferred_element_type=F32)
        cw = cw3[:, 0:LANES] + cw3[:, LANES:2 * LANES] + cw3[:, 2 * LANES:]
        lane = lax.broadcasted_iota(jnp.int32, cw.shape, 1)
        y = jnp.zeros((rows, d), F32)
        for e in range(EXPERTS_PER_GROUP):
            a = jnp.dot(x, wg_ref[e], preferred_element_type=F32)
            u = jnp.dot(x, wu_ref[e], preferred_element_type=F32)
            w = jnp.sum(jnp.where(lane == g * EXPERTS_PER_GROUP + e, cw, 0.0), axis=1, keepdims=True)
            hid = a * _sigmoid(a) * u * w
            y = y + jnp.dot(hid.astype(BF16), wd_ref[e], preferred_element_type=F32)
        back = lax.dot_general(sel, y.astype(BF16), (((0,), (0,)), ((), ())), preferred_element_type=F32)
        o_ref[...] += back.reshape(slabs, slab_rows, d)

    @pl.when((count > 0) & (count <= MOE_SMALL_CHUNK))
    def _():
        chunk(0, MOE_SMALL_CHUNK)

    @pl.when(count > MOE_SMALL_CHUNK)
    def _():
        def body(c, carry):
            chunk(c, MOE_CHUNK)
            return carry
        lax.fori_loop(0, (count + MOE_CHUNK - 1) // MOE_CHUNK, body, 0)

    @pl.when(g == pl.num_programs(1) - 1)
    def _():
        for sl, mod_row in enumerate(mod_rows):
            out = xn_ref[sl] + mod_ref[mod_row, 5:6, :] * o_ref[sl]
            o_ref[sl] = _rms(out, fg_ref[...]) if final else out


def _moe(h2, comb, xn, mod, mod_rows, wg, wu, wd, final_g, tm, final):
    t, d = xn.shape
    slabs = len(mod_rows)
    slab_rows = tm // slabs
    n_tiles = t // tm
    view = lambda a: a.reshape(slabs, n_tiles, slab_rows, a.shape[-1])
    tile = lambda a: pl.BlockSpec((slabs, None, slab_rows, a.shape[-1]), lambda i, g: (0, i, 0, 0))
    ff = wg.shape[2]
    idx = jnp.arange(tm)
    tri = (idx[:, None] < idx[None, :]).astype(BF16)
    grp = lambda i, g: (g, 0, 0)
    once = pl.Buffered(1)
    xn_spec = pl.BlockSpec((slabs, None, slab_rows, d), lambda i, g: (0, i, 0, 0))
    out = pl.pallas_call(
        functools.partial(_moe_kernel, final=final, mod_rows=tuple(mod_rows)),
        grid=(n_tiles, N_GROUPS),
        in_specs=[tile(h2), tile(comb), xn_spec,
                  _const_spec(mod.shape),
                  pl.BlockSpec((tm, tm), lambda i, g: (0, 0), pipeline_mode=once),
                  pl.BlockSpec((EXPERTS_PER_GROUP, d, ff), grp),
                  pl.BlockSpec((EXPERTS_PER_GROUP, d, ff), grp),
                  pl.BlockSpec((EXPERTS_PER_GROUP, ff, d), grp),
                  pl.BlockSpec(final_g.shape, lambda i, g: (0, 0))],
        out_specs=tile(xn),
        out_shape=jax.ShapeDtypeStruct((slabs, n_tiles, slab_rows, d), F32),
        scratch_shapes=[pltpu.VMEM((8, tm), F32), pltpu.VMEM((tm, 3 * LANES), BF16)],
        compiler_params=pltpu.CompilerParams(dimension_semantics=("parallel", "arbitrary"),
                                             vmem_limit_bytes=MOE_VMEM_LIMIT),
        name="moe",
    )(view(h2), view(comb), view(xn), mod, tri, wg, wu, wd, final_g)
    return out.reshape(t, d)


def _rope_tables(s):
    rows = s // GRID_W
    row = jnp.repeat(jnp.arange(rows, dtype=F32), GRID_W)
    col = jnp.tile(jnp.arange(GRID_W, dtype=F32), rows)

    def cos_sin(rot_dim):
        n = rot_dim // 4
        inv = ROPE_BASE ** (-jnp.arange(n, dtype=F32) / n)
        ang = jnp.concatenate([row[:, None] * inv, col[:, None] * inv], axis=-1)
        return jnp.cos(ang), jnp.sin(ang)

    ca, sa = cos_sin(MLA_ROPE)
    one = jnp.ones((s, MLA_NOPE), F32)
    pad1 = jnp.ones((s, LANES - MLA_NOPE - MLA_ROPE), F32)
    cos_a = jnp.concatenate([one, ca, ca, pad1], axis=-1)
    sin_a = jnp.concatenate([0 * one, -sa, sa, 0 * pad1], axis=-1)
    cd, sd = cos_sin(DIFF_HD)
    cos_d = jnp.concatenate([cd, cd, cd, cd], axis=-1)
    sin_d = jnp.concatenate([-sd, sd, -sd, sd], axis=-1)
    return cos_a, sin_a, cos_d, sin_d


def _layout_w_in(w_in):
    d = w_in.shape[0]
    o = 0
    parts = {}
    for name, width in (("cq", Q_LORA), ("ckv", KV_LORA), ("kr", MLA_ROPE), ("glu", 2 * CONV_CH),
                        ("dq", 2 * DIFF_HEADS * DIFF_HD), ("dk", 2 * DIFF_HEADS * DIFF_HD),
                        ("dv", 2 * DIFF_HEADS * DIFF_HD), ("gates", 3 * d)):
        parts[name] = w_in[:, o:o + width]
        o += width
    kr_blk = jnp.concatenate([jnp.zeros((d, MLA_NOPE), w_in.dtype), parts["kr"],
                              jnp.zeros((d, LANES - MLA_NOPE - MLA_ROPE), w_in.dtype)], axis=1)
    win = jnp.concatenate([parts["cq"], parts["ckv"], kr_blk, parts["glu"], parts["dq"], parts["dk"],
                           parts["gates"]], axis=1)
    return win.astype(BF16), parts["dv"].T.astype(BF16)


def _layout_w_uq(w_uq):
    r = w_uq.shape[0]
    w = w_uq.reshape(r, MLA_HEADS, MLA_NOPE + MLA_ROPE)
    w = jnp.pad(w, ((0, 0), (0, 0), (0, LANES - MLA_NOPE - MLA_ROPE)))
    return w.reshape(r, MLA_HEADS * LANES).astype(BF16)


def _layout_w_ukv(w_ukv):
    r = w_ukv.shape[0]
    w = w_ukv.reshape(r, MLA_HEADS, MLA_NOPE + MLA_V)
    wk = jnp.pad(w[:, :, :MLA_NOPE], ((0, 0), (0, 0), (0, LANES - MLA_NOPE))).reshape(r, MLA_HEADS * LANES)
    wvt = w[:, :, MLA_NOPE:].reshape(r, MLA_HEADS * MLA_V).T
    return wk.astype(BF16), wvt.astype(BF16)


def _layout_router(w_rg, b_rg, w_re, b_re):
    d = w_rg.shape[0]
    w = jnp.concatenate([w_re, w_rg, jnp.zeros((d, LANES - N_EXPERTS - N_GROUPS), F32)], axis=1)
    hi = w.astype(BF16)
    lo = (w - hi.astype(F32)).astype(BF16)
    b = jnp.concatenate([b_re, b_rg, jnp.zeros((LANES - N_EXPERTS - N_GROUPS,), F32)])[None, :]
    return jnp.concatenate([hi, lo], axis=1), b


def kernel(x, c, ctx, c_ctx, w_ada, b_ada, norm1_g, w_in, q_norm_g, w_uq, kv_norm_g, w_ukv, conv_w, conv_b,
           conv_ln_g, conv_ln_b, lam_q1, lam_k1, lam_q2, lam_k2, diff_subln_g, w_o_mla, w_o_conv, w_o_diff,
           w_out, norm2_g, w_rg, b_rg, w_re, b_re, w_gate, w_up, w_down, final_g):
    b, s, d = x.shape
    n_ctx = ctx.shape[1]
    depth = w_ada.shape[0]
    tm = 512
    tq = min(4 * Q_SUB, s)
    tq_c = min(tq, n_ctx)
    tk = min(256, s)
    tr = 256
    tmoe_l = min(1024, b * s)
    tmoe_c = min(1024, b * n_ctx)
    ctx_row = b

    rows = -(-(b + 1) // 8) * 8
    cc = jnp.zeros((rows, d), F32).at[:b].set(c).at[b].set(c_ctx)
    mod_all = _ada(cc, w_ada, b_ada).reshape(depth, rows, 6, d)
    tables = _rope_tables(s)
    row2 = lambda a: a[None, :]

    xl = x.reshape(b * s, d)
    xc = ctx.reshape(b * n_ctx, d)
    for l in range(depth):
        update_ctx = l < depth - 1
        final = l == depth - 1
        lam_init = 0.8 - 0.6 * math.exp(-0.3 * l)
        mod = mod_all[l]
        win, wdvt = _layout_w_in(w_in[l])
        wuq = _layout_w_uq(w_uq[l])
        wuk, wuvt = _layout_w_ukv(w_ukv[l])
        n1g, qng, kvg = row2(norm1_g[l]), row2(q_norm_g[l]), row2(kv_norm_g[l])
        lam_p = jnp.stack([lam_q1[l], lam_k1[l], lam_q2[l], lam_k2[l]])
        subln = diff_subln_g[l][:, None]

        pc = _proj(xc, mod, ctx_row, n1g, win, qng, wuq, kvg, wuk, wuvt, wdvt, None, n_ctx, tm)
        pl_ = _proj(xl, mod, None, n1g, win, qng, wuq, kvg, wuk, wuvt, wdvt, tables, s, tm)
        per_batch = lambda arrs, n: [a if i in (2, 6) else a.reshape(b, n, -1) for i, a in enumerate(arrs)]
        qc, kc, vc, uc, dqc, dkc, dvc, gc = per_batch(pc, n_ctx)
        ql, kl, vl, ul, dql, dkl, dvl, gl = per_batch(pl_, s)

        wm, wc, wd = w_o_mla[l].astype(BF16), w_o_conv[l].astype(BF16), w_o_diff[l].astype(BF16)
        wout = w_out[l].astype(BF16)
        wr, br = _layout_router(w_rg[l], b_rg[l], w_re[l], b_re[l])
        n2g = row2(norm2_g[l])
        cw, cb = conv_w[l], row2(conv_b[l])
        cg, cbeta = row2(conv_ln_g[l]), row2(conv_ln_b[l])
        wg, wu, wdn = w_gate[l].astype(BF16), w_up[l].astype(BF16), w_down[l].astype(BF16)
        fg = row2(final_g)

        def tail(xs, om, oc, od, gates, mod_row, seq, tmoe, is_final):
            flat = lambda a: a.reshape(-1, a.shape[-1])
            xn, h2, comb = _merge(xs, flat(om), flat(oc), flat(od), flat(gates), mod, mod_row,
                                  wm, wc, wd, wout, n2g, wr, br, seq, tm)
            moe_rows = tuple(range(b)) if mod_row is None else (mod_row,)
            return _moe(h2, comb, xn, mod, moe_rows, wg, wu, wdn, fg, tmoe, is_final)

        om_l = _mla_attn(ql, kc, vc, kl, vl, tq, tk)
        od_l = _diff_attn(lam_p, subln, dql, dkc, dvc, dkl, dvl, tq, tk, lam_init)
        oc_l = _conv(ul, cw, cb, cg, cbeta, tr)
        xl_new = tail(xl, om_l, oc_l, od_l, gl, None, s, tmoe_l, final)
        if update_ctx:
            om_c = _mla_attn(qc, kc, vc, None, None, tq_c, tk)
            od_c = _diff_attn(lam_p, subln, dqc, dkc, dvc, None, None, tq_c, tk, lam_init)
            oc_c = _conv(uc, cw, cb, cg, cbeta, tr)
            xc = tail(xc, om_c, oc_c, od_c, gc, ctx_row, b * n_ctx, tmoe_c, False)
        xl = xl_new
    return xl.reshape(b, s, d)
```

```python
import functools
import math

import jax
import jax.numpy as jnp
from jax import lax
from jax.experimental import pallas as pl
from jax.experimental.pallas import tpu as pltpu

F32 = jnp.float32
BF16 = jnp.bfloat16

GRID_W = 64
ROPE_BASE = 10000.0
NORM_EPS = 1e-6

MLA_HEADS = 8
MLA_NOPE = 64
MLA_ROPE = 32
MLA_V = 64
Q_LORA = 256
KV_LORA = 128
CONV_CH = 512
CONV_WIDTH = 31
DIFF_HEADS = 4
DIFF_HD = 64
N_GROUPS = 4
EXPERTS_PER_GROUP = 8
N_EXPERTS = N_GROUPS * EXPERTS_PER_GROUP
EXPERT_FF = 256

LANES = 128
SUBLANES = 8
HALO = 16
VMEM_LIMIT = 48 * 1024 * 1024

C_CQ = 0
C_CKV = C_CQ + Q_LORA
C_KR = C_CKV + KV_LORA
C_GA = C_KR + LANES
C_GG = C_GA + CONV_CH
C_DQ = C_GG + CONV_CH
C_DK = C_DQ + 2 * DIFF_HEADS * DIFF_HD
C_GATES = C_DK + 2 * DIFF_HEADS * DIFF_HD

ONES_ROWS = 16
VT_A = MLA_V + ONES_ROWS
VT_D = 2 * DIFF_HD + ONES_ROWS
LOG2E = 1.4426950408889634
MLA_REF_LANE = MLA_NOPE + MLA_ROPE
GROUP_LANE = N_EXPERTS
MOE_CHUNK = 320
MOE_SMALL_CHUNK = 256
MERGE_PART = 256
MOE_VMEM_LIMIT = 56 * 1024 * 1024
Q_SUB = 256
MLA_LOOKAHEAD = 5
DIFF_LOOKAHEAD = 7


def _params(*sem):
    return pltpu.CompilerParams(dimension_semantics=sem, vmem_limit_bytes=VMEM_LIMIT)


def _sigmoid(x):
    return 0.5 * jnp.tanh(0.5 * x) + 0.5


def _rms(x, g):
    return x * lax.rsqrt(jnp.mean(x * x, axis=-1, keepdims=True) + NORM_EPS) * g


def _dot_nt(a, b):
    return lax.dot_general(a, b, (((1,), (1,)), ((), ())), preferred_element_type=F32)


def _const_spec(shape):
    return pl.BlockSpec(shape, lambda *_: (0,) * len(shape))


def _ada_kernel(c_ref, w_ref, b_ref, o_ref):
    c = c_ref[...]
    h = c * _sigmoid(c)
    o_ref[...] = jnp.dot(h, w_ref[...], preferred_element_type=F32,
                         precision=lax.Precision.HIGHEST) + b_ref[...]


def _ada(cc, w_ada, b_ada):
    depth, d, n = w_ada.shape
    bn = 1536
    rows = cc.shape[0]
    return pl.pallas_call(
        _ada_kernel,
        grid=(depth, n // bn),
        in_specs=[pl.BlockSpec((rows, d), lambda l, j: (0, 0)),
                  pl.BlockSpec((None, d, bn), lambda l, j: (l, 0, j)),
                  pl.BlockSpec((None, 1, bn), lambda l, j: (l, 0, j))],
        out_specs=pl.BlockSpec((None, rows, bn), lambda l, j: (l, 0, j)),
        out_shape=jax.ShapeDtypeStruct((depth, rows, n), F32),
        compiler_params=_params("parallel", "parallel"),
        name="adaln",
    )(cc, w_ada, b_ada.reshape(depth, 1, n))


def _swap_halves(t, half):
    n = t.shape[1]
    lane = lax.broadcasted_iota(jnp.int32, t.shape, 1) % (2 * half)
    return jnp.where(lane < half, pltpu.roll(t, n - half, 1), pltpu.roll(t, half, 1))


def _proj_kernel(*refs, rope, scale_a, scale_d):
    if rope:
        (x_ref, mod_ref, n1g_ref, win_ref, qng_ref, wuq_ref, kvg_ref, wuk_ref, wuvt_ref, wdvt_ref,
         ca_ref, sa_ref, cd_ref, sd_ref,
         q_ref, k_ref, vt_ref, u_ref, dq_ref, dk_ref, dvt_ref, g_ref) = refs
    else:
        (x_ref, mod_ref, n1g_ref, win_ref, qng_ref, wuq_ref, kvg_ref, wuk_ref, wuvt_ref, wdvt_ref,
         q_ref, k_ref, vt_ref, u_ref, dq_ref, dk_ref, dvt_ref, g_ref) = refs

    h = _rms(x_ref[...], n1g_ref[...]) * (1.0 + mod_ref[1:2, :]) + mod_ref[0:1, :]
    hb = h.astype(BF16)

    def proj(a, b):
        return jnp.dot(hb, win_ref[:, a:b], preferred_element_type=F32)

    def rope_a(t):
        return t * ca_ref[...] + _swap_halves(t, MLA_ROPE // 2) * sa_ref[...] if rope else t

    def rope_d(t):
        return t * cd_ref[...] + _swap_halves(t, DIFF_HD // 2) * sd_ref[...] if rope else t

    cq = _rms(proj(C_CQ, C_CKV), qng_ref[...]).astype(BF16)
    ckv = _rms(proj(C_CKV, C_KR), kvg_ref[...]).astype(BF16)
    kr = rope_a(proj(C_KR, C_GA))
    lane = lax.broadcasted_iota(jnp.int32, kr.shape, 1)
    kr = jnp.where(lane == MLA_REF_LANE, 1.0, kr)
    q = jnp.dot(cq, wuq_ref[...], preferred_element_type=F32)
    kn = jnp.dot(ckv, wuk_ref[...], preferred_element_type=F32)
    vt = _dot_nt(wuvt_ref[...], ckv)
    ones = jnp.ones((ONES_ROWS, vt.shape[1]), BF16)
    for hd in range(MLA_HEADS):
        blk = slice(hd * LANES, (hd + 1) * LANES)
        q_ref[:, blk] = (rope_a(q[:, blk]) * scale_a).astype(BF16)
        k_ref[:, blk] = (kn[:, blk] + kr).astype(BF16)
        vt_ref[hd * VT_A:hd * VT_A + MLA_V, :] = vt[hd * MLA_V:(hd + 1) * MLA_V, :].astype(BF16)
        vt_ref[hd * VT_A + MLA_V:(hd + 1) * VT_A, :] = ones

    u_ref[...] = proj(C_GA, C_GG) * _sigmoid(proj(C_GG, C_DQ))

    dq = proj(C_DQ, C_DK)
    dk = proj(C_DK, C_GATES)
    dvt = _dot_nt(wdvt_ref[...], hb)
    for hd in range(DIFF_HEADS):
        blk = slice(hd * LANES, (hd + 1) * LANES)
        qh = rope_d(dq[:, blk]) * scale_d
        kh = rope_d(dk[:, blk])
        low = lane < DIFF_HD
        for mp, (qm, km) in enumerate(((qh, kh), (pltpu.roll(qh, DIFF_HD, 1), pltpu.roll(kh, DIFF_HD, 1)))):
            mblk = slice((2 * hd + mp) * LANES, (2 * hd + mp + 1) * LANES)
            dq_ref[:, mblk] = jnp.where(low, qm, 0.0).astype(BF16)
            dk_ref[:, mblk] = jnp.where(low, km, jnp.where(lane == DIFF_HD, 1.0, 0.0)).astype(BF16)
        dvt_ref[hd * VT_D:hd * VT_D + LANES, :] = dvt[blk, :].astype(BF16)
        dvt_ref[hd * VT_D + LANES:(hd + 1) * VT_D, :] = ones

    d = x_ref.shape[1]
    for j in range(3):
        g_ref[:, j * d:(j + 1) * d] = _sigmoid(proj(C_GATES + j * d, C_GATES + (j + 1) * d)).astype(BF16)


def _proj(x, mod, mod_row, n1g, win, qng, wuq, kvg, wuk, wuvt, wdvt, tables, seq, tm):
    t, d = x.shape
    rope = tables is not None
    tiles_per_seq = seq // tm
    row = lambda i: (i, 0)
    col = lambda i: (0, i)
    if mod_row is None:
        mod_map = lambda i: (i // tiles_per_seq, 0, 0)
    else:
        mod_map = lambda i: (mod_row, 0, 0)
    consts = (n1g, win, qng, wuq, kvg, wuk, wuvt, wdvt)
    in_specs = [pl.BlockSpec((tm, d), row), pl.BlockSpec((None, 6, d), mod_map)]
    in_specs += [_const_spec(a.shape) for a in consts]
    args = [x, mod, *consts]
    if rope:
        in_specs += [pl.BlockSpec((tm, LANES), lambda i: (i % tiles_per_seq, 0))] * 4
        args += list(tables)
    outs = [((t, MLA_HEADS * LANES), (tm, MLA_HEADS * LANES), row, BF16),
            ((t, MLA_HEADS * LANES), (tm, MLA_HEADS * LANES), row, BF16),
            ((MLA_HEADS * VT_A, t), (MLA_HEADS * VT_A, tm), col, BF16),
            ((t, CONV_CH), (tm, CONV_CH), row, F32),
            ((t, 2 * DIFF_HEADS * LANES), (tm, 2 * DIFF_HEADS * LANES), row, BF16),
            ((t, 2 * DIFF_HEADS * LANES), (tm, 2 * DIFF_HEADS * LANES), row, BF16),
            ((DIFF_HEADS * VT_D, t), (DIFF_HEADS * VT_D, tm), col, BF16),
            ((t, 3 * d), (tm, 3 * d), row, BF16)]
    kern = functools.partial(_proj_kernel, rope=rope,
                             scale_a=LOG2E / math.sqrt(MLA_NOPE + MLA_ROPE), scale_d=LOG2E / math.sqrt(DIFF_HD))
    return pl.pallas_call(
        kern,
        grid=(t // tm,),
        in_specs=in_specs,
        out_specs=[pl.BlockSpec(blk, imap) for _, blk, imap, _ in outs],
        out_shape=[jax.ShapeDtypeStruct(shape, dt) for shape, _, _, dt in outs],
        compiler_params=_params("parallel"),
        name="proj_rope" if rope else "proj_ctx",
    )(*args)


def _flash_streams(streams, lookahead, ref_row=None):
    n_chunks = len(streams[0][1])
    items = [(sid, c) for c in range(n_chunks) for sid in range(len(streams))]
    m = [None] * len(streams)
    acc = [None] * len(streams)
    ref = [None] * len(streams)
    if ref_row is not None and n_chunks > 1:
        assert len(streams) > 1
        lookahead = min(lookahead, len(streams) - 1)

    def finish(sid, st, vt):
        cmax = jnp.max(st, axis=0, keepdims=True)
        m_new = cmax if m[sid] is None else jnp.maximum(m[sid], cmax)
        pt = jnp.exp2((st - m_new).astype(BF16))
        pv = jnp.dot(vt, pt, preferred_element_type=F32)
        acc[sid] = pv if m[sid] is None else acc[sid] * jnp.exp2(m[sid] - m_new) + pv
        m[sid] = m_new

    def finish_folded(sid, dt, vt):
        db = dt.astype(BF16)
        cmax = jnp.max(db, axis=0, keepdims=True)
        first = m[sid] is None
        delta = cmax if first else jnp.maximum(cmax, jnp.zeros_like(cmax))
        pt = jnp.exp2(db - delta)
        pv = jnp.dot(vt, pt, preferred_element_type=F32)
        m_new = delta.astype(F32) if first else ref[sid] + delta.astype(F32)
        acc[sid] = pv if first else acc[sid] * jnp.exp2(m[sid] - m_new) + pv
        m[sid] = m_new
        ref[sid] = m_new.astype(BF16).astype(F32)

    qts = [q.astype(F32).T.astype(BF16) for q, _ in streams]
    pending = []
    for sid, c in items:
        k, vt = streams[sid][1][c]
        qt = qts[sid]
        if ref_row is not None and ref[sid] is not None:
            row = lax.broadcasted_iota(jnp.int32, qt.shape, 0)
            qt = jnp.where(row == ref_row, jnp.broadcast_to((-ref[sid]).astype(BF16), qt.shape), qt)
        st = jnp.dot(k, qt, preferred_element_type=F32)
        if len(pending) == lookahead:
            (finish if ref_row is None else finish_folded)(*pending.pop(0))
        pending.append((sid, st, vt))
    for item in pending:
        (finish if ref_row is None else finish_folded)(*item)
    return acc


def _key_chunks(kc_ref, vtc_ref, kl_ref, vtl_ref, tk, kcols, vrows):
    chunks = [(kc_ref[:, kcols], vtc_ref[vrows, :])]
    if kl_ref is not None:
        for j in range(kl_ref.shape[0] // tk):
            keys = slice(j * tk, (j + 1) * tk)
            chunks.append((kl_ref[keys, kcols], vtl_ref[vrows, keys]))
    return chunks


def _mla_attn_kernel(*refs, latent, tk):
    if latent:
        q_ref, kc_ref, vtc_ref, kl_ref, vtl_ref, o_ref = refs
    else:
        q_ref, kc_ref, vtc_ref, o_ref = refs
        kl_ref = vtl_ref = None
    subs = [slice(r, r + Q_SUB) for r in range(0, q_ref.shape[0], Q_SUB)]
    streams = []
    for qs in subs:
        for hh in range(2):
            cols = slice(hh * LANES, (hh + 1) * LANES)
            rows = slice(hh * VT_A, (hh + 1) * VT_A)
            streams.append((q_ref[qs, cols], _key_chunks(kc_ref, vtc_ref, kl_ref, vtl_ref, tk, cols, rows)))
    outs = [acc[0:MLA_V, :] * (1.0 / acc[MLA_V:MLA_V + 1, :]) for acc in _flash_streams(streams, MLA_LOOKAHEAD, ref_row=MLA_REF_LANE)]
    for i, qs in enumerate(subs):
        o_ref[qs, :] = jnp.concatenate(outs[2 * i:2 * i + 2], axis=0).T.astype(BF16)


def _mla_attn(q, kc, vtc, kl, vtl, tq, tk):
    b, lq, _ = q.shape
    c = kc.shape[1]
    latent = kl is not None
    in_specs = [pl.BlockSpec((None, tq, 2 * LANES), lambda i, p, j: (i, j, p)),
                pl.BlockSpec((None, c, 2 * LANES), lambda i, p, j: (i, 0, p)),
                pl.BlockSpec((2 * VT_A, c), lambda i, p, j: (p, i))]
    args = [q, kc, vtc]
    if latent:
        s = kl.shape[1]
        in_specs += [pl.BlockSpec((None, s, 2 * LANES), lambda i, p, j: (i, 0, p)),
                     pl.BlockSpec((2 * VT_A, s), lambda i, p, j: (p, i))]
        args += [kl, vtl]
    return pl.pallas_call(
        functools.partial(_mla_attn_kernel, latent=latent, tk=tk),
        grid=(b, MLA_HEADS // 2, lq // tq),
        in_specs=in_specs,
        out_specs=pl.BlockSpec((None, tq, LANES), lambda i, p, j: (i, j, p)),
        out_shape=jax.ShapeDtypeStruct((b, lq, MLA_HEADS * MLA_V), BF16),
        compiler_params=_params("parallel", "parallel", "parallel"),
        name="mla_attn_lat" if latent else "mla_attn_ctx",
    )(*args)


def _diff_attn_kernel(*refs, latent, tk, lam_init):
    if latent:
        lam_ref, g_ref, q_ref, kc_ref, vtc_ref, kl_ref, vtl_ref, o_ref = refs
    else:
        lam_ref, g_ref, q_ref, kc_ref, vtc_ref, o_ref = refs
        kl_ref = vtl_ref = None
    subs = [slice(r, r + Q_SUB) for r in range(0, q_ref.shape[0], Q_SUB)]
    streams = []
    for qs in subs:
        for mp in range(2):
            cols = slice(mp * LANES, (mp + 1) * LANES)
            chunks = _key_chunks(kc_ref, vtc_ref, kl_ref, vtl_ref, tk, cols, slice(None))
            streams.append((q_ref[qs, cols], chunks))
    accs = _flash_streams(streams, DIFF_LOOKAHEAD, ref_row=DIFF_HD)
    outs = [acc[0:LANES, :] * (1.0 / acc[LANES:LANES + 1, :]) for acc in accs]
    lam = (jnp.exp(jnp.sum(lam_ref[0:1, :] * lam_ref[1:2, :], axis=-1, keepdims=True))
           - jnp.exp(jnp.sum(lam_ref[2:3, :] * lam_ref[3:4, :], axis=-1, keepdims=True)) + lam_init)
    for i, qs in enumerate(subs):
        o = outs[2 * i] - lam * outs[2 * i + 1]
        o = o * lax.rsqrt(jnp.mean(o * o, axis=0, keepdims=True) + NORM_EPS) * (g_ref[...] * (1.0 - lam_init))
        o_ref[qs, :] = o.T.astype(BF16)


def _diff_attn(lam_p, subln_g, q, kc, vtc, kl, vtl, tq, tk, lam_init):
    b, lq, _ = q.shape
    c = kc.shape[1]
    latent = kl is not None
    kblk = lambda rows: pl.BlockSpec((None, rows, 2 * LANES), lambda i, h, j: (i, 0, h))
    vblk = lambda keys: pl.BlockSpec((VT_D, keys), lambda i, h, j: (h, i))
    in_specs = [_const_spec(lam_p.shape), _const_spec(subln_g.shape),
                pl.BlockSpec((None, tq, 2 * LANES), lambda i, h, j: (i, j, h)), kblk(c), vblk(c)]
    args = [lam_p, subln_g, q, kc, vtc]
    if latent:
        in_specs += [kblk(kl.shape[1]), vblk(kl.shape[1])]
        args += [kl, vtl]
    return pl.pallas_call(
        functools.partial(_diff_attn_kernel, latent=latent, tk=tk, lam_init=lam_init),
        grid=(b, DIFF_HEADS, lq // tq),
        in_specs=in_specs,
        out_specs=pl.BlockSpec((None, tq, LANES), lambda i, h, j: (i, j, h)),
        out_shape=jax.ShapeDtypeStruct((b, lq, DIFF_HEADS * LANES), BF16),
        compiler_params=_params("parallel", "parallel", "parallel"),
        name="diff_attn_lat" if latent else "diff_attn_ctx",
    )(*args)


def _conv_kernel(prev_ref, cur_ref, next_ref, w_ref, b_ref, g_ref, beta_ref, o_ref, pad_ref, shift_ref):
    j = pl.program_id(1)
    tr = cur_ref.shape[0]
    zero = jnp.zeros((HALO, cur_ref.shape[1]), F32)
    pad_ref[0:HALO, :] = jnp.where(j == 0, zero, prev_ref[...])
    pad_ref[HALO:HALO + tr, :] = cur_ref[...]
    pad_ref[HALO + tr:, :] = jnp.where(j == pl.num_programs(1) - 1, zero, next_ref[...])
    span = shift_ref.shape[1]
    for phase in range(1, SUBLANES):
        shift_ref[phase] = pad_ref[phase:phase + span, :]
    off = HALO - CONV_WIDTH // 2
    acc = jnp.zeros(cur_ref.shape, F32) + b_ref[...]
    for tap in range(CONV_WIDTH):
        phase = (off + tap) % SUBLANES
        base = off + tap - phase
        win = pad_ref[base:base + tr, :] if phase == 0 else shift_ref[phase, base:base + tr, :]
        acc = acc + win * w_ref[tap:tap + 1, :]
    mu = jnp.mean(acc, axis=-1, keepdims=True)
    cen = acc - mu
    y = cen * lax.rsqrt(jnp.mean(cen * cen, axis=-1, keepdims=True) + NORM_EPS) * g_ref[...] + beta_ref[...]
    o_ref[...] = (y * _sigmoid(y)).astype(BF16)


def _conv(u, w, bias, g, beta, tr):
    b, l, ch = u.shape
    per = tr // HALO
    last = l // HALO - 1
    return pl.pallas_call(
        _conv_kernel,
        grid=(b, l // tr),
        in_specs=[pl.BlockSpec((None, HALO, ch), lambda i, j: (i, jnp.maximum(j * per - 1, 0), 0)),
                  pl.BlockSpec((None, tr, ch), lambda i, j: (i, j, 0)),
                  pl.BlockSpec((None, HALO, ch), lambda i, j: (i, jnp.minimum((j + 1) * per, last), 0)),
                  _const_spec(w.shape), _const_spec(bias.shape), _const_spec(g.shape), _const_spec(beta.shape)],
        out_specs=pl.BlockSpec((None, tr, ch), lambda i, j: (i, j, 0)),
        out_shape=jax.ShapeDtypeStruct((b, l, ch), BF16),
        scratch_shapes=[pltpu.VMEM((tr + 2 * HALO, ch), F32),
                        pltpu.VMEM((SUBLANES, tr + 2 * HALO - SUBLANES, ch), F32)],
        compiler_params=_params("parallel", "parallel"),
        name="conv_ln_silu",
    )(u, u, u, w, bias, g, beta)


def _route(logits):
    lane = lax.broadcasted_iota(jnp.int32, logits.shape, 1)
    neg = jnp.float32(-jnp.inf)
    big = jnp.int32(LANES)
    is_grp = (lane >= N_EXPERTS) & (lane < N_EXPERTS + N_GROUPS)
    gl = jnp.where(is_grp, logits, neg)
    gmax = jnp.max(gl, axis=-1, keepdims=True)
    gsel = jnp.min(jnp.where(gl == gmax, lane, big), axis=-1, keepdims=True) - N_EXPERTS
    p_g = 1.0 / jnp.sum(jnp.exp(gl - gmax), axis=-1, keepdims=True)
    in_grp = (lane >= gsel * EXPERTS_PER_GROUP) & (lane < (gsel + 1) * EXPERTS_PER_GROUP)
    e1 = jnp.where(in_grp, logits, neg)
    v1 = jnp.max(e1, axis=-1, keepdims=True)
    i1 = jnp.min(jnp.where(e1 == v1, lane, big), axis=-1, keepdims=True)
    e2 = jnp.where(lane == i1, neg, e1)
    v2 = jnp.max(e2, axis=-1, keepdims=True)
    i2 = jnp.min(jnp.where(e2 == v2, lane, big), axis=-1, keepdims=True)
    r = jnp.exp(v2 - v1)
    w1 = p_g / (1.0 + r)
    comb = jnp.where(lane == i1, w1, jnp.where(lane == i2, w1 * r, 0.0))
    return jnp.where(lane == GROUP_LANE, gsel.astype(F32), comb)


def _merge_kernel(x_ref, om_ref, oc_ref, od_ref, gates_ref, mod_ref, wm_ref, wc_ref, wd_ref, wout_ref,
                  n2g_ref, wr_ref, br_ref, xn_ref, h2_ref, comb_ref):
    tm, d = x_ref.shape
    parts = [slice(r, r + MERGE_PART) for r in range(0, tm, MERGE_PART)]

    def branch_sum(rows):
        gate = lambda j: gates_ref[rows, j * d:(j + 1) * d].astype(F32)
        return (gate(0) * jnp.dot(om_ref[rows, :], wm_ref[...], preferred_element_type=F32)
                + gate(1) * jnp.dot(oc_ref[rows, :], wc_ref[...], preferred_element_type=F32)
                + gate(2) * jnp.dot(od_ref[rows, :], wd_ref[...], preferred_element_type=F32))

    def residual_norm(rows, y):
        z = jnp.dot(y.astype(BF16), wout_ref[...], preferred_element_type=F32)
        xn = x_ref[rows, :] + mod_ref[2:3, :] * z
        xn_ref[rows, :] = xn
        h2 = _rms(xn, n2g_ref[...]) * (1.0 + mod_ref[4:5, :]) + mod_ref[3:4, :]
        h_hi = h2.astype(BF16)
        h2_ref[rows, :] = h_hi
        return h_hi, (h2 - h_hi.astype(F32)).astype(BF16)

    def router(rows, h_hi, h_lo):
        hw = jnp.dot(h_hi, wr_ref[...], preferred_element_type=F32)
        lw = jnp.dot(h_lo, wr_ref[:, 0:LANES], preferred_element_type=F32)
        comb_ref[rows, :] = _route(hw[:, 0:LANES] + hw[:, LANES:] + lw + br_ref[...])

    ys = [branch_sum(rows) for rows in parts]
    hs = [residual_norm(rows, y) for rows, y in zip(parts, ys)]
    for rows, (h_hi, h_lo) in zip(parts, hs):
        router(rows, h_hi, h_lo)


def _merge(x, om, oc, od, gates, mod, mod_row, wm, wc, wd, wout, n2g, wr, br, seq, tm):
    t, d = x.shape
    tiles_per_seq = seq // tm
    row = lambda i: (i, 0)
    if mod_row is None:
        mod_map = lambda i: (i // tiles_per_seq, 0, 0)
    else:
        mod_map = lambda i: (mod_row, 0, 0)
    rows = lambda a: pl.BlockSpec((tm, a.shape[1]), row)
    return pl.pallas_call(
        _merge_kernel,
        grid=(t // tm,),
        in_specs=[rows(x), rows(om), rows(oc), rows(od), rows(gates),
                  pl.BlockSpec((None, 6, d), mod_map)]
                 + [_const_spec(a.shape) for a in (wm, wc, wd, wout, n2g, wr, br)],
        out_specs=[pl.BlockSpec((tm, d), row), pl.BlockSpec((tm, d), row), pl.BlockSpec((tm, LANES), row)],
        out_shape=[jax.ShapeDtypeStruct((t, d), F32), jax.ShapeDtypeStruct((t, d), BF16),
                   jax.ShapeDtypeStruct((t, LANES), F32)],
        compiler_params=_params("parallel"),
        name="merge_route",
    )(x, om, oc, od, gates, mod, wm, wc, wd, wout, n2g, wr, br)


def _moe_kernel(h_ref, comb_ref, xn_ref, mod_ref, tri_ref, wg_ref, wu_ref, wd_ref, fg_ref, o_ref,
                pos_ref, comb3_ref, *, final, mod_rows):
    g = pl.program_id(1)
    slabs, slab_rows, d = o_ref.shape
    tm = slabs * slab_rows

    @pl.when(g == 0)
    def _():
        o_ref[...] = jnp.zeros_like(o_ref)
        comb = comb_ref[...].reshape(tm, LANES)
        hi = comb.astype(BF16)
        r1 = comb - hi.astype(F32)
        mid = r1.astype(BF16)
        lo = (r1 - mid.astype(F32)).astype(BF16)
        comb3_ref[...] = jnp.concatenate([hi, mid, lo], axis=1)
        grow = comb.T[GROUP_LANE:GROUP_LANE + 1, :]
        gid = lax.broadcasted_iota(jnp.int32, (8, tm), 0).astype(F32)
        member = grow == gid
        rank = jnp.dot(jnp.where(member, 1.0, 0.0).astype(BF16), tri_ref[...], preferred_element_type=F32)
        pos_ref[...] = jnp.where(member, rank, -1.0)

    row = lax.broadcasted_iota(jnp.int32, (8, tm), 0)
    posg = jnp.sum(jnp.where(row == g, pos_ref[...], 0.0), axis=0, keepdims=True)
    count = jnp.max(posg).astype(jnp.int32) + 1

    def chunk(c, rows):
        slot = (lax.broadcasted_iota(jnp.int32, (rows, tm), 0) + c * rows).astype(F32)
        sel = jnp.where(posg == slot, 1.0, 0.0).astype(BF16)
        h = h_ref[...].reshape(tm, d)
        x = jnp.dot(sel, h, preferred_element_type=F32).astype(BF16)
        cw3 = jnp.dot(sel, comb3_ref[...], preferred_element_type=F32)
        cw = cw3[:, 0:LANES] + cw3[:, LANES:2 * LANES] + cw3[:, 2 * LANES:]
        lane = lax.broadcasted_iota(jnp.int32, cw.shape, 1)
        y = jnp.zeros((rows, d), F32)
        for e in range(EXPERTS_PER_GROUP):
            a = jnp.dot(x, wg_ref[e], preferred_element_type=F32)
            u = jnp.dot(x, wu_ref[e], preferred_element_type=F32)
            w = jnp.sum(jnp.where(lane == g * EXPERTS_PER_GROUP + e, cw, 0.0), axis=1, keepdims=True)
            hid = a * _sigmoid(a) * u * w
            y = y + jnp.dot(hid.astype(BF16), wd_ref[e], preferred_element_type=F32)
        back = lax.dot_general(sel, y.astype(BF16), (((0,), (0,)), ((), ())), preferred_element_type=F32)
        o_ref[...] += back.reshape(slabs, slab_rows, d)

    @pl.when((count > 0) & (count <= MOE_SMALL_CHUNK))
    def _():
        chunk(0, MOE_SMALL_CHUNK)

    @pl.when(count > MOE_SMALL_CHUNK)
    def _():
        def body(c, carry):
            chunk(c, MOE_CHUNK)
            return carry
        lax.fori_loop(0, (count + MOE_CHUNK - 1) // MOE_CHUNK, body, 0)

    @pl.when(g == pl.num_programs(1) - 1)
    def _():
        for sl, mod_row in enumerate(mod_rows):
            out = xn_ref[sl] + mod_ref[mod_row, 5:6, :] * o_ref[sl]
            o_ref[sl] = _rms(out, fg_ref[...]) if final else out


def _moe(h2, comb, xn, mod, mod_rows, wg, wu, wd, final_g, tm, final):
    t, d = xn.shape
    slabs = len(mod_rows)
    slab_rows = tm // slabs
    n_tiles = t // tm
    view = lambda a: a.reshape(slabs, n_tiles, slab_rows, a.shape[-1])
    tile = lambda a: pl.BlockSpec((slabs, None, slab_rows, a.shape[-1]), lambda i, g: (0, i, 0, 0))
    ff = wg.shape[2]
    idx = jnp.arange(tm)
    tri = (idx[:, None] < idx[None, :]).astype(BF16)
    grp = lambda i, g: (g, 0, 0)
    once = pl.Buffered(1)
    xn_spec = pl.BlockSpec((slabs, None, slab_rows, d), lambda i, g: (0, i, 0, 0))
    out = pl.pallas_call(
        functools.partial(_moe_kernel, final=final, mod_rows=tuple(mod_rows)),
        grid=(n_tiles, N_GROUPS),
        in_specs=[tile(h2), tile(comb), xn_spec,
                  _const_spec(mod.shape),
                  pl.BlockSpec((tm, tm), lambda i, g: (0, 0), pipeline_mode=once),
                  pl.BlockSpec((EXPERTS_PER_GROUP, d, ff), grp),
                  pl.BlockSpec((EXPERTS_PER_GROUP, d, ff), grp),
                  pl.BlockSpec((EXPERTS_PER_GROUP, ff, d), grp),
                  pl.BlockSpec(final_g.shape, lambda i, g: (0, 0))],
        out_specs=tile(xn),
        out_shape=jax.ShapeDtypeStruct((slabs, n_tiles, slab_rows, d), F32),
        scratch_shapes=[pltpu.VMEM((8, tm), F32), pltpu.VMEM((tm, 3 * LANES), BF16)],
        compiler_params=pltpu.CompilerParams(dimension_semantics=("parallel", "arbitrary"),
                                             vmem_limit_bytes=MOE_VMEM_LIMIT),
        name="moe",
    )(view(h2), view(comb), view(xn), mod, tri, wg, wu, wd, final_g)
    return out.reshape(t, d)


def _rope_tables(s):
    rows = s // GRID_W
    row = jnp.repeat(jnp.arange(rows, dtype=F32), GRID_W)
    col = jnp.tile(jnp.arange(GRID_W, dtype=F32), rows)

    def cos_sin(rot_dim):
        n = rot_dim // 4
        inv = ROPE_BASE ** (-jnp.arange(n, dtype=F32) / n)
        ang = jnp.concatenate([row[:, None] * inv, col[:, None] * inv], axis=-1)
        return jnp.cos(ang), jnp.sin(ang)

    ca, sa = cos_sin(MLA_ROPE)
    one = jnp.ones((s, MLA_NOPE), F32)
    pad1 = jnp.ones((s, LANES - MLA_NOPE - MLA_ROPE), F32)
    cos_a = jnp.concatenate([one, ca, ca, pad1], axis=-1)
    sin_a = jnp.concatenate([0 * one, -sa, sa, 0 * pad1], axis=-1)
    cd, sd = cos_sin(DIFF_HD)
    cos_d = jnp.concatenate([cd, cd, cd, cd], axis=-1)
    sin_d = jnp.concatenate([-sd, sd, -sd, sd], axis=-1)
    return cos_a, sin_a, cos_d, sin_d


def _layout_w_in(w_in):
    d = w_in.shape[0]
    o = 0
    parts = {}
    for name, width in (("cq", Q_LORA), ("ckv", KV_LORA), ("kr", MLA_ROPE), ("glu", 2 * CONV_CH),
                        ("dq", 2 * DIFF_HEADS * DIFF_HD), ("dk", 2 * DIFF_HEADS * DIFF_HD),
                        ("dv", 2 * DIFF_HEADS * DIFF_HD), ("gates", 3 * d)):
        parts[name] = w_in[:, o:o + width]
        o += width
    kr_blk = jnp.concatenate([jnp.zeros((d, MLA_NOPE), w_in.dtype), parts["kr"],
                              jnp.zeros((d, LANES - MLA_NOPE - MLA_ROPE), w_in.dtype)], axis=1)
    win = jnp.concatenate([parts["cq"], parts["ckv"], kr_blk, parts["glu"], parts["dq"], parts["dk"],
                           parts["gates"]], axis=1)
    return win.astype(BF16), parts["dv"].T.astype(BF16)


def _layout_w_uq(w_uq):
    r = w_uq.shape[0]
    w = w_uq.reshape(r, MLA_HEADS, MLA_NOPE + MLA_ROPE)
    w = jnp.pad(w, ((0, 0), (0, 0), (0, LANES - MLA_NOPE - MLA_ROPE)))
    return w.reshape(r, MLA_HEADS * LANES).astype(BF16)


def _layout_w_ukv(w_ukv):
    r = w_ukv.shape[0]
    w = w_ukv.reshape(r, MLA_HEADS, MLA_NOPE + MLA_V)
    wk = jnp.pad(w[:, :, :MLA_NOPE], ((0, 0), (0, 0), (0, LANES - MLA_NOPE))).reshape(r, MLA_HEADS * LANES)
    wvt = w[:, :, MLA_NOPE:].reshape(r, MLA_HEADS * MLA_V).T
    return wk.astype(BF16), wvt.astype(BF16)


def _layout_router(w_rg, b_rg, w_re, b_re):
    d = w_rg.shape[0]
    w = jnp.concatenate([w_re, w_rg, jnp.zeros((d, LANES - N_EXPERTS - N_GROUPS), F32)], axis=1)
    hi = w.astype(BF16)
    lo = (w - hi.astype(F32)).astype(BF16)
    b = jnp.concatenate([b_re, b_rg, jnp.zeros((LANES - N_EXPERTS - N_GROUPS,), F32)])[None, :]
    return jnp.concatenate([hi, lo], axis=1), b


def kernel(x, c, ctx, c_ctx, w_ada, b_ada, norm1_g, w_in, q_norm_g, w_uq, kv_norm_g, w_ukv, conv_w, conv_b,
           conv_ln_g, conv_ln_b, lam_q1, lam_k1, lam_q2, lam_k2, diff_subln_g, w_o_mla, w_o_conv, w_o_diff,
           w_out, norm2_g, w_rg, b_rg, w_re, b_re, w_gate, w_up, w_down, final_g):
    b, s, d = x.shape
    n_ctx = ctx.shape[1]
    depth = w_ada.shape[0]
    tm = 512
    tq = min(4 * Q_SUB, s)
    tq_c = min(tq, n_ctx)
    tk = min(256, s)
    tr = 256
    tmoe_l = min(1024, b * s)
    tmoe_c = min(1024, b * n_ctx)
    ctx_row = b

    rows = -(-(b + 1) // 8) * 8
    cc = jnp.zeros((rows, d), F32).at[:b].set(c).at[b].set(c_ctx)
    mod_all = _ada(cc, w_ada, b_ada).reshape(depth, rows, 6, d)
    tables = _rope_tables(s)
    row2 = lambda a: a[None, :]

    xl = x.reshape(b * s, d)
    xc = ctx.reshape(b * n_ctx, d)
    for l in range(depth):
        update_ctx = l < depth - 1
        final = l == depth - 1
        lam_init = 0.8 - 0.6 * math.exp(-0.3 * l)
        mod = mod_all[l]
        win, wdvt = _layout_w_in(w_in[l])
        wuq = _layout_w_uq(w_uq[l])
        wuk, wuvt = _layout_w_ukv(w_ukv[l])
        n1g, qng, kvg = row2(norm1_g[l]), row2(q_norm_g[l]), row2(kv_norm_g[l])
        lam_p = jnp.stack([lam_q1[l], lam_k1[l], lam_q2[l], lam_k2[l]])
        subln = diff_subln_g[l][:, None]

        pc = _proj(xc, mod, ctx_row, n1g, win, qng, wuq, kvg, wuk, wuvt, wdvt, None, n_ctx, tm)
        pl_ = _proj(xl, mod, None, n1g, win, qng, wuq, kvg, wuk, wuvt, wdvt, tables, s, tm)
        per_batch = lambda arrs, n: [a if i in (2, 6) else a.reshape(b, n, -1) for i, a in enumerate(arrs)]
        qc, kc, vc, uc, dqc, dkc, dvc, gc = per_batch(pc, n_ctx)
        ql, kl, vl, ul, dql, dkl, dvl, gl = per_batch(pl_, s)

        wm, wc, wd = w_o_mla[l].astype(BF16), w_o_conv[l].astype(BF16), w_o_diff[l].astype(BF16)
        wout = w_out[l].astype(BF16)
        wr, br = _layout_router(w_rg[l], b_rg[l], w_re[l], b_re[l])
        n2g = row2(norm2_g[l])
        cw, cb = conv_w[l], row2(conv_b[l])
        cg, cbeta = row2(conv_ln_g[l]), row2(conv_ln_b[l])
        wg, wu, wdn = w_gate[l].astype(BF16), w_up[l].astype(BF16), w_down[l].astype(BF16)
        fg = row2(final_g)

        def tail(xs, om, oc, od, gates, mod_row, seq, tmoe, is_final):
            flat = lambda a: a.reshape(-1, a.shape[-1])
            xn, h2, comb = _merge(xs, flat(om), flat(oc), flat(od), flat(gates), mod, mod_row,
                                  wm, wc, wd, wout, n2g, wr, br, seq, tm)
            moe_rows = tuple(range(b)) if mod_row is None else (mod_row,)
            return _moe(h2, comb, xn, mod, moe_rows, wg, wu, wdn, fg, tmoe, is_final)

        om_l = _mla_attn(ql, kc, vc, kl, vl, tq, tk)
        od_l = _diff_attn(lam_p, subln, dql, dkc, dvc, dkl, dvl, tq, tk, lam_init)
        oc_l = _conv(ul, cw, cb, cg, cbeta, tr)
        xl_new = tail(xl, om_l, oc_l, od_l, gl, None, s, tmoe_l, final)
        if update_ctx:
            om_c = _mla_attn(qc, kc, vc, None, None, tq_c, tk)
            od_c = _diff_attn(lam_p, subln, dqc, dkc, dvc, None, None, tq_c, tk, lam_init)
            oc_c = _conv(uc, cw, cb, cg, cbeta, tr)
            xc = tail(xc, om_c, oc_c, od_c, gc, ctx_row, b * n_ctx, tmoe_c, False)
        xl = xl_new
    return xl.reshape(b, s, d)
```

```python
import functools
import math

import jax
import jax.numpy as jnp
from jax import lax
from jax.experimental import pallas as pl
from jax.experimental.pallas import tpu as pltpu

F32 = jnp.float32
BF16 = jnp.bfloat16

GRID_W = 64
ROPE_BASE = 10000.0
NORM_EPS = 1e-6

MLA_HEADS = 8
MLA_NOPE = 64
MLA_ROPE = 32
MLA_V = 64
Q_LORA = 256
KV_LORA = 128
CONV_CH = 512
CONV_WIDTH = 31
DIFF_HEADS = 4
DIFF_HD = 64
N_GROUPS = 4
EXPERTS_PER_GROUP = 8
N_EXPERTS = N_GROUPS * EXPERTS_PER_GROUP
EXPERT_FF = 256

LANES = 128
SUBLANES = 8
HALO = 16
VMEM_LIMIT = 48 * 1024 * 1024

C_CQ = 0
C_CKV = C_CQ + Q_LORA
C_KR = C_CKV + KV_LORA
C_GA = C_KR + LANES
C_GG = C_GA + CONV_CH
C_DQ = C_GG + CONV_CH
C_DK = C_DQ + 2 * DIFF_HEADS * DIFF_HD
C_GATES = C_DK + 2 * DIFF_HEADS * DIFF_HD

ONES_ROWS = 16
VT_A = MLA_V + ONES_ROWS
VT_D = 2 * DIFF_HD + ONES_ROWS
LOG2E = 1.4426950408889634
MLA_REF_LANE = MLA_NOPE + MLA_ROPE
GROUP_LANE = N_EXPERTS
MOE_CHUNK = 320
MOE_SMALL_CHUNK = 256
MERGE_PART = 256
MOE_VMEM_LIMIT = 56 * 1024 * 1024
Q_SUB = 256
MLA_LOOKAHEAD = 5
DIFF_LOOKAHEAD = 7


def _params(*sem):
    return pltpu.CompilerParams(dimension_semantics=sem, vmem_limit_bytes=VMEM_LIMIT)


def _sigmoid(x):
    return 0.5 * jnp.tanh(0.5 * x) + 0.5


def _rms(x, g):
    return x * lax.rsqrt(jnp.mean(x * x, axis=-1, keepdims=True) + NORM_EPS) * g


def _dot_nt(a, b):
    return lax.dot_general(a, b, (((1,), (1,)), ((), ())), preferred_element_type=F32)


def _const_spec(shape):
    return pl.BlockSpec(shape, lambda *_: (0,) * len(shape))


def _ada_kernel(c_ref, w_ref, b_ref, o_ref):
    c = c_ref[...]
    h = c * _sigmoid(c)
    o_ref[...] = jnp.dot(h, w_ref[...], preferred_element_type=F32,
                         precision=lax.Precision.HIGHEST) + b_ref[...]


def _ada(cc, w_ada, b_ada):
    depth, d, n = w_ada.shape
    bn = 1536
    rows = cc.shape[0]
    return pl.pallas_call(
        _ada_kernel,
        grid=(depth, n // bn),
        in_specs=[pl.BlockSpec((rows, d), lambda l, j: (0, 0)),
                  pl.BlockSpec((None, d, bn), lambda l, j: (l, 0, j)),
                  pl.BlockSpec((None, 1, bn), lambda l, j: (l, 0, j))],
        out_specs=pl.BlockSpec((None, rows, bn), lambda l, j: (l, 0, j)),
        out_shape=jax.ShapeDtypeStruct((depth, rows, n), F32),
        compiler_params=_params("parallel", "parallel"),
        name="adaln",
    )(cc, w_ada, b_ada.reshape(depth, 1, n))


def _swap_halves(t, half):
    n = t.shape[1]
    lane = lax.broadcasted_iota(jnp.int32, t.shape, 1) % (2 * half)
    return jnp.where(lane < half, pltpu.roll(t, n - half, 1), pltpu.roll(t, half, 1))


def _proj_kernel(*refs, rope, scale_a, scale_d):
    if rope:
        (x_ref, mod_ref, n1g_ref, win_ref, qng_ref, wuq_ref, kvg_ref, wuk_ref, wuvt_ref, wdvt_ref,
         ca_ref, sa_ref, cd_ref, sd_ref,
         q_ref, k_ref, vt_ref, u_ref, dq_ref, dk_ref, dvt_ref, g_ref) = refs
    else:
        (x_ref, mod_ref, n1g_ref, win_ref, qng_ref, wuq_ref, kvg_ref, wuk_ref, wuvt_ref, wdvt_ref,
         q_ref, k_ref, vt_ref, u_ref, dq_ref, dk_ref, dvt_ref, g_ref) = refs

    h = _rms(x_ref[...], n1g_ref[...]) * (1.0 + mod_ref[1:2, :]) + mod_ref[0:1, :]
    hb = h.astype(BF16)

    def proj(a, b):
        return jnp.dot(hb, win_ref[:, a:b], preferred_element_type=F32)

    def rope_a(t):
        return t * ca_ref[...] + _swap_halves(t, MLA_ROPE // 2) * sa_ref[...] if rope else t

    def rope_d(t):
        return t * cd_ref[...] + _swap_halves(t, DIFF_HD // 2) * sd_ref[...] if rope else t

    cq = _rms(proj(C_CQ, C_CKV), qng_ref[...]).astype(BF16)
    ckv = _rms(proj(C_CKV, C_KR), kvg_ref[...]).astype(BF16)
    kr = rope_a(proj(C_KR, C_GA))
    lane = lax.broadcasted_iota(jnp.int32, kr.shape, 1)
    kr = jnp.where(lane == MLA_REF_LANE, 1.0, kr)
    q = jnp.dot(cq, wuq_ref[...], preferred_element_type=F32)
    kn = jnp.dot(ckv, wuk_ref[...], preferred_element_type=F32)
    vt = _dot_nt(wuvt_ref[...], ckv)
    ones = jnp.ones((ONES_ROWS, vt.shape[1]), BF16)
    for hd in range(MLA_HEADS):
        blk = slice(hd * LANES, (hd + 1) * LANES)
        q_ref[:, blk] = (rope_a(q[:, blk]) * scale_a).astype(BF16)
        k_ref[:, blk] = (kn[:, blk] + kr).astype(BF16)
        vt_ref[hd * VT_A:hd * VT_A + MLA_V, :] = vt[hd * MLA_V:(hd + 1) * MLA_V, :].astype(BF16)
        vt_ref[hd * VT_A + MLA_V:(hd + 1) * VT_A, :] = ones

    u_ref[...] = proj(C_GA, C_GG) * _sigmoid(proj(C_GG, C_DQ))

    dq = proj(C_DQ, C_DK)
    dk = proj(C_DK, C_GATES)
    dvt = _dot_nt(wdvt_ref[...], hb)
    for hd in range(DIFF_HEADS):
        blk = slice(hd * LANES, (hd + 1) * LANES)
        qh = rope_d(dq[:, blk]) * scale_d
        kh = rope_d(dk[:, blk])
        low = lane < DIFF_HD
        for mp, (qm, km) in enumerate(((qh, kh), (pltpu.roll(qh, DIFF_HD, 1), pltpu.roll(kh, DIFF_HD, 1)))):
            mblk = slice((2 * hd + mp) * LANES, (2 * hd + mp + 1) * LANES)
            dq_ref[:, mblk] = jnp.where(low, qm, 0.0).astype(BF16)
            dk_ref[:, mblk] = jnp.where(low, km, jnp.where(lane == DIFF_HD, 1.0, 0.0)).astype(BF16)
        dvt_ref[hd * VT_D:hd * VT_D + LANES, :] = dvt[blk, :].astype(BF16)
        dvt_ref[hd * VT_D + LANES:(hd + 1) * VT_D, :] = ones

    d = x_ref.shape[1]
    for j in range(3):
        g_ref[:, j * d:(j + 1) * d] = _sigmoid(proj(C_GATES + j * d, C_GATES + (j + 1) * d)).astype(BF16)


def _proj(x, mod, mod_row, n1g, win, qng, wuq, kvg, wuk, wuvt, wdvt, tables, seq, tm):
    t, d = x.shape
    rope = tables is not None
    tiles_per_seq = seq // tm
    row = lambda i: (i, 0)
    col = lambda i: (0, i)
    if mod_row is None:
        mod_map = lambda i: (i // tiles_per_seq, 0, 0)
    else:
        mod_map = lambda i: (mod_row, 0, 0)
    consts = (n1g, win, qng, wuq, kvg, wuk, wuvt, wdvt)
    in_specs = [pl.BlockSpec((tm, d), row), pl.BlockSpec((None, 6, d), mod_map)]
    in_specs += [_const_spec(a.shape) for a in consts]
    args = [x, mod, *consts]
    if rope:
        in_specs += [pl.BlockSpec((tm, LANES), lambda i: (i % tiles_per_seq, 0))] * 4
        args += list(tables)
    outs = [((t, MLA_HEADS * LANES), (tm, MLA_HEADS * LANES), row, BF16),
            ((t, MLA_HEADS * LANES), (tm, MLA_HEADS * LANES), row, BF16),
            ((MLA_HEADS * VT_A, t), (MLA_HEADS * VT_A, tm), col, BF16),
            ((t, CONV_CH), (tm, CONV_CH), row, F32),
            ((t, 2 * DIFF_HEADS * LANES), (tm, 2 * DIFF_HEADS * LANES), row, BF16),
            ((t, 2 * DIFF_HEADS * LANES), (tm, 2 * DIFF_HEADS * LANES), row, BF16),
            ((DIFF_HEADS * VT_D, t), (DIFF_HEADS * VT_D, tm), col, BF16),
            ((t, 3 * d), (tm, 3 * d), row, BF16)]
    kern = functools.partial(_proj_kernel, rope=rope,
                             scale_a=LOG2E / math.sqrt(MLA_NOPE + MLA_ROPE), scale_d=LOG2E / math.sqrt(DIFF_HD))
    return pl.pallas_call(
        kern,
        grid=(t // tm,),
        in_specs=in_specs,
        out_specs=[pl.BlockSpec(blk, imap) for _, blk, imap, _ in outs],
        out_shape=[jax.ShapeDtypeStruct(shape, dt) for shape, _, _, dt in outs],
        compiler_params=_params("parallel"),
        name="proj_rope" if rope else "proj_ctx",
    )(*args)


def _flash_streams(streams, lookahead, ref_row=None):
    n_chunks = len(streams[0][1])
    items = [(sid, c) for c in range(n_chunks) for sid in range(len(streams))]
    m = [None] * len(streams)
    acc = [None] * len(streams)
    ref = [None] * len(streams)
    if ref_row is not None and n_chunks > 1:
        assert len(streams) > 1
        lookahead = min(lookahead, len(streams) - 1)

    def finish(sid, st, vt):
        cmax = jnp.max(st, axis=0, keepdims=True)
        m_new = cmax if m[sid] is None else jnp.maximum(m[sid], cmax)
        pt = jnp.exp2((st - m_new).astype(BF16))
        pv = jnp.dot(vt, pt, preferred_element_type=F32)
        acc[sid] = pv if m[sid] is None else acc[sid] * jnp.exp2(m[sid] - m_new) + pv
        m[sid] = m_new

    def finish_folded(sid, dt, vt):
        db = dt.astype(BF16)
        cmax = jnp.max(db, axis=0, keepdims=True)
        first = m[sid] is None
        delta = cmax if first else jnp.maximum(cmax, jnp.zeros_like(cmax))
        pt = jnp.exp2(db - delta)
        pv = jnp.dot(vt, pt, preferred_element_type=F32)
        m_new = delta.astype(F32) if first else ref[sid] + delta.astype(F32)
        acc[sid] = pv if first else acc[sid] * jnp.exp2(m[sid] - m_new) + pv
        m[sid] = m_new
        ref[sid] = m_new.astype(BF16).astype(F32)

    qts = [q.astype(F32).T.astype(BF16) for q, _ in streams]
    pending = []
    for sid, c in items:
        k, vt = streams[sid][1][c]
        qt = qts[sid]
        if ref_row is not None and ref[sid] is not None:
            row = lax.broadcasted_iota(jnp.int32, qt.shape, 0)
            qt = jnp.where(row == ref_row, jnp.broadcast_to((-ref[sid]).astype(BF16), qt.shape), qt)
        st = jnp.dot(k, qt, preferred_element_type=F32)
        if len(pending) == lookahead:
            (finish if ref_row is None else finish_folded)(*pending.pop(0))
        pending.append((sid, st, vt))
    for item in pending:
        (finish if ref_row is None else finish_folded)(*item)
    return acc


def _key_chunks(kc_ref, vtc_ref, kl_ref, vtl_ref, tk, kcols, vrows):
    chunks = [(kc_ref[:, kcols], vtc_ref[vrows, :])]
    if kl_ref is not None:
        for j in range(kl_ref.shape[0] // tk):
            keys = slice(j * tk, (j + 1) * tk)
            chunks.append((kl_ref[keys, kcols], vtl_ref[vrows, keys]))
    return chunks


def _mla_attn_kernel(*refs, latent, tk):
    if latent:
        q_ref, kc_ref, vtc_ref, kl_ref, vtl_ref, o_ref = refs
    else:
        q_ref, kc_ref, vtc_ref, o_ref = refs
        kl_ref = vtl_ref = None
    subs = [slice(r, r + Q_SUB) for r in range(0, q_ref.shape[0], Q_SUB)]
    streams = []
    for qs in subs:
        for hh in range(2):
            cols = slice(hh * LANES, (hh + 1) * LANES)
            rows = slice(hh * VT_A, (hh + 1) * VT_A)
            streams.append((q_ref[qs, cols], _key_chunks(kc_ref, vtc_ref, kl_ref, vtl_ref, tk, cols, rows)))
    outs = [acc[0:MLA_V, :] * (1.0 / acc[MLA_V:MLA_V + 1, :]) for acc in _flash_streams(streams, MLA_LOOKAHEAD, ref_row=MLA_REF_LANE)]
    for i, qs in enumerate(subs):
        o_ref[qs, :] = jnp.concatenate(outs[2 * i:2 * i + 2], axis=0).T.astype(BF16)


def _mla_attn(q, kc, vtc, kl, vtl, tq, tk):
    b, lq, _ = q.shape
    c = kc.shape[1]
    latent = kl is not None
    in_specs = [pl.BlockSpec((None, tq, 2 * LANES), lambda i, p, j: (i, j, p)),
                pl.BlockSpec((None, c, 2 * LANES), lambda i, p, j: (i, 0, p)),
                pl.BlockSpec((2 * VT_A, c), lambda i, p, j: (p, i))]
    args = [q, kc, vtc]
    if latent:
        s = kl.shape[1]
        in_specs += [pl.BlockSpec((None, s, 2 * LANES), lambda i, p, j: (i, 0, p)),
                     pl.BlockSpec((2 * VT_A, s), lambda i, p, j: (p, i))]
        args += [kl, vtl]
    return pl.pallas_call(
        functools.partial(_mla_attn_kernel, latent=latent, tk=tk),
        grid=(b, MLA_HEADS // 2, lq // tq),
        in_specs=in_specs,
        out_specs=pl.BlockSpec((None, tq, LANES), lambda i, p, j: (i, j, p)),
        out_shape=jax.ShapeDtypeStruct((b, lq, MLA_HEADS * MLA_V), BF16),
        compiler_params=_params("parallel", "parallel", "parallel"),
        name="mla_attn_lat" if latent else "mla_attn_ctx",
    )(*args)


def _diff_attn_kernel(*refs, latent, tk, lam_init):
    if latent:
        lam_ref, g_ref, q_ref, kc_ref, vtc_ref, kl_ref, vtl_ref, o_ref = refs
    else:
        lam_ref, g_ref, q_ref, kc_ref, vtc_ref, o_ref = refs
        kl_ref = vtl_ref = None
    subs = [slice(r, r + Q_SUB) for r in range(0, q_ref.shape[0], Q_SUB)]
    streams = []
    for qs in subs:
        for mp in range(2):
            cols = slice(mp * LANES, (mp + 1) * LANES)
            chunks = _key_chunks(kc_ref, vtc_ref, kl_ref, vtl_ref, tk, cols, slice(None))
            streams.append((q_ref[qs, cols], chunks))
    accs = _flash_streams(streams, DIFF_LOOKAHEAD, ref_row=DIFF_HD)
    outs = [acc[0:LANES, :] * (1.0 / acc[LANES:LANES + 1, :]) for acc in accs]
    lam = (jnp.exp(jnp.sum(lam_ref[0:1, :] * lam_ref[1:2, :], axis=-1, keepdims=True))
           - jnp.exp(jnp.sum(lam_ref[2:3, :] * lam_ref[3:4, :], axis=-1, keepdims=True)) + lam_init)
    for i, qs in enumerate(subs):
        o = outs[2 * i] - lam * outs[2 * i + 1]
        o = o * lax.rsqrt(jnp.mean(o * o, axis=0, keepdims=True) + NORM_EPS) * (g_ref[...] * (1.0 - lam_init))
        o_ref[qs, :] = o.T.astype(BF16)


def _diff_attn(lam_p, subln_g, q, kc, vtc, kl, vtl, tq, tk, lam_init):
    b, lq, _ = q.shape
    c = kc.shape[1]
    latent = kl is not None
    kblk = lambda rows: pl.BlockSpec((None, rows, 2 * LANES), lambda i, h, j: (i, 0, h))
    vblk = lambda keys: pl.BlockSpec((VT_D, keys), lambda i, h, j: (h, i))
    in_specs = [_const_spec(lam_p.shape), _const_spec(subln_g.shape),
                pl.BlockSpec((None, tq, 2 * LANES), lambda i, h, j: (i, j, h)), kblk(c), vblk(c)]
    args = [lam_p, subln_g, q, kc, vtc]
    if latent:
        in_specs += [kblk(kl.shape[1]), vblk(kl.shape[1])]
        args += [kl, vtl]
    return pl.pallas_call(
        functools.partial(_diff_attn_kernel, latent=latent, tk=tk, lam_init=lam_init),
        grid=(b, DIFF_HEADS, lq // tq),
        in_specs=in_specs,
        out_specs=pl.BlockSpec((None, tq, LANES), lambda i, h, j: (i, j, h)),
        out_shape=jax.ShapeDtypeStruct((b, lq, DIFF_HEADS * LANES), BF16),
        compiler_params=_params("parallel", "parallel", "parallel"),
        name="diff_attn_lat" if latent else "diff_attn_ctx",
    )(*args)


def _conv_fill(first, last, prev_ref, cur_ref, next_ref, pad_ref, shift_ref):
    tr = cur_ref.shape[0]
    zero = jnp.zeros((HALO, cur_ref.shape[1]), F32)
    pad_ref[0:HALO, :] = jnp.where(first, zero, prev_ref[...])
    pad_ref[HALO:HALO + tr, :] = cur_ref[...]
    pad_ref[HALO + tr:, :] = jnp.where(last, zero, next_ref[...])
    span = shift_ref.shape[1]
    for phase in range(1, SUBLANES):
        shift_ref[phase] = pad_ref[phase:phase + span, :]


def _conv_rows(pad_ref, shift_ref, w_ref, b_ref, g_ref, beta_ref, r0, n):
    off = HALO - CONV_WIDTH // 2
    acc = jnp.zeros((n, pad_ref.shape[1]), F32) + b_ref[...]
    for tap in range(CONV_WIDTH):
        phase = (off + tap) % SUBLANES
        base = r0 + off + tap - phase
        win = pad_ref[base:base + n, :] if phase == 0 else shift_ref[phase, base:base + n, :]
        acc = acc + win * w_ref[tap:tap + 1, :]
    mu = jnp.mean(acc, axis=-1, keepdims=True)
    cen = acc - mu
    y = cen * lax.rsqrt(jnp.mean(cen * cen, axis=-1, keepdims=True) + NORM_EPS) * g_ref[...] + beta_ref[...]
    return (y * _sigmoid(y)).astype(BF16)


def _conv_kernel(prev_ref, cur_ref, next_ref, w_ref, b_ref, g_ref, beta_ref, o_ref, pad_ref, shift_ref):
    j = pl.program_id(1)
    _conv_fill(j == 0, j == pl.num_programs(1) - 1, prev_ref, cur_ref, next_ref, pad_ref, shift_ref)
    o_ref[...] = _conv_rows(pad_ref, shift_ref, w_ref, b_ref, g_ref, beta_ref, 0, cur_ref.shape[0])


def _conv(u, w, bias, g, beta, tr):
    b, l, ch = u.shape
    per = tr // HALO
    last = l // HALO - 1
    return pl.pallas_call(
        _conv_kernel,
        grid=(b, l // tr),
        in_specs=[pl.BlockSpec((None, HALO, ch), lambda i, j: (i, jnp.maximum(j * per - 1, 0), 0)),
                  pl.BlockSpec((None, tr, ch), lambda i, j: (i, j, 0)),
                  pl.BlockSpec((None, HALO, ch), lambda i, j: (i, jnp.minimum((j + 1) * per, last), 0)),
                  _const_spec(w.shape), _const_spec(bias.shape), _const_spec(g.shape), _const_spec(beta.shape)],
        out_specs=pl.BlockSpec((None, tr, ch), lambda i, j: (i, j, 0)),
        out_shape=jax.ShapeDtypeStruct((b, l, ch), BF16),
        scratch_shapes=[pltpu.VMEM((tr + 2 * HALO, ch), F32),
                        pltpu.VMEM((SUBLANES, tr + 2 * HALO - SUBLANES, ch), F32)],
        compiler_params=_params("parallel", "parallel"),
        name="conv_ln_silu",
    )(u, u, u, w, bias, g, beta)


def _route(logits):
    lane = lax.broadcasted_iota(jnp.int32, logits.shape, 1)
    neg = jnp.float32(-jnp.inf)
    big = jnp.int32(LANES)
    is_grp = (lane >= N_EXPERTS) & (lane < N_EXPERTS + N_GROUPS)
    gl = jnp.where(is_grp, logits, neg)
    gmax = jnp.max(gl, axis=-1, keepdims=True)
    gsel = jnp.min(jnp.where(gl == gmax, lane, big), axis=-1, keepdims=True) - N_EXPERTS
    p_g = 1.0 / jnp.sum(jnp.exp(gl - gmax), axis=-1, keepdims=True)
    in_grp = (lane >= gsel * EXPERTS_PER_GROUP) & (lane < (gsel + 1) * EXPERTS_PER_GROUP)
    e1 = jnp.where(in_grp, logits, neg)
    v1 = jnp.max(e1, axis=-1, keepdims=True)
    i1 = jnp.min(jnp.where(e1 == v1, lane, big), axis=-1, keepdims=True)
    e2 = jnp.where(lane == i1, neg, e1)
    v2 = jnp.max(e2, axis=-1, keepdims=True)
    i2 = jnp.min(jnp.where(e2 == v2, lane, big), axis=-1, keepdims=True)
    r = jnp.exp(v2 - v1)
    w1 = p_g / (1.0 + r)
    comb = jnp.where(lane == i1, w1, jnp.where(lane == i2, w1 * r, 0.0))
    return jnp.where(lane == GROUP_LANE, gsel.astype(F32), comb)


def _merge_kernel(*refs, tiles_per_seq):
    if tiles_per_seq is None:
        (x_ref, om_ref, oc_ref, od_ref, gates_ref, mod_ref, wm_ref, wc_ref, wd_ref, wout_ref,
         n2g_ref, wr_ref, br_ref, xn_ref, h2_ref, comb_ref) = refs
        conv_branch = lambda rows: oc_ref[rows, :]
    else:
        (x_ref, om_ref, up_ref, uc_ref, un_ref, cw_ref, cb_ref, cg_ref, cbeta_ref, od_ref, gates_ref, mod_ref,
         wm_ref, wc_ref, wd_ref, wout_ref, n2g_ref, wr_ref, br_ref, xn_ref, h2_ref, comb_ref,
         pad_ref, shift_ref) = refs
        ts = pl.program_id(0) % tiles_per_seq
        _conv_fill(ts == 0, ts == tiles_per_seq - 1, up_ref, uc_ref, un_ref, pad_ref, shift_ref)
        conv_branch = lambda rows: _conv_rows(pad_ref, shift_ref, cw_ref, cb_ref, cg_ref, cbeta_ref,
                                              rows.start, rows.stop - rows.start)
    tm, d = x_ref.shape
    parts = [slice(r, r + MERGE_PART) for r in range(0, tm, MERGE_PART)]

    def branch_sum(rows):
        gate = lambda j: gates_ref[rows, j * d:(j + 1) * d].astype(F32)
        return (gate(0) * jnp.dot(om_ref[rows, :], wm_ref[...], preferred_element_type=F32)
                + gate(1) * jnp.dot(conv_branch(rows), wc_ref[...], preferred_element_type=F32)
                + gate(2) * jnp.dot(od_ref[rows, :], wd_ref[...], preferred_element_type=F32))

    def residual_norm(rows, y):
        z = jnp.dot(y.astype(BF16), wout_ref[...], preferred_element_type=F32)
        xn = x_ref[rows, :] + mod_ref[2:3, :] * z
        xn_ref[rows, :] = xn
        h2 = _rms(xn, n2g_ref[...]) * (1.0 + mod_ref[4:5, :]) + mod_ref[3:4, :]
        h_hi = h2.astype(BF16)
        h2_ref[rows, :] = h_hi
        return h_hi, (h2 - h_hi.astype(F32)).astype(BF16)

    def router(rows, h_hi, h_lo):
        hw = jnp.dot(h_hi, wr_ref[...], preferred_element_type=F32)
        lw = jnp.dot(h_lo, wr_ref[:, 0:LANES], preferred_element_type=F32)
        comb_ref[rows, :] = _route(hw[:, 0:LANES] + hw[:, LANES:] + lw + br_ref[...])

    ys = [branch_sum(rows) for rows in parts]
    hs = [residual_norm(rows, y) for rows, y in zip(parts, ys)]
    for rows, (h_hi, h_lo) in zip(parts, hs):
        router(rows, h_hi, h_lo)


def _merge(x, om, oc, od, gates, mod, mod_row, wm, wc, wd, wout, n2g, wr, br, seq, tm, conv=None):
    t, d = x.shape
    tiles_per_seq = seq // tm
    row = lambda i: (i, 0)
    if mod_row is None:
        mod_map = lambda i: (i // tiles_per_seq, 0, 0)
    else:
        mod_map = lambda i: (mod_row, 0, 0)
    rows = lambda a: pl.BlockSpec((tm, a.shape[1]), row)
    scratch = []
    if conv is None:
        conv_specs, conv_args = [rows(oc)], [oc]
    else:
        ch = oc.shape[1]
        per = tm // HALO
        last = t // HALO - 1
        conv_specs = [pl.BlockSpec((HALO, ch), lambda i: (jnp.maximum(i * per - 1, 0), 0)), rows(oc),
                      pl.BlockSpec((HALO, ch), lambda i: (jnp.minimum((i + 1) * per, last), 0))]
        conv_specs += [_const_spec(a.shape) for a in conv]
        conv_args = [oc, oc, oc, *conv]
        scratch = [pltpu.VMEM((tm + 2 * HALO, ch), F32),
                   pltpu.VMEM((SUBLANES, tm + 2 * HALO - SUBLANES, ch), F32)]
    return pl.pallas_call(
        functools.partial(_merge_kernel, tiles_per_seq=None if conv is None else tiles_per_seq),
        grid=(t // tm,),
        in_specs=[rows(x), rows(om), *conv_specs, rows(od), rows(gates),
                  pl.BlockSpec((None, 6, d), mod_map)]
                 + [_const_spec(a.shape) for a in (wm, wc, wd, wout, n2g, wr, br)],
        out_specs=[pl.BlockSpec((tm, d), row), pl.BlockSpec((tm, d), row), pl.BlockSpec((tm, LANES), row)],
        out_shape=[jax.ShapeDtypeStruct((t, d), F32), jax.ShapeDtypeStruct((t, d), BF16),
                   jax.ShapeDtypeStruct((t, LANES), F32)],
        scratch_shapes=scratch,
        compiler_params=_params("parallel"),
        name="merge_route",
    )(x, om, *conv_args, od, gates, mod, wm, wc, wd, wout, n2g, wr, br)


def _moe_kernel(h_ref, comb_ref, xn_ref, mod_ref, tri_ref, wg_ref, wu_ref, wd_ref, fg_ref, o_ref,
                pos_ref, comb3_ref, *, final, mod_rows):
    g = pl.program_id(1)
    slabs, slab_rows, d = o_ref.shape
    tm = slabs * slab_rows

    @pl.when(g == 0)
    def _():
        o_ref[...] = jnp.zeros_like(o_ref)
        comb = comb_ref[...].reshape(tm, LANES)
        hi = comb.astype(BF16)
        r1 = comb - hi.astype(F32)
        mid = r1.astype(BF16)
        lo = (r1 - mid.astype(F32)).astype(BF16)
        comb3_ref[...] = jnp.concatenate([hi, mid, lo], axis=1)
        grow = comb.T[GROUP_LANE:GROUP_LANE + 1, :]
        gid = lax.broadcasted_iota(jnp.int32, (8, tm), 0).astype(F32)
        member = grow == gid
        rank = jnp.dot(jnp.where(member, 1.0, 0.0).astype(BF16), tri_ref[...], preferred_element_type=F32)
        pos_ref[...] = jnp.where(member, rank, -1.0)

    row = lax.broadcasted_iota(jnp.int32, (8, tm), 0)
    posg = jnp.sum(jnp.where(row == g, pos_ref[...], 0.0), axis=0, keepdims=True)
    count = jnp.max(posg).astype(jnp.int32) + 1

    def chunk(c, rows):
        slot = (lax.broadcasted_iota(jnp.int32, (rows, tm), 0) + c * rows).astype(F32)
        sel = jnp.where(posg == slot, 1.0, 0.0).astype(BF16)
        h = h_ref[...].reshape(tm, d)
        x = jnp.dot(sel, h, preferred_element_type=F32).astype(BF16)
        cw3 = jnp.dot(sel, comb3_ref[...], preferred_element_type=F32)
        cw = cw3[:, 0:LANES] + cw3[:, LANES:2 * LANES] + cw3[:, 2 * LANES:]
        lane = lax.broadcasted_iota(jnp.int32, cw.shape, 1)
        y = jnp.zeros((rows, d), F32)
        for e in range(EXPERTS_PER_GROUP):
            a = jnp.dot(x, wg_ref[e], preferred_element_type=F32)
            u = jnp.dot(x, wu_ref[e], preferred_element_type=F32)
            w = jnp.sum(jnp.where(lane == g * EXPERTS_PER_GROUP + e, cw, 0.0), axis=1, keepdims=True)
            hid = a * _sigmoid(a) * u * w
            y = y + jnp.dot(hid.astype(BF16), wd_ref[e], preferred_element_type=F32)
        back = lax.dot_general(sel, y.astype(BF16), (((0,), (0,)), ((), ())), preferred_element_type=F32)
        o_ref[...] += back.reshape(slabs, slab_rows, d)

    @pl.when((count > 0) & (count <= MOE_SMALL_CHUNK))
    def _():
        chunk(0, MOE_SMALL_CHUNK)

    @pl.when(count > MOE_SMALL_CHUNK)
    def _():
        def body(c, carry):
            chunk(c, MOE_CHUNK)
            return carry
        lax.fori_loop(0, (count + MOE_CHUNK - 1) // MOE_CHUNK, body, 0)

    @pl.when(g == pl.num_programs(1) - 1)
    def _():
        for sl, mod_row in enumerate(mod_rows):
            out = xn_ref[sl] + mod_ref[mod_row, 5:6, :] * o_ref[sl]
            o_ref[sl] = _rms(out, fg_ref[...]) if final else out


def _moe(h2, comb, xn, mod, mod_rows, wg, wu, wd, final_g, tm, final):
    t, d = xn.shape
    slabs = len(mod_rows)
    slab_rows = tm // slabs
    n_tiles = t // tm
    view = lambda a: a.reshape(slabs, n_tiles, slab_rows, a.shape[-1])
    tile = lambda a: pl.BlockSpec((slabs, None, slab_rows, a.shape[-1]), lambda i, g: (0, i, 0, 0))
    ff = wg.shape[2]
    idx = jnp.arange(tm)
    tri = (idx[:, None] < idx[None, :]).astype(BF16)
    grp = lambda i, g: (g, 0, 0)
    once = pl.Buffered(1)
    xn_spec = pl.BlockSpec((slabs, None, slab_rows, d), lambda i, g: (0, i, 0, 0))
    out = pl.pallas_call(
        functools.partial(_moe_kernel, final=final, mod_rows=tuple(mod_rows)),
        grid=(n_tiles, N_GROUPS),
        in_specs=[tile(h2), tile(comb), xn_spec,
                  _const_spec(mod.shape),
                  pl.BlockSpec((tm, tm), lambda i, g: (0, 0), pipeline_mode=once),
                  pl.BlockSpec((EXPERTS_PER_GROUP, d, ff), grp),
                  pl.BlockSpec((EXPERTS_PER_GROUP, d, ff), grp),
                  pl.BlockSpec((EXPERTS_PER_GROUP, ff, d), grp),
                  pl.BlockSpec(final_g.shape, lambda i, g: (0, 0))],
        out_specs=tile(xn),
        out_shape=jax.ShapeDtypeStruct((slabs, n_tiles, slab_rows, d), F32),
        scratch_shapes=[pltpu.VMEM((8, tm), F32), pltpu.VMEM((tm, 3 * LANES), BF16)],
        compiler_params=pltpu.CompilerParams(dimension_semantics=("parallel", "arbitrary"),
                                             vmem_limit_bytes=MOE_VMEM_LIMIT),
        name="moe",
    )(view(h2), view(comb), view(xn), mod, tri, wg, wu, wd, final_g)
    return out.reshape(t, d)


def _rope_tables(s):
    rows = s // GRID_W
    row = jnp.repeat(jnp.arange(rows, dtype=F32), GRID_W)
    col = jnp.tile(jnp.arange(GRID_W, dtype=F32), rows)

    def cos_sin(rot_dim):
        n = rot_dim // 4
        inv = ROPE_BASE ** (-jnp.arange(n, dtype=F32) / n)
        ang = jnp.concatenate([row[:, None] * inv, col[:, None] * inv], axis=-1)
        return jnp.cos(ang), jnp.sin(ang)

    ca, sa = cos_sin(MLA_ROPE)
    one = jnp.ones((s, MLA_NOPE), F32)
    pad1 = jnp.ones((s, LANES - MLA_NOPE - MLA_ROPE), F32)
    cos_a = jnp.concatenate([one, ca, ca, pad1], axis=-1)
    sin_a = jnp.concatenate([0 * one, -sa, sa, 0 * pad1], axis=-1)
    cd, sd = cos_sin(DIFF_HD)
    cos_d = jnp.concatenate([cd, cd, cd, cd], axis=-1)
    sin_d = jnp.concatenate([-sd, sd, -sd, sd], axis=-1)
    return cos_a, sin_a, cos_d, sin_d


def _layout_w_in(w_in):
    d = w_in.shape[0]
    o = 0
    parts = {}
    for name, width in (("cq", Q_LORA), ("ckv", KV_LORA), ("kr", MLA_ROPE), ("glu", 2 * CONV_CH),
                        ("dq", 2 * DIFF_HEADS * DIFF_HD), ("dk", 2 * DIFF_HEADS * DIFF_HD),
                        ("dv", 2 * DIFF_HEADS * DIFF_HD), ("gates", 3 * d)):
        parts[name] = w_in[:, o:o + width]
        o += width
    kr_blk = jnp.concatenate([jnp.zeros((d, MLA_NOPE), w_in.dtype), parts["kr"],
                              jnp.zeros((d, LANES - MLA_NOPE - MLA_ROPE), w_in.dtype)], axis=1)
    win = jnp.concatenate([parts["cq"], parts["ckv"], kr_blk, parts["glu"], parts["dq"], parts["dk"],
                           parts["gates"]], axis=1)
    return win.astype(BF16), parts["dv"].T.astype(BF16)


def _layout_w_uq(w_uq):
    r = w_uq.shape[0]
    w = w_uq.reshape(r, MLA_HEADS, MLA_NOPE + MLA_ROPE)
    w = jnp.pad(w, ((0, 0), (0, 0), (0, LANES - MLA_NOPE - MLA_ROPE)))
    return w.reshape(r, MLA_HEADS * LANES).astype(BF16)


def _layout_w_ukv(w_ukv):
    r = w_ukv.shape[0]
    w = w_ukv.reshape(r, MLA_HEADS, MLA_NOPE + MLA_V)
    wk = jnp.pad(w[:, :, :MLA_NOPE], ((0, 0), (0, 0), (0, LANES - MLA_NOPE))).reshape(r, MLA_HEADS * LANES)
    wvt = w[:, :, MLA_NOPE:].reshape(r, MLA_HEADS * MLA_V).T
    return wk.astype(BF16), wvt.astype(BF16)


def _layout_router(w_rg, b_rg, w_re, b_re):
    d = w_rg.shape[0]
    w = jnp.concatenate([w_re, w_rg, jnp.zeros((d, LANES - N_EXPERTS - N_GROUPS), F32)], axis=1)
    hi = w.astype(BF16)
    lo = (w - hi.astype(F32)).astype(BF16)
    b = jnp.concatenate([b_re, b_rg, jnp.zeros((LANES - N_EXPERTS - N_GROUPS,), F32)])[None, :]
    return jnp.concatenate([hi, lo], axis=1), b


def kernel(x, c, ctx, c_ctx, w_ada, b_ada, norm1_g, w_in, q_norm_g, w_uq, kv_norm_g, w_ukv, conv_w, conv_b,
           conv_ln_g, conv_ln_b, lam_q1, lam_k1, lam_q2, lam_k2, diff_subln_g, w_o_mla, w_o_conv, w_o_diff,
           w_out, norm2_g, w_rg, b_rg, w_re, b_re, w_gate, w_up, w_down, final_g):
    b, s, d = x.shape
    n_ctx = ctx.shape[1]
    depth = w_ada.shape[0]
    tm = 512
    tq = min(4 * Q_SUB, s)
    tq_c = min(tq, n_ctx)
    tk = min(256, s)
    tr = 256
    tmoe_l = min(1024, b * s)
    tmoe_c = min(1024, b * n_ctx)
    ctx_row = b

    rows = -(-(b + 1) // 8) * 8
    cc = jnp.zeros((rows, d), F32).at[:b].set(c).at[b].set(c_ctx)
    mod_all = _ada(cc, w_ada, b_ada).reshape(depth, rows, 6, d)
    tables = _rope_tables(s)
    row2 = lambda a: a[None, :]

    xl = x.reshape(b * s, d)
    xc = ctx.reshape(b * n_ctx, d)
    for l in range(depth):
        update_ctx = l < depth - 1
        final = l == depth - 1
        lam_init = 0.8 - 0.6 * math.exp(-0.3 * l)
        mod = mod_all[l]
        win, wdvt = _layout_w_in(w_in[l])
        wuq = _layout_w_uq(w_uq[l])
        wuk, wuvt = _layout_w_ukv(w_ukv[l])
        n1g, qng, kvg = row2(norm1_g[l]), row2(q_norm_g[l]), row2(kv_norm_g[l])
        lam_p = jnp.stack([lam_q1[l], lam_k1[l], lam_q2[l], lam_k2[l]])
        subln = diff_subln_g[l][:, None]

        pc = _proj(xc, mod, ctx_row, n1g, win, qng, wuq, kvg, wuk, wuvt, wdvt, None, n_ctx, tm)
        pl_ = _proj(xl, mod, None, n1g, win, qng, wuq, kvg, wuk, wuvt, wdvt, tables, s, tm)
        per_batch = lambda arrs, n: [a if i in (2, 6) else a.reshape(b, n, -1) for i, a in enumerate(arrs)]
        qc, kc, vc, uc, dqc, dkc, dvc, gc = per_batch(pc, n_ctx)
        ql, kl, vl, ul, dql, dkl, dvl, gl = per_batch(pl_, s)

        wm, wc, wd = w_o_mla[l].astype(BF16), w_o_conv[l].astype(BF16), w_o_diff[l].astype(BF16)
        wout = w_out[l].astype(BF16)
        wr, br = _layout_router(w_rg[l], b_rg[l], w_re[l], b_re[l])
        n2g = row2(norm2_g[l])
        cw, cb = conv_w[l], row2(conv_b[l])
        cg, cbeta = row2(conv_ln_g[l]), row2(conv_ln_b[l])
        wg, wu, wdn = w_gate[l].astype(BF16), w_up[l].astype(BF16), w_down[l].astype(BF16)
        fg = row2(final_g)

        def tail(xs, om, oc, od, gates, mod_row, seq, tmoe, is_final, conv=None):
            flat = lambda a: a.reshape(-1, a.shape[-1])
            xn, h2, comb = _merge(xs, flat(om), flat(oc), flat(od), flat(gates), mod, mod_row,
                                  wm, wc, wd, wout, n2g, wr, br, seq, tm, conv=conv)
            moe_rows = tuple(range(b)) if mod_row is None else (mod_row,)
            return _moe(h2, comb, xn, mod, moe_rows, wg, wu, wdn, fg, tmoe, is_final)

        om_l = _mla_attn(ql, kc, vc, kl, vl, tq, tk)
        od_l = _diff_attn(lam_p, subln, dql, dkc, dvc, dkl, dvl, tq, tk, lam_init)
        xl_new = tail(xl, om_l, ul, od_l, gl, None, s, tmoe_l, final, conv=(cw, cb, cg, cbeta))
        if update_ctx:
            om_c = _mla_attn(qc, kc, vc, None, None, tq_c, tk)
            od_c = _diff_attn(lam_p, subln, dqc, dkc, dvc, None, None, tq_c, tk, lam_init)
            oc_c = _conv(uc, cw, cb, cg, cbeta, tr)
            xc = tail(xc, om_c, oc_c, od_c, gc, ctx_row, b * n_ctx, tmoe_c, False)
        xl = xl_new
    return xl.reshape(b, s, d)
```

```python
import functools
import math

import jax
import jax.numpy as jnp
from jax import lax
from jax.experimental import pallas as pl
from jax.experimental.pallas import tpu as pltpu

F32 = jnp.float32
BF16 = jnp.bfloat16

GRID_W = 64
ROPE_BASE = 10000.0
NORM_EPS = 1e-6

MLA_HEADS = 8
MLA_NOPE = 64
MLA_ROPE = 32
MLA_V = 64
Q_LORA = 256
KV_LORA = 128
CONV_CH = 512
CONV_WIDTH = 31
DIFF_HEADS = 4
DIFF_HD = 64
N_GROUPS = 4
EXPERTS_PER_GROUP = 8
N_EXPERTS = N_GROUPS * EXPERTS_PER_GROUP
EXPERT_FF = 256

LANES = 128
SUBLANES = 8
HALO = 16
VMEM_LIMIT = 48 * 1024 * 1024

C_CQ = 0
C_CKV = C_CQ + Q_LORA
C_KR = C_CKV + KV_LORA
C_GA = C_KR + LANES
C_GG = C_GA + CONV_CH
C_DQ = C_GG + CONV_CH
C_DK = C_DQ + 2 * DIFF_HEADS * DIFF_HD
C_GATES = C_DK + 2 * DIFF_HEADS * DIFF_HD

ONES_ROWS = 16
VT_A = MLA_V + ONES_ROWS
VT_D = 2 * DIFF_HD + ONES_ROWS
LOG2E = 1.4426950408889634
MLA_REF_LANE = MLA_NOPE + MLA_ROPE
GROUP_LANE = N_EXPERTS
MOE_CHUNK = 320
MOE_SMALL_CHUNK = 256
MERGE_PART = 256
MOE_VMEM_LIMIT = 56 * 1024 * 1024
Q_SUB = 256
MLA_LOOKAHEAD = 5
DIFF_LOOKAHEAD = 7


def _params(*sem):
    return pltpu.CompilerParams(dimension_semantics=sem, vmem_limit_bytes=VMEM_LIMIT)


def _sigmoid(x):
    return 0.5 * jnp.tanh(0.5 * x) + 0.5


def _rms(x, g):
    return x * lax.rsqrt(jnp.mean(x * x, axis=-1, keepdims=True) + NORM_EPS) * g


def _dot_nt(a, b):
    return lax.dot_general(a, b, (((1,), (1,)), ((), ())), preferred_element_type=F32)


def _const_spec(shape):
    return pl.BlockSpec(shape, lambda *_: (0,) * len(shape))


def _ada_kernel(c_ref, w_ref, b_ref, o_ref):
    c = c_ref[...]
    h = c * _sigmoid(c)
    o_ref[...] = jnp.dot(h, w_ref[...], preferred_element_type=F32,
                         precision=lax.Precision.HIGHEST) + b_ref[...]


def _ada(cc, w_ada, b_ada):
    depth, d, n = w_ada.shape
    bn = 1536
    rows = cc.shape[0]
    return pl.pallas_call(
        _ada_kernel,
        grid=(depth, n // bn),
        in_specs=[pl.BlockSpec((rows, d), lambda l, j: (0, 0)),
                  pl.BlockSpec((None, d, bn), lambda l, j: (l, 0, j)),
                  pl.BlockSpec((None, 1, bn), lambda l, j: (l, 0, j))],
        out_specs=pl.BlockSpec((None, rows, bn), lambda l, j: (l, 0, j)),
        out_shape=jax.ShapeDtypeStruct((depth, rows, n), F32),
        compiler_params=_params("parallel", "parallel"),
        name="adaln",
    )(cc, w_ada, b_ada.reshape(depth, 1, n))


def _swap_halves(t, half):
    n = t.shape[1]
    lane = lax.broadcasted_iota(jnp.int32, t.shape, 1) % (2 * half)
    return jnp.where(lane < half, pltpu.roll(t, n - half, 1), pltpu.roll(t, half, 1))


def _proj_kernel(*refs, rope, scale_a, scale_d):
    if rope:
        (x_ref, mod_ref, n1g_ref, win_ref, qng_ref, wuq_ref, kvg_ref, wuk_ref, wuvt_ref, wdvt_ref,
         ca_ref, sa_ref, cd_ref, sd_ref,
         q_ref, k_ref, vt_ref, u_ref, dq_ref, dk_ref, dvt_ref, g_ref) = refs
    else:
        (x_ref, mod_ref, n1g_ref, win_ref, qng_ref, wuq_ref, kvg_ref, wuk_ref, wuvt_ref, wdvt_ref,
         q_ref, k_ref, vt_ref, u_ref, dq_ref, dk_ref, dvt_ref, g_ref) = refs

    h = _rms(x_ref[...], n1g_ref[...]) * (1.0 + mod_ref[1:2, :]) + mod_ref[0:1, :]
    hb = h.astype(BF16)

    def proj(a, b):
        return jnp.dot(hb, win_ref[:, a:b], preferred_element_type=F32)

    def rope_a(t):
        return t * ca_ref[...] + _swap_halves(t, MLA_ROPE // 2) * sa_ref[...] if rope else t

    def rope_d(t):
        return t * cd_ref[...] + _swap_halves(t, DIFF_HD // 2) * sd_ref[...] if rope else t

    cq = _rms(proj(C_CQ, C_CKV), qng_ref[...]).astype(BF16)
    ckv = _rms(proj(C_CKV, C_KR), kvg_ref[...]).astype(BF16)
    kr = rope_a(proj(C_KR, C_GA))
    lane = lax.broadcasted_iota(jnp.int32, kr.shape, 1)
    kr = jnp.where(lane == MLA_REF_LANE, 1.0, kr)
    q = jnp.dot(cq, wuq_ref[...], preferred_element_type=F32)
    kn = jnp.dot(ckv, wuk_ref[...], preferred_element_type=F32)
    vt = _dot_nt(wuvt_ref[...], ckv)
    ones = jnp.ones((ONES_ROWS, vt.shape[1]), BF16)
    for hd in range(MLA_HEADS):
        blk = slice(hd * LANES, (hd + 1) * LANES)
        q_ref[:, blk] = (rope_a(q[:, blk]) * scale_a).astype(BF16)
        k_ref[:, blk] = (kn[:, blk] + kr).astype(BF16)
        vt_ref[hd * VT_A:hd * VT_A + MLA_V, :] = vt[hd * MLA_V:(hd + 1) * MLA_V, :].astype(BF16)
        vt_ref[hd * VT_A + MLA_V:(hd + 1) * VT_A, :] = ones

    u_ref[...] = proj(C_GA, C_GG) * _sigmoid(proj(C_GG, C_DQ))

    dq = proj(C_DQ, C_DK)
    dk = proj(C_DK, C_GATES)
    dvt = _dot_nt(wdvt_ref[...], hb)
    for hd in range(DIFF_HEADS):
        blk = slice(hd * LANES, (hd + 1) * LANES)
        qh = rope_d(dq[:, blk]) * scale_d
        kh = rope_d(dk[:, blk])
        low = lane < DIFF_HD
        for mp, (qm, km) in enumerate(((qh, kh), (pltpu.roll(qh, DIFF_HD, 1), pltpu.roll(kh, DIFF_HD, 1)))):
            mblk = slice((2 * hd + mp) * LANES, (2 * hd + mp + 1) * LANES)
            dq_ref[:, mblk] = jnp.where(low, qm, 0.0).astype(BF16)
            dk_ref[:, mblk] = jnp.where(low, km, jnp.where(lane == DIFF_HD, 1.0, 0.0)).astype(BF16)
        dvt_ref[hd * VT_D:hd * VT_D + LANES, :] = dvt[blk, :].astype(BF16)
        dvt_ref[hd * VT_D + LANES:(hd + 1) * VT_D, :] = ones

    d = x_ref.shape[1]
    for j in range(3):
        g_ref[:, j * d:(j + 1) * d] = _sigmoid(proj(C_GATES + j * d, C_GATES + (j + 1) * d)).astype(BF16)


def _proj(x, mod, mod_row, n1g, win, qng, wuq, kvg, wuk, wuvt, wdvt, tables, seq, tm):
    t, d = x.shape
    rope = tables is not None
    tiles_per_seq = seq // tm
    row = lambda i: (i, 0)
    col = lambda i: (0, i)
    if mod_row is None:
        mod_map = lambda i: (i // tiles_per_seq, 0, 0)
    else:
        mod_map = lambda i: (mod_row, 0, 0)
    consts = (n1g, win, qng, wuq, kvg, wuk, wuvt, wdvt)
    in_specs = [pl.BlockSpec((tm, d), row), pl.BlockSpec((None, 6, d), mod_map)]
    in_specs += [_const_spec(a.shape) for a in consts]
    args = [x, mod, *consts]
    if rope:
        in_specs += [pl.BlockSpec((tm, LANES), lambda i: (i % tiles_per_seq, 0))] * 4
        args += list(tables)
    outs = [((t, MLA_HEADS * LANES), (tm, MLA_HEADS * LANES), row, BF16),
            ((t, MLA_HEADS * LANES), (tm, MLA_HEADS * LANES), row, BF16),
            ((MLA_HEADS * VT_A, t), (MLA_HEADS * VT_A, tm), col, BF16),
            ((t, CONV_CH), (tm, CONV_CH), row, F32),
            ((t, 2 * DIFF_HEADS * LANES), (tm, 2 * DIFF_HEADS * LANES), row, BF16),
            ((t, 2 * DIFF_HEADS * LANES), (tm, 2 * DIFF_HEADS * LANES), row, BF16),
            ((DIFF_HEADS * VT_D, t), (DIFF_HEADS * VT_D, tm), col, BF16),
            ((t, 3 * d), (tm, 3 * d), row, BF16)]
    kern = functools.partial(_proj_kernel, rope=rope,
                             scale_a=LOG2E / math.sqrt(MLA_NOPE + MLA_ROPE), scale_d=LOG2E / math.sqrt(DIFF_HD))
    return pl.pallas_call(
        kern,
        grid=(t // tm,),
        in_specs=in_specs,
        out_specs=[pl.BlockSpec(blk, imap) for _, blk, imap, _ in outs],
        out_shape=[jax.ShapeDtypeStruct(shape, dt) for shape, _, _, dt in outs],
        compiler_params=_params("parallel"),
        name="proj_rope" if rope else "proj_ctx",
    )(*args)


def _flash_streams(streams, lookahead, ref_row=None):
    n_chunks = len(streams[0][1])
    items = [(sid, c) for c in range(n_chunks) for sid in range(len(streams))]
    m = [None] * len(streams)
    acc = [None] * len(streams)
    ref = [None] * len(streams)
    if ref_row is not None and n_chunks > 1:
        assert len(streams) > 1
        lookahead = min(lookahead, len(streams) - 1)

    def finish(sid, st, vt):
        cmax = jnp.max(st, axis=0, keepdims=True)
        m_new = cmax if m[sid] is None else jnp.maximum(m[sid], cmax)
        pt = jnp.exp2((st - m_new).astype(BF16))
        pv = jnp.dot(vt, pt, preferred_element_type=F32)
        acc[sid] = pv if m[sid] is None else acc[sid] * jnp.exp2(m[sid] - m_new) + pv
        m[sid] = m_new

    def finish_folded(sid, dt, vt):
        db = dt.astype(BF16)
        cmax = jnp.max(db, axis=0, keepdims=True)
        first = m[sid] is None
        delta = cmax if first else jnp.maximum(cmax, jnp.zeros_like(cmax))
        pt = jnp.exp2(db - delta)
        pv = jnp.dot(vt, pt, preferred_element_type=F32)
        m_new = delta.astype(F32) if first else ref[sid] + delta.astype(F32)
        acc[sid] = pv if first else acc[sid] * jnp.exp2(m[sid] - m_new) + pv
        m[sid] = m_new
        ref[sid] = m_new.astype(BF16).astype(F32)

    qts = [q.astype(F32).T.astype(BF16) for q, _ in streams]
    pending = []
    for sid, c in items:
        k, vt = streams[sid][1][c]
        qt = qts[sid]
        if ref_row is not None and ref[sid] is not None:
            row = lax.broadcasted_iota(jnp.int32, qt.shape, 0)
            qt = jnp.where(row == ref_row, jnp.broadcast_to((-ref[sid]).astype(BF16), qt.shape), qt)
        st = jnp.dot(k, qt, preferred_element_type=F32)
        if len(pending) == lookahead:
            (finish if ref_row is None else finish_folded)(*pending.pop(0))
        pending.append((sid, st, vt))
    for item in pending:
        (finish if ref_row is None else finish_folded)(*item)
    return acc


def _key_chunks(kc_ref, vtc_ref, kl_ref, vtl_ref, tk, kcols, vrows):
    chunks = [(kc_ref[:, kcols], vtc_ref[vrows, :])]
    if kl_ref is not None:
        for j in range(kl_ref.shape[0] // tk):
            keys = slice(j * tk, (j + 1) * tk)
            chunks.append((kl_ref[keys, kcols], vtl_ref[vrows, keys]))
    return chunks


def _cast_riders(cast, grid):
    if not cast:
        return [], [], [], []
    layer, arrs = cast
    steps = math.prod(grid)
    strides = [math.prod(grid[i + 1:]) for i in range(len(grid))]
    step = lambda *ids: sum(i * st for i, st in zip(ids, strides))
    in_specs, out_specs, shapes = [], [], []
    for a in arrs:
        _, e, r, c = a.shape
        if steps >= e:
            per = steps // e
            rows = r // per
            assert steps % e == 0 and r % per == 0 and rows % 16 == 0
            in_specs.append(pl.BlockSpec((None, None, rows, c),
                                         lambda *ids, per=per: (layer, step(*ids) // per, step(*ids) % per, 0)))
            out_specs.append(pl.BlockSpec((None, rows, c),
                                          lambda *ids, per=per: (step(*ids) // per, step(*ids) % per, 0)))
        else:
            assert e % steps == 0
            in_specs.append(pl.BlockSpec((None, e // steps, r, c), lambda *ids: (layer, step(*ids), 0, 0)))
            out_specs.append(pl.BlockSpec((e // steps, r, c), lambda *ids: (step(*ids), 0, 0)))
        shapes.append(jax.ShapeDtypeStruct((e, r, c), BF16))
    return list(arrs), in_specs, out_specs, shapes


def _run_casts(in_refs, out_refs):
    for w_ref, o_ref in zip(in_refs, out_refs):
        o_ref[...] = w_ref[...].astype(BF16)


def _mla_attn_kernel(*refs, latent, tk, n_cast=0):
    if n_cast:
        _run_casts(refs[5:5 + n_cast], refs[6 + n_cast:])
        refs = refs[:5] + refs[5 + n_cast:6 + n_cast]
    if latent:
        q_ref, kc_ref, vtc_ref, kl_ref, vtl_ref, o_ref = refs
    else:
        q_ref, kc_ref, vtc_ref, o_ref = refs
        kl_ref = vtl_ref = None
    subs = [slice(r, r + Q_SUB) for r in range(0, q_ref.shape[0], Q_SUB)]
    streams = []
    for qs in subs:
        for hh in range(2):
            cols = slice(hh * LANES, (hh + 1) * LANES)
            rows = slice(hh * VT_A, (hh + 1) * VT_A)
            streams.append((q_ref[qs, cols], _key_chunks(kc_ref, vtc_ref, kl_ref, vtl_ref, tk, cols, rows)))
    outs = [acc[0:MLA_V, :] * (1.0 / acc[MLA_V:MLA_V + 1, :]) for acc in _flash_streams(streams, MLA_LOOKAHEAD, ref_row=MLA_REF_LANE)]
    for i, qs in enumerate(subs):
        o_ref[qs, :] = jnp.concatenate(outs[2 * i:2 * i + 2], axis=0).T.astype(BF16)


def _mla_attn(q, kc, vtc, kl, vtl, tq, tk, cast=()):
    b, lq, _ = q.shape
    c = kc.shape[1]
    latent = kl is not None
    grid = (b, MLA_HEADS // 2, lq // tq)
    in_specs = [pl.BlockSpec((None, tq, 2 * LANES), lambda i, p, j: (i, j, p)),
                pl.BlockSpec((None, c, 2 * LANES), lambda i, p, j: (i, 0, p)),
                pl.BlockSpec((2 * VT_A, c), lambda i, p, j: (p, i))]
    args = [q, kc, vtc]
    if latent:
        s = kl.shape[1]
        in_specs += [pl.BlockSpec((None, s, 2 * LANES), lambda i, p, j: (i, 0, p)),
                     pl.BlockSpec((2 * VT_A, s), lambda i, p, j: (p, i))]
        args += [kl, vtl]
    cast_args, cast_in, cast_out, cast_shapes = _cast_riders(cast, grid)
    outs = pl.pallas_call(
        functools.partial(_mla_attn_kernel, latent=latent, tk=tk, n_cast=len(cast_args)),
        grid=grid,
        in_specs=in_specs + cast_in,
        out_specs=[pl.BlockSpec((None, tq, LANES), lambda i, p, j: (i, j, p))] + cast_out,
        out_shape=[jax.ShapeDtypeStruct((b, lq, MLA_HEADS * MLA_V), BF16)] + cast_shapes,
        compiler_params=_params("parallel", "parallel", "parallel"),
        name="mla_attn_lat" if latent else "mla_attn_ctx",
    )(*args, *cast_args)
    return outs[0], list(outs[1:])


def _diff_attn_kernel(*refs, latent, tk, lam_init, n_cast=0):
    if n_cast:
        _run_casts(refs[7:7 + n_cast], refs[8 + n_cast:])
        refs = refs[:7] + refs[7 + n_cast:8 + n_cast]
    if latent:
        lam_ref, g_ref, q_ref, kc_ref, vtc_ref, kl_ref, vtl_ref, o_ref = refs
    else:
        lam_ref, g_ref, q_ref, kc_ref, vtc_ref, o_ref = refs
        kl_ref = vtl_ref = None
    subs = [slice(r, r + Q_SUB) for r in range(0, q_ref.shape[0], Q_SUB)]
    streams = []
    for qs in subs:
        for mp in range(2):
            cols = slice(mp * LANES, (mp + 1) * LANES)
            chunks = _key_chunks(kc_ref, vtc_ref, kl_ref, vtl_ref, tk, cols, slice(None))
            streams.append((q_ref[qs, cols], chunks))
    accs = _flash_streams(streams, DIFF_LOOKAHEAD, ref_row=DIFF_HD)
    outs = [acc[0:LANES, :] * (1.0 / acc[LANES:LANES + 1, :]) for acc in accs]
    lam = (jnp.exp(jnp.sum(lam_ref[0:1, :] * lam_ref[1:2, :], axis=-1, keepdims=True))
           - jnp.exp(jnp.sum(lam_ref[2:3, :] * lam_ref[3:4, :], axis=-1, keepdims=True)) + lam_init)
    for i, qs in enumerate(subs):
        o = outs[2 * i] - lam * outs[2 * i + 1]
        o = o * lax.rsqrt(jnp.mean(o * o, axis=0, keepdims=True) + NORM_EPS) * (g_ref[...] * (1.0 - lam_init))
        o_ref[qs, :] = o.T.astype(BF16)


def _diff_attn(lam_p, subln_g, q, kc, vtc, kl, vtl, tq, tk, lam_init, cast=()):
    b, lq, _ = q.shape
    c = kc.shape[1]
    latent = kl is not None
    grid = (b, DIFF_HEADS, lq // tq)
    kblk = lambda rows: pl.BlockSpec((None, rows, 2 * LANES), lambda i, h, j: (i, 0, h))
    vblk = lambda keys: pl.BlockSpec((VT_D, keys), lambda i, h, j: (h, i))
    in_specs = [_const_spec(lam_p.shape), _const_spec(subln_g.shape),
                pl.BlockSpec((None, tq, 2 * LANES), lambda i, h, j: (i, j, h)), kblk(c), vblk(c)]
    args = [lam_p, subln_g, q, kc, vtc]
    if latent:
        in_specs += [kblk(kl.shape[1]), vblk(kl.shape[1])]
        args += [kl, vtl]
    cast_args, cast_in, cast_out, cast_shapes = _cast_riders(cast, grid)
    outs = pl.pallas_call(
        functools.partial(_diff_attn_kernel, latent=latent, tk=tk, lam_init=lam_init, n_cast=len(cast_args)),
        grid=grid,
        in_specs=in_specs + cast_in,
        out_specs=[pl.BlockSpec((None, tq, LANES), lambda i, h, j: (i, j, h))] + cast_out,
        out_shape=[jax.ShapeDtypeStruct((b, lq, DIFF_HEADS * LANES), BF16)] + cast_shapes,
        compiler_params=_params("parallel", "parallel", "parallel"),
        name="diff_attn_lat" if latent else "diff_attn_ctx",
    )(*args, *cast_args)
    return outs[0], list(outs[1:])


def _conv_fill(first, last, prev_ref, cur_ref, next_ref, pad_ref, shift_ref):
    tr = cur_ref.shape[0]
    zero = jnp.zeros((HALO, cur_ref.shape[1]), F32)
    pad_ref[0:HALO, :] = jnp.where(first, zero, prev_ref[...])
    pad_ref[HALO:HALO + tr, :] = cur_ref[...]
    pad_ref[HALO + tr:, :] = jnp.where(last, zero, next_ref[...])
    span = shift_ref.shape[1]
    for phase in range(1, SUBLANES):
        shift_ref[phase] = pad_ref[phase:phase + span, :]


def _conv_rows(pad_ref, shift_ref, w_ref, b_ref, g_ref, beta_ref, r0, n):
    off = HALO - CONV_WIDTH // 2
    acc = jnp.zeros((n, pad_ref.shape[1]), F32) + b_ref[...]
    for tap in range(CONV_WIDTH):
        phase = (off + tap) % SUBLANES
        base = r0 + off + tap - phase
        win = pad_ref[base:base + n, :] if phase == 0 else shift_ref[phase, base:base + n, :]
        acc = acc + win * w_ref[tap:tap + 1, :]
    mu = jnp.mean(acc, axis=-1, keepdims=True)
    cen = acc - mu
    y = cen * lax.rsqrt(jnp.mean(cen * cen, axis=-1, keepdims=True) + NORM_EPS) * g_ref[...] + beta_ref[...]
    return (y * _sigmoid(y)).astype(BF16)


def _conv_kernel(prev_ref, cur_ref, next_ref, w_ref, b_ref, g_ref, beta_ref, o_ref, pad_ref, shift_ref):
    j = pl.program_id(1)
    _conv_fill(j == 0, j == pl.num_programs(1) - 1, prev_ref, cur_ref, next_ref, pad_ref, shift_ref)
    o_ref[...] = _conv_rows(pad_ref, shift_ref, w_ref, b_ref, g_ref, beta_ref, 0, cur_ref.shape[0])


def _conv(u, w, bias, g, beta, tr):
    b, l, ch = u.shape
    per = tr // HALO
    last = l // HALO - 1
    return pl.pallas_call(
        _conv_kernel,
        grid=(b, l // tr),
        in_specs=[pl.BlockSpec((None, HALO, ch), lambda i, j: (i, jnp.maximum(j * per - 1, 0), 0)),
                  pl.BlockSpec((None, tr, ch), lambda i, j: (i, j, 0)),
                  pl.BlockSpec((None, HALO, ch), lambda i, j: (i, jnp.minimum((j + 1) * per, last), 0)),
                  _const_spec(w.shape), _const_spec(bias.shape), _const_spec(g.shape), _const_spec(beta.shape)],
        out_specs=pl.BlockSpec((None, tr, ch), lambda i, j: (i, j, 0)),
        out_shape=jax.ShapeDtypeStruct((b, l, ch), BF16),
        scratch_shapes=[pltpu.VMEM((tr + 2 * HALO, ch), F32),
                        pltpu.VMEM((SUBLANES, tr + 2 * HALO - SUBLANES, ch), F32)],
        compiler_params=_params("parallel", "parallel"),
        name="conv_ln_silu",
    )(u, u, u, w, bias, g, beta)


def _route(logits):
    lane = lax.broadcasted_iota(jnp.int32, logits.shape, 1)
    neg = jnp.float32(-jnp.inf)
    big = jnp.int32(LANES)
    is_grp = (lane >= N_EXPERTS) & (lane < N_EXPERTS + N_GROUPS)
    gl = jnp.where(is_grp, logits, neg)
    gmax = jnp.max(gl, axis=-1, keepdims=True)
    gsel = jnp.min(jnp.where(gl == gmax, lane, big), axis=-1, keepdims=True) - N_EXPERTS
    p_g = 1.0 / jnp.sum(jnp.exp(gl - gmax), axis=-1, keepdims=True)
    in_grp = (lane >= gsel * EXPERTS_PER_GROUP) & (lane < (gsel + 1) * EXPERTS_PER_GROUP)
    e1 = jnp.where(in_grp, logits, neg)
    v1 = jnp.max(e1, axis=-1, keepdims=True)
    i1 = jnp.min(jnp.where(e1 == v1, lane, big), axis=-1, keepdims=True)
    e2 = jnp.where(lane == i1, neg, e1)
    v2 = jnp.max(e2, axis=-1, keepdims=True)
    i2 = jnp.min(jnp.where(e2 == v2, lane, big), axis=-1, keepdims=True)
    r = jnp.exp(v2 - v1)
    w1 = p_g / (1.0 + r)
    comb = jnp.where(lane == i1, w1, jnp.where(lane == i2, w1 * r, 0.0))
    return jnp.where(lane == GROUP_LANE, gsel.astype(F32), comb)


def _merge_kernel(*refs, tiles_per_seq):
    if tiles_per_seq is None:
        (x_ref, om_ref, oc_ref, od_ref, gates_ref, mod_ref, wm_ref, wc_ref, wd_ref, wout_ref,
         n2g_ref, wr_ref, br_ref, xn_ref, h2_ref, comb_ref) = refs
        conv_branch = lambda rows: oc_ref[rows, :]
    else:
        (x_ref, om_ref, up_ref, uc_ref, un_ref, cw_ref, cb_ref, cg_ref, cbeta_ref, od_ref, gates_ref, mod_ref,
         wm_ref, wc_ref, wd_ref, wout_ref, n2g_ref, wr_ref, br_ref, xn_ref, h2_ref, comb_ref,
         pad_ref, shift_ref) = refs
        ts = pl.program_id(0) % tiles_per_seq
        _conv_fill(ts == 0, ts == tiles_per_seq - 1, up_ref, uc_ref, un_ref, pad_ref, shift_ref)
        conv_branch = lambda rows: _conv_rows(pad_ref, shift_ref, cw_ref, cb_ref, cg_ref, cbeta_ref,
                                              rows.start, rows.stop - rows.start)
    tm, d = x_ref.shape
    parts = [slice(r, r + MERGE_PART) for r in range(0, tm, MERGE_PART)]

    def branch_sum(rows):
        gate = lambda j: gates_ref[rows, j * d:(j + 1) * d].astype(F32)
        return (gate(0) * jnp.dot(om_ref[rows, :], wm_ref[...], preferred_element_type=F32)
                + gate(1) * jnp.dot(conv_branch(rows), wc_ref[...], preferred_element_type=F32)
                + gate(2) * jnp.dot(od_ref[rows, :], wd_ref[...], preferred_element_type=F32))

    def residual_norm(rows, y):
        z = jnp.dot(y.astype(BF16), wout_ref[...], preferred_element_type=F32)
        xn = x_ref[rows, :] + mod_ref[2:3, :] * z
        xn_ref[rows, :] = xn
        h2 = _rms(xn, n2g_ref[...]) * (1.0 + mod_ref[4:5, :]) + mod_ref[3:4, :]
        h_hi = h2.astype(BF16)
        h2_ref[rows, :] = h_hi
        return h_hi, (h2 - h_hi.astype(F32)).astype(BF16)

    def router(rows, h_hi, h_lo):
        hw = jnp.dot(h_hi, wr_ref[...], preferred_element_type=F32)
        lw = jnp.dot(h_lo, wr_ref[:, 0:LANES], preferred_element_type=F32)
        comb_ref[rows, :] = _route(hw[:, 0:LANES] + hw[:, LANES:] + lw + br_ref[...])

    ys = [branch_sum(rows) for rows in parts]
    hs = [residual_norm(rows, y) for rows, y in zip(parts, ys)]
    for rows, (h_hi, h_lo) in zip(parts, hs):
        router(rows, h_hi, h_lo)


def _merge(x, om, oc, od, gates, mod, mod_row, wm, wc, wd, wout, n2g, wr, br, seq, tm, conv=None):
    t, d = x.shape
    tiles_per_seq = seq // tm
    row = lambda i: (i, 0)
    if mod_row is None:
        mod_map = lambda i: (i // tiles_per_seq, 0, 0)
    else:
        mod_map = lambda i: (mod_row, 0, 0)
    rows = lambda a: pl.BlockSpec((tm, a.shape[1]), row)
    scratch = []
    if conv is None:
        conv_specs, conv_args = [rows(oc)], [oc]
    else:
        ch = oc.shape[1]
        per = tm // HALO
        last = t // HALO - 1
        conv_specs = [pl.BlockSpec((HALO, ch), lambda i: (jnp.maximum(i * per - 1, 0), 0)), rows(oc),
                      pl.BlockSpec((HALO, ch), lambda i: (jnp.minimum((i + 1) * per, last), 0))]
        conv_specs += [_const_spec(a.shape) for a in conv]
        conv_args = [oc, oc, oc, *conv]
        scratch = [pltpu.VMEM((tm + 2 * HALO, ch), F32),
                   pltpu.VMEM((SUBLANES, tm + 2 * HALO - SUBLANES, ch), F32)]
    return pl.pallas_call(
        functools.partial(_merge_kernel, tiles_per_seq=None if conv is None else tiles_per_seq),
        grid=(t // tm,),
        in_specs=[rows(x), rows(om), *conv_specs, rows(od), rows(gates),
                  pl.BlockSpec((None, 6, d), mod_map)]
                 + [_const_spec(a.shape) for a in (wm, wc, wd, wout, n2g, wr, br)],
        out_specs=[pl.BlockSpec((tm, d), row), pl.BlockSpec((tm, d), row), pl.BlockSpec((tm, LANES), row)],
        out_shape=[jax.ShapeDtypeStruct((t, d), F32), jax.ShapeDtypeStruct((t, d), BF16),
                   jax.ShapeDtypeStruct((t, LANES), F32)],
        scratch_shapes=scratch,
        compiler_params=_params("parallel"),
        name="merge_route",
    )(x, om, *conv_args, od, gates, mod, wm, wc, wd, wout, n2g, wr, br)


def _moe_kernel(h_ref, comb_ref, xn_ref, mod_ref, tri_ref, wg_ref, wu_ref, wd_ref, fg_ref, o_ref,
                pos_ref, comb3_ref, *, final, mod_rows):
    g = pl.program_id(1)
    slabs, slab_rows, d = o_ref.shape
    tm = slabs * slab_rows

    @pl.when(g == 0)
    def _():
        o_ref[...] = jnp.zeros_like(o_ref)
        comb = comb_ref[...].reshape(tm, LANES)
        hi = comb.astype(BF16)
        r1 = comb - hi.astype(F32)
        mid = r1.astype(BF16)
        lo = (r1 - mid.astype(F32)).astype(BF16)
        comb3_ref[...] = jnp.concatenate([hi, mid, lo], axis=1)
        grow = comb.T[GROUP_LANE:GROUP_LANE + 1, :]
        gid = lax.broadcasted_iota(jnp.int32, (8, tm), 0).astype(F32)
        member = grow == gid
        rank = jnp.dot(jnp.where(member, 1.0, 0.0).astype(BF16), tri_ref[...], preferred_element_type=F32)
        pos_ref[...] = jnp.where(member, rank, -1.0)

    row = lax.broadcasted_iota(jnp.int32, (8, tm), 0)
    posg = jnp.sum(jnp.where(row == g, pos_ref[...], 0.0), axis=0, keepdims=True)
    count = jnp.max(posg).astype(jnp.int32) + 1

    def chunk(c, rows):
        slot = (lax.broadcasted_iota(jnp.int32, (rows, tm), 0) + c * rows).astype(F32)
        sel = jnp.where(posg == slot, 1.0, 0.0).astype(BF16)
        h = h_ref[...].reshape(tm, d)
        x = jnp.dot(sel, h, preferred_element_type=F32).astype(BF16)
        cw3 = jnp.dot(sel, comb3_ref[...], preferred_element_type=F32)
        cw = cw3[:, 0:LANES] + cw3[:, LANES:2 * LANES] + cw3[:, 2 * LANES:]
        lane = lax.broadcasted_iota(jnp.int32, cw.shape, 1)
        y = jnp.zeros((rows, d), F32)
        for e in range(EXPERTS_PER_GROUP):
            a = jnp.dot(x, wg_ref[e], preferred_element_type=F32)
            u = jnp.dot(x, wu_ref[e], preferred_element_type=F32)
            w = jnp.sum(jnp.where(lane == g * EXPERTS_PER_GROUP + e, cw, 0.0), axis=1, keepdims=True)
            hid = a * _sigmoid(a) * u * w
            y = y + jnp.dot(hid.astype(BF16), wd_ref[e], preferred_element_type=F32)
        back = lax.dot_general(sel, y.astype(BF16), (((0,), (0,)), ((), ())), preferred_element_type=F32)
        o_ref[...] += back.reshape(slabs, slab_rows, d)

    @pl.when((count > 0) & (count <= MOE_SMALL_CHUNK))
    def _():
        chunk(0, MOE_SMALL_CHUNK)

    @pl.when(count > MOE_SMALL_CHUNK)
    def _():
        def body(c, carry):
            chunk(c, MOE_CHUNK)
            return carry
        lax.fori_loop(0, (count + MOE_CHUNK - 1) // MOE_CHUNK, body, 0)

    @pl.when(g == pl.num_programs(1) - 1)
    def _():
        for sl, mod_row in enumerate(mod_rows):
            out = xn_ref[sl] + mod_ref[mod_row, 5:6, :] * o_ref[sl]
            o_ref[sl] = _rms(out, fg_ref[...]) if final else out


def _moe(h2, comb, xn, mod, mod_rows, wg, wu, wd, final_g, tm, final):
    t, d = xn.shape
    slabs = len(mod_rows)
    slab_rows = tm // slabs
    n_tiles = t // tm
    view = lambda a: a.reshape(slabs, n_tiles, slab_rows, a.shape[-1])
    tile = lambda a: pl.BlockSpec((slabs, None, slab_rows, a.shape[-1]), lambda i, g: (0, i, 0, 0))
    ff = wg.shape[2]
    idx = jnp.arange(tm)
    tri = (idx[:, None] < idx[None, :]).astype(BF16)
    grp = lambda i, g: (g, 0, 0)
    once = pl.Buffered(1)
    xn_spec = pl.BlockSpec((slabs, None, slab_rows, d), lambda i, g: (0, i, 0, 0))
    out = pl.pallas_call(
        functools.partial(_moe_kernel, final=final, mod_rows=tuple(mod_rows)),
        grid=(n_tiles, N_GROUPS),
        in_specs=[tile(h2), tile(comb), xn_spec,
                  _const_spec(mod.shape),
                  pl.BlockSpec((tm, tm), lambda i, g: (0, 0), pipeline_mode=once),
                  pl.BlockSpec((EXPERTS_PER_GROUP, d, ff), grp),
                  pl.BlockSpec((EXPERTS_PER_GROUP, d, ff), grp),
                  pl.BlockSpec((EXPERTS_PER_GROUP, ff, d), grp),
                  pl.BlockSpec(final_g.shape, lambda i, g: (0, 0))],
        out_specs=tile(xn),
        out_shape=jax.ShapeDtypeStruct((slabs, n_tiles, slab_rows, d), F32),
        scratch_shapes=[pltpu.VMEM((8, tm), F32), pltpu.VMEM((tm, 3 * LANES), BF16)],
        compiler_params=pltpu.CompilerParams(dimension_semantics=("parallel", "arbitrary"),
                                             vmem_limit_bytes=MOE_VMEM_LIMIT),
        name="moe",
    )(view(h2), view(comb), view(xn), mod, tri, wg, wu, wd, final_g)
    return out.reshape(t, d)


def _rope_tables(s):
    rows = s // GRID_W
    row = jnp.repeat(jnp.arange(rows, dtype=F32), GRID_W)
    col = jnp.tile(jnp.arange(GRID_W, dtype=F32), rows)

    def cos_sin(rot_dim):
        n = rot_dim // 4
        inv = ROPE_BASE ** (-jnp.arange(n, dtype=F32) / n)
        ang = jnp.concatenate([row[:, None] * inv, col[:, None] * inv], axis=-1)
        return jnp.cos(ang), jnp.sin(ang)

    ca, sa = cos_sin(MLA_ROPE)
    one = jnp.ones((s, MLA_NOPE), F32)
    pad1 = jnp.ones((s, LANES - MLA_NOPE - MLA_ROPE), F32)
    cos_a = jnp.concatenate([one, ca, ca, pad1], axis=-1)
    sin_a = jnp.concatenate([0 * one, -sa, sa, 0 * pad1], axis=-1)
    cd, sd = cos_sin(DIFF_HD)
    cos_d = jnp.concatenate([cd, cd, cd, cd], axis=-1)
    sin_d = jnp.concatenate([-sd, sd, -sd, sd], axis=-1)
    return cos_a, sin_a, cos_d, sin_d


def _layout_w_in(w_in):
    d = w_in.shape[0]
    o = 0
    parts = {}
    for name, width in (("cq", Q_LORA), ("ckv", KV_LORA), ("kr", MLA_ROPE), ("glu", 2 * CONV_CH),
                        ("dq", 2 * DIFF_HEADS * DIFF_HD), ("dk", 2 * DIFF_HEADS * DIFF_HD),
                        ("dv", 2 * DIFF_HEADS * DIFF_HD), ("gates", 3 * d)):
        parts[name] = w_in[:, o:o + width]
        o += width
    kr_blk = jnp.concatenate([jnp.zeros((d, MLA_NOPE), w_in.dtype), parts["kr"],
                              jnp.zeros((d, LANES - MLA_NOPE - MLA_ROPE), w_in.dtype)], axis=1)
    win = jnp.concatenate([parts["cq"], parts["ckv"], kr_blk, parts["glu"], parts["dq"], parts["dk"],
                           parts["gates"]], axis=1)
    return win.astype(BF16), parts["dv"].T.astype(BF16)


def _layout_w_uq(w_uq):
    r = w_uq.shape[0]
    w = w_uq.reshape(r, MLA_HEADS, MLA_NOPE + MLA_ROPE)
    w = jnp.pad(w, ((0, 0), (0, 0), (0, LANES - MLA_NOPE - MLA_ROPE)))
    return w.reshape(r, MLA_HEADS * LANES).astype(BF16)


def _layout_w_ukv(w_ukv):
    r = w_ukv.shape[0]
    w = w_ukv.reshape(r, MLA_HEADS, MLA_NOPE + MLA_V)
    wk = jnp.pad(w[:, :, :MLA_NOPE], ((0, 0), (0, 0), (0, LANES - MLA_NOPE))).reshape(r, MLA_HEADS * LANES)
    wvt = w[:, :, MLA_NOPE:].reshape(r, MLA_HEADS * MLA_V).T
    return wk.astype(BF16), wvt.astype(BF16)


def _layout_router(w_rg, b_rg, w_re, b_re):
    d = w_rg.shape[0]
    w = jnp.concatenate([w_re, w_rg, jnp.zeros((d, LANES - N_EXPERTS - N_GROUPS), F32)], axis=1)
    hi = w.astype(BF16)
    lo = (w - hi.astype(F32)).astype(BF16)
    b = jnp.concatenate([b_re, b_rg, jnp.zeros((LANES - N_EXPERTS - N_GROUPS,), F32)])[None, :]
    return jnp.concatenate([hi, lo], axis=1), b


def kernel(x, c, ctx, c_ctx, w_ada, b_ada, norm1_g, w_in, q_norm_g, w_uq, kv_norm_g, w_ukv, conv_w, conv_b,
           conv_ln_g, conv_ln_b, lam_q1, lam_k1, lam_q2, lam_k2, diff_subln_g, w_o_mla, w_o_conv, w_o_diff,
           w_out, norm2_g, w_rg, b_rg, w_re, b_re, w_gate, w_up, w_down, final_g):
    b, s, d = x.shape
    n_ctx = ctx.shape[1]
    depth = w_ada.shape[0]
    tm = 512
    tq = min(4 * Q_SUB, s)
    tq_c = min(tq, n_ctx)
    tk = min(256, s)
    tr = 256
    tmoe_l = min(1024, b * s)
    tmoe_c = min(1024, b * n_ctx)
    ctx_row = b

    rows = -(-(b + 1) // 8) * 8
    cc = jnp.zeros((rows, d), F32).at[:b].set(c).at[b].set(c_ctx)
    mod_all = _ada(cc, w_ada, b_ada).reshape(depth, rows, 6, d)
    tables = _rope_tables(s)
    row2 = lambda a: a[None, :]

    xl = x.reshape(b * s, d)
    xc = ctx.reshape(b * n_ctx, d)
    for l in range(depth):
        update_ctx = l < depth - 1
        final = l == depth - 1
        lam_init = 0.8 - 0.6 * math.exp(-0.3 * l)
        mod = mod_all[l]
        win, wdvt = _layout_w_in(w_in[l])
        wuq = _layout_w_uq(w_uq[l])
        wuk, wuvt = _layout_w_ukv(w_ukv[l])
        n1g, qng, kvg = row2(norm1_g[l]), row2(q_norm_g[l]), row2(kv_norm_g[l])
        lam_p = jnp.stack([lam_q1[l], lam_k1[l], lam_q2[l], lam_k2[l]])
        subln = diff_subln_g[l][:, None]

        pc = _proj(xc, mod, ctx_row, n1g, win, qng, wuq, kvg, wuk, wuvt, wdvt, None, n_ctx, tm)
        pl_ = _proj(xl, mod, None, n1g, win, qng, wuq, kvg, wuk, wuvt, wdvt, tables, s, tm)
        per_batch = lambda arrs, n: [a if i in (2, 6) else a.reshape(b, n, -1) for i, a in enumerate(arrs)]
        qc, kc, vc, uc, dqc, dkc, dvc, gc = per_batch(pc, n_ctx)
        ql, kl, vl, ul, dql, dkl, dvl, gl = per_batch(pl_, s)

        wm, wc, wd = w_o_mla[l].astype(BF16), w_o_conv[l].astype(BF16), w_o_diff[l].astype(BF16)
        wout = w_out[l].astype(BF16)
        wr, br = _layout_router(w_rg[l], b_rg[l], w_re[l], b_re[l])
        n2g = row2(norm2_g[l])
        cw, cb = conv_w[l], row2(conv_b[l])
        cg, cbeta = row2(conv_ln_g[l]), row2(conv_ln_b[l])
        fg = row2(final_g)

        def tail(xs, om, oc, od, gates, mod_row, seq, tmoe, is_final, conv=None):
            flat = lambda a: a.reshape(-1, a.shape[-1])
            xn, h2, comb = _merge(xs, flat(om), flat(oc), flat(od), flat(gates), mod, mod_row,
                                  wm, wc, wd, wout, n2g, wr, br, seq, tm, conv=conv)
            moe_rows = tuple(range(b)) if mod_row is None else (mod_row,)
            return _moe(h2, comb, xn, mod, moe_rows, wg, wu, wdn, fg, tmoe, is_final)

        om_l, (wg, wu) = _mla_attn(ql, kc, vc, kl, vl, tq, tk, cast=(l, [w_gate, w_up]))
        od_l, (wdn,) = _diff_attn(lam_p, subln, dql, dkc, dvc, dkl, dvl, tq, tk, lam_init, cast=(l, [w_down]))
        xl_new = tail(xl, om_l, ul, od_l, gl, None, s, tmoe_l, final, conv=(cw, cb, cg, cbeta))
        if update_ctx:
            om_c, _ = _mla_attn(qc, kc, vc, None, None, tq_c, tk)
            od_c, _ = _diff_attn(lam_p, subln, dqc, dkc, dvc, None, None, tq_c, tk, lam_init)
            oc_c = _conv(uc, cw, cb, cg, cbeta, tr)
            xc = tail(xc, om_c, oc_c, od_c, gc, ctx_row, b * n_ctx, tmoe_c, False)
        xl = xl_new
    return xl.reshape(b, s, d)
```

```python
import functools
import math

import jax
import jax.numpy as jnp
from jax import lax
from jax.experimental import pallas as pl
from jax.experimental.pallas import tpu as pltpu

F32 = jnp.float32
BF16 = jnp.bfloat16

GRID_W = 64
ROPE_BASE = 10000.0
NORM_EPS = 1e-6

MLA_HEADS = 8
MLA_NOPE = 64
MLA_ROPE = 32
MLA_V = 64
Q_LORA = 256
KV_LORA = 128
CONV_CH = 512
CONV_WIDTH = 31
DIFF_HEADS = 4
DIFF_HD = 64
N_GROUPS = 4
EXPERTS_PER_GROUP = 8
N_EXPERTS = N_GROUPS * EXPERTS_PER_GROUP
EXPERT_FF = 256

LANES = 128
SUBLANES = 8
HALO = 16
VMEM_LIMIT = 48 * 1024 * 1024

C_CQ = 0
C_CKV = C_CQ + Q_LORA
C_KR = C_CKV + KV_LORA
C_GA = C_KR + LANES
C_GG = C_GA + CONV_CH
C_DQ = C_GG + CONV_CH
C_DK = C_DQ + 2 * DIFF_HEADS * DIFF_HD
C_GATES = C_DK + 2 * DIFF_HEADS * DIFF_HD

ONES_ROWS = 16
VT_A = MLA_V + ONES_ROWS
VT_D = 2 * DIFF_HD + ONES_ROWS
LOG2E = 1.4426950408889634
MLA_REF_LANE = MLA_NOPE + MLA_ROPE
GROUP_LANE = N_EXPERTS
MOE_CHUNK = 320
MOE_SMALL_CHUNK = 256
MERGE_PART = 256
MOE_VMEM_LIMIT = 56 * 1024 * 1024
Q_SUB = 256
MLA_HEADS_PER_STEP = 4
DIFF_HEADS_PER_STEP = 2
MLA_LOOKAHEAD = 5
DIFF_LOOKAHEAD = 7


def _params(*sem):
    return pltpu.CompilerParams(dimension_semantics=sem, vmem_limit_bytes=VMEM_LIMIT)


def _sigmoid(x):
    return 0.5 * jnp.tanh(0.5 * x) + 0.5


def _rms(x, g):
    return x * lax.rsqrt(jnp.mean(x * x, axis=-1, keepdims=True) + NORM_EPS) * g


def _dot_nt(a, b):
    return lax.dot_general(a, b, (((1,), (1,)), ((), ())), preferred_element_type=F32)


def _const_spec(shape):
    return pl.BlockSpec(shape, lambda *_: (0,) * len(shape))


def _ada_kernel(c_ref, w_ref, b_ref, o_ref):
    c = c_ref[...]
    h = c * _sigmoid(c)
    o_ref[...] = jnp.dot(h, w_ref[...], preferred_element_type=F32,
                         precision=lax.Precision.HIGHEST) + b_ref[...]


def _ada(cc, w_ada, b_ada):
    depth, d, n = w_ada.shape
    bn = 1536
    rows = cc.shape[0]
    return pl.pallas_call(
        _ada_kernel,
        grid=(depth, n // bn),
        in_specs=[pl.BlockSpec((rows, d), lambda l, j: (0, 0)),
                  pl.BlockSpec((None, d, bn), lambda l, j: (l, 0, j)),
                  pl.BlockSpec((None, 1, bn), lambda l, j: (l, 0, j))],
        out_specs=pl.BlockSpec((None, rows, bn), lambda l, j: (l, 0, j)),
        out_shape=jax.ShapeDtypeStruct((depth, rows, n), F32),
        compiler_params=_params("parallel", "parallel"),
        name="adaln",
    )(cc, w_ada, b_ada.reshape(depth, 1, n))


def _swap_halves(t, half):
    n = t.shape[1]
    lane = lax.broadcasted_iota(jnp.int32, t.shape, 1) % (2 * half)
    return jnp.where(lane < half, pltpu.roll(t, n - half, 1), pltpu.roll(t, half, 1))


def _proj_kernel(*refs, rope, scale_a, scale_d):
    if rope:
        (x_ref, mod_ref, n1g_ref, win_ref, qng_ref, wuq_ref, kvg_ref, wuk_ref, wuvt_ref, wdvt_ref,
         ca_ref, sa_ref, cd_ref, sd_ref,
         q_ref, k_ref, vt_ref, u_ref, dq_ref, dk_ref, dvt_ref, g_ref) = refs
    else:
        (x_ref, mod_ref, n1g_ref, win_ref, qng_ref, wuq_ref, kvg_ref, wuk_ref, wuvt_ref, wdvt_ref,
         q_ref, k_ref, vt_ref, u_ref, dq_ref, dk_ref, dvt_ref, g_ref) = refs

    h = _rms(x_ref[...], n1g_ref[...]) * (1.0 + mod_ref[1:2, :]) + mod_ref[0:1, :]
    hb = h.astype(BF16)

    def proj(a, b):
        return jnp.dot(hb, win_ref[:, a:b], preferred_element_type=F32)

    def rope_a(t):
        return t * ca_ref[...] + _swap_halves(t, MLA_ROPE // 2) * sa_ref[...] if rope else t

    def rope_d(t):
        return t * cd_ref[...] + _swap_halves(t, DIFF_HD // 2) * sd_ref[...] if rope else t

    cq = _rms(proj(C_CQ, C_CKV), qng_ref[...]).astype(BF16)
    ckv = _rms(proj(C_CKV, C_KR), kvg_ref[...]).astype(BF16)
    kr = rope_a(proj(C_KR, C_GA))
    lane = lax.broadcasted_iota(jnp.int32, kr.shape, 1)
    kr = jnp.where(lane == MLA_REF_LANE, 1.0, kr)
    q = jnp.dot(cq, wuq_ref[...], preferred_element_type=F32)
    kn = jnp.dot(ckv, wuk_ref[...], preferred_element_type=F32)
    vt = _dot_nt(wuvt_ref[...], ckv)
    ones = jnp.ones((ONES_ROWS, vt.shape[1]), BF16)
    for hd in range(MLA_HEADS):
        blk = slice(hd * LANES, (hd + 1) * LANES)
        q_ref[:, blk] = (rope_a(q[:, blk]) * scale_a).astype(BF16)
        k_ref[:, blk] = (kn[:, blk] + kr).astype(BF16)
        vt_ref[hd * VT_A:hd * VT_A + MLA_V, :] = vt[hd * MLA_V:(hd + 1) * MLA_V, :].astype(BF16)
        vt_ref[hd * VT_A + MLA_V:(hd + 1) * VT_A, :] = ones

    u_ref[...] = proj(C_GA, C_GG) * _sigmoid(proj(C_GG, C_DQ))

    dq = proj(C_DQ, C_DK)
    dk = proj(C_DK, C_GATES)
    dvt = _dot_nt(wdvt_ref[...], hb)
    for hd in range(DIFF_HEADS):
        blk = slice(hd * LANES, (hd + 1) * LANES)
        qh = rope_d(dq[:, blk]) * scale_d
        kh = rope_d(dk[:, blk])
        low = lane < DIFF_HD
        for mp, (qm, km) in enumerate(((qh, kh), (pltpu.roll(qh, DIFF_HD, 1), pltpu.roll(kh, DIFF_HD, 1)))):
            mblk = slice((2 * hd + mp) * LANES, (2 * hd + mp + 1) * LANES)
            dq_ref[:, mblk] = jnp.where(low, qm, 0.0).astype(BF16)
            dk_ref[:, mblk] = jnp.where(low, km, jnp.where(lane == DIFF_HD, 1.0, 0.0)).astype(BF16)
        dvt_ref[hd * VT_D:hd * VT_D + LANES, :] = dvt[blk, :].astype(BF16)
        dvt_ref[hd * VT_D + LANES:(hd + 1) * VT_D, :] = ones

    d = x_ref.shape[1]
    for j in range(3):
        g_ref[:, j * d:(j + 1) * d] = _sigmoid(proj(C_GATES + j * d, C_GATES + (j + 1) * d)).astype(BF16)


def _proj(x, mod, mod_row, n1g, win, qng, wuq, kvg, wuk, wuvt, wdvt, tables, seq, tm):
    t, d = x.shape
    rope = tables is not None
    tiles_per_seq = seq // tm
    row = lambda i: (i, 0)
    col = lambda i: (0, i)
    if mod_row is None:
        mod_map = lambda i: (i // tiles_per_seq, 0, 0)
    else:
        mod_map = lambda i: (mod_row, 0, 0)
    consts = (n1g, win, qng, wuq, kvg, wuk, wuvt, wdvt)
    in_specs = [pl.BlockSpec((tm, d), row), pl.BlockSpec((None, 6, d), mod_map)]
    in_specs += [_const_spec(a.shape) for a in consts]
    args = [x, mod, *consts]
    if rope:
        in_specs += [pl.BlockSpec((tm, LANES), lambda i: (i % tiles_per_seq, 0))] * 4
        args += list(tables)
    outs = [((t, MLA_HEADS * LANES), (tm, MLA_HEADS * LANES), row, BF16),
            ((t, MLA_HEADS * LANES), (tm, MLA_HEADS * LANES), row, BF16),
            ((MLA_HEADS * VT_A, t), (MLA_HEADS * VT_A, tm), col, BF16),
            ((t, CONV_CH), (tm, CONV_CH), row, F32),
            ((t, 2 * DIFF_HEADS * LANES), (tm, 2 * DIFF_HEADS * LANES), row, BF16),
            ((t, 2 * DIFF_HEADS * LANES), (tm, 2 * DIFF_HEADS * LANES), row, BF16),
            ((DIFF_HEADS * VT_D, t), (DIFF_HEADS * VT_D, tm), col, BF16),
            ((t, 3 * d), (tm, 3 * d), row, BF16)]
    kern = functools.partial(_proj_kernel, rope=rope,
                             scale_a=LOG2E / math.sqrt(MLA_NOPE + MLA_ROPE), scale_d=LOG2E / math.sqrt(DIFF_HD))
    return pl.pallas_call(
        kern,
        grid=(t // tm,),
        in_specs=in_specs,
        out_specs=[pl.BlockSpec(blk, imap) for _, blk, imap, _ in outs],
        out_shape=[jax.ShapeDtypeStruct(shape, dt) for shape, _, _, dt in outs],
        compiler_params=_params("parallel"),
        name="proj_rope" if rope else "proj_ctx",
    )(*args)


def _flash_streams(streams, lookahead, ref_row=None):
    n_chunks = len(streams[0][1])
    items = [(sid, c) for c in range(n_chunks) for sid in range(len(streams))]
    m = [None] * len(streams)
    acc = [None] * len(streams)
    ref = [None] * len(streams)
    if ref_row is not None and n_chunks > 1:
        assert len(streams) > 1
        lookahead = min(lookahead, len(streams) - 1)

    def finish(sid, st, vt):
        cmax = jnp.max(st, axis=0, keepdims=True)
        m_new = cmax if m[sid] is None else jnp.maximum(m[sid], cmax)
        pt = jnp.exp2((st - m_new).astype(BF16))
        pv = jnp.dot(vt, pt, preferred_element_type=F32)
        acc[sid] = pv if m[sid] is None else acc[sid] * jnp.exp2(m[sid] - m_new) + pv
        m[sid] = m_new

    def finish_folded(sid, dt, vt):
        db = dt.astype(BF16)
        cmax = jnp.max(db, axis=0, keepdims=True)
        first = m[sid] is None
        delta = cmax if first else jnp.maximum(cmax, jnp.zeros_like(cmax))
        pt = jnp.exp2(db - delta)
        pv = jnp.dot(vt, pt, preferred_element_type=F32)
        m_new = delta.astype(F32) if first else ref[sid] + delta.astype(F32)
        acc[sid] = pv if first else acc[sid] * jnp.exp2(m[sid] - m_new) + pv
        m[sid] = m_new
        ref[sid] = m_new.astype(BF16).astype(F32)

    qts = [q.astype(F32).T.astype(BF16) for q, _ in streams]
    pending = []
    for sid, c in items:
        k, vt = streams[sid][1][c]
        qt = qts[sid]
        if ref_row is not None and ref[sid] is not None:
            row = lax.broadcasted_iota(jnp.int32, qt.shape, 0)
            qt = jnp.where(row == ref_row, jnp.broadcast_to((-ref[sid]).astype(BF16), qt.shape), qt)
        st = jnp.dot(k, qt, preferred_element_type=F32)
        if len(pending) == lookahead:
            (finish if ref_row is None else finish_folded)(*pending.pop(0))
        pending.append((sid, st, vt))
    for item in pending:
        (finish if ref_row is None else finish_folded)(*item)
    return acc


def _key_chunks(kc_ref, vtc_ref, kl_ref, vtl_ref, tk, kcols, vrows):
    chunks = [(kc_ref[:, kcols], vtc_ref[vrows, :])]
    if kl_ref is not None:
        for j in range(kl_ref.shape[0] // tk):
            keys = slice(j * tk, (j + 1) * tk)
            chunks.append((kl_ref[keys, kcols], vtl_ref[vrows, keys]))
    return chunks


def _cast_riders(cast, grid):
    if not cast:
        return [], [], [], []
    layer, arrs = cast
    steps = math.prod(grid)
    strides = [math.prod(grid[i + 1:]) for i in range(len(grid))]
    step = lambda *ids: sum(i * st for i, st in zip(ids, strides))
    in_specs, out_specs, shapes = [], [], []
    for a in arrs:
        _, e, r, c = a.shape
        if steps >= e:
            per = steps // e
            rows = r // per
            assert steps % e == 0 and r % per == 0 and rows % 16 == 0
            in_specs.append(pl.BlockSpec((None, None, rows, c),
                                         lambda *ids, per=per: (layer, step(*ids) // per, step(*ids) % per, 0)))
            out_specs.append(pl.BlockSpec((None, rows, c),
                                          lambda *ids, per=per: (step(*ids) // per, step(*ids) % per, 0)))
        else:
            assert e % steps == 0
            in_specs.append(pl.BlockSpec((None, e // steps, r, c), lambda *ids: (layer, step(*ids), 0, 0)))
            out_specs.append(pl.BlockSpec((e // steps, r, c), lambda *ids: (step(*ids), 0, 0)))
        shapes.append(jax.ShapeDtypeStruct((e, r, c), BF16))
    return list(arrs), in_specs, out_specs, shapes


def _run_casts(in_refs, out_refs):
    for w_ref, o_ref in zip(in_refs, out_refs):
        o_ref[...] = w_ref[...].astype(BF16)


def _mla_attn_kernel(*refs, latent, tk, n_cast=0):
    if n_cast:
        _run_casts(refs[5:5 + n_cast], refs[6 + n_cast:])
        refs = refs[:5] + refs[5 + n_cast:6 + n_cast]
    if latent:
        q_ref, kc_ref, vtc_ref, kl_ref, vtl_ref, o_ref = refs
    else:
        q_ref, kc_ref, vtc_ref, o_ref = refs
        kl_ref = vtl_ref = None
    subs = [slice(r, r + Q_SUB) for r in range(0, q_ref.shape[0], Q_SUB)]
    heads = q_ref.shape[1] // LANES
    streams = []
    for qs in subs:
        for hh in range(heads):
            cols = slice(hh * LANES, (hh + 1) * LANES)
            rows = slice(hh * VT_A, (hh + 1) * VT_A)
            streams.append((q_ref[qs, cols], _key_chunks(kc_ref, vtc_ref, kl_ref, vtl_ref, tk, cols, rows)))
    accs = _flash_streams(streams, MLA_LOOKAHEAD, ref_row=MLA_REF_LANE)
    outs = [acc[0:MLA_V, :] * (1.0 / acc[MLA_V:MLA_V + 1, :]) for acc in accs]
    for i, qs in enumerate(subs):
        o_ref[qs, :] = jnp.concatenate(outs[heads * i:heads * (i + 1)], axis=0).T.astype(BF16)


def _mla_attn(q, kc, vtc, kl, vtl, tq, tk, cast=()):
    b, lq, _ = q.shape
    c = kc.shape[1]
    latent = kl is not None
    hps = MLA_HEADS_PER_STEP if latent else 2
    grid = (b, MLA_HEADS // hps, lq // tq)
    in_specs = [pl.BlockSpec((None, tq, hps * LANES), lambda i, p, j: (i, j, p)),
                pl.BlockSpec((None, c, hps * LANES), lambda i, p, j: (i, 0, p)),
                pl.BlockSpec((hps * VT_A, c), lambda i, p, j: (p, i))]
    args = [q, kc, vtc]
    if latent:
        s = kl.shape[1]
        in_specs += [pl.BlockSpec((None, s, hps * LANES), lambda i, p, j: (i, 0, p)),
                     pl.BlockSpec((hps * VT_A, s), lambda i, p, j: (p, i))]
        args += [kl, vtl]
    cast_args, cast_in, cast_out, cast_shapes = _cast_riders(cast, grid)
    outs = pl.pallas_call(
        functools.partial(_mla_attn_kernel, latent=latent, tk=tk, n_cast=len(cast_args)),
        grid=grid,
        in_specs=in_specs + cast_in,
        out_specs=[pl.BlockSpec((None, tq, hps * MLA_V), lambda i, p, j: (i, j, p))] + cast_out,
        out_shape=[jax.ShapeDtypeStruct((b, lq, MLA_HEADS * MLA_V), BF16)] + cast_shapes,
        compiler_params=_params("parallel", "parallel", "parallel"),
        name="mla_attn_lat" if latent else "mla_attn_ctx",
    )(*args, *cast_args)
    return outs[0], list(outs[1:])


def _diff_attn_kernel(*refs, latent, tk, lam_init, n_cast=0):
    if n_cast:
        _run_casts(refs[7:7 + n_cast], refs[8 + n_cast:])
        refs = refs[:7] + refs[7 + n_cast:8 + n_cast]
    if latent:
        lam_ref, g_ref, q_ref, kc_ref, vtc_ref, kl_ref, vtl_ref, o_ref = refs
    else:
        lam_ref, g_ref, q_ref, kc_ref, vtc_ref, o_ref = refs
        kl_ref = vtl_ref = None
    subs = [slice(r, r + Q_SUB) for r in range(0, q_ref.shape[0], Q_SUB)]
    heads = q_ref.shape[1] // (2 * LANES)
    streams = []
    for qs in subs:
        for hh in range(heads):
            for mp in range(2):
                cols = slice((2 * hh + mp) * LANES, (2 * hh + mp + 1) * LANES)
                rows = slice(hh * VT_D, (hh + 1) * VT_D)
                streams.append((q_ref[qs, cols], _key_chunks(kc_ref, vtc_ref, kl_ref, vtl_ref, tk, cols, rows)))
    accs = _flash_streams(streams, DIFF_LOOKAHEAD, ref_row=DIFF_HD)
    outs = [acc[0:LANES, :] * (1.0 / acc[LANES:LANES + 1, :]) for acc in accs]
    lam = (jnp.exp(jnp.sum(lam_ref[0:1, :] * lam_ref[1:2, :], axis=-1, keepdims=True))
           - jnp.exp(jnp.sum(lam_ref[2:3, :] * lam_ref[3:4, :], axis=-1, keepdims=True)) + lam_init)
    for i, qs in enumerate(subs):
        for hh in range(heads):
            s0 = 2 * (heads * i + hh)
            o = outs[s0] - lam * outs[s0 + 1]
            o = o * lax.rsqrt(jnp.mean(o * o, axis=0, keepdims=True) + NORM_EPS) * (g_ref[...] * (1.0 - lam_init))
            o_ref[qs, hh * LANES:(hh + 1) * LANES] = o.T.astype(BF16)


def _diff_attn(lam_p, subln_g, q, kc, vtc, kl, vtl, tq, tk, lam_init, cast=()):
    b, lq, _ = q.shape
    c = kc.shape[1]
    latent = kl is not None
    hps = DIFF_HEADS_PER_STEP if latent else 1
    grid = (b, DIFF_HEADS // hps, lq // tq)
    kblk = lambda rows: pl.BlockSpec((None, rows, hps * 2 * LANES), lambda i, h, j: (i, 0, h))
    vblk = lambda keys: pl.BlockSpec((hps * VT_D, keys), lambda i, h, j: (h, i))
    in_specs = [_const_spec(lam_p.shape), _const_spec(subln_g.shape),
                pl.BlockSpec((None, tq, hps * 2 * LANES), lambda i, h, j: (i, j, h)), kblk(c), vblk(c)]
    args = [lam_p, subln_g, q, kc, vtc]
    if latent:
        in_specs += [kblk(kl.shape[1]), vblk(kl.shape[1])]
        args += [kl, vtl]
    cast_args, cast_in, cast_out, cast_shapes = _cast_riders(cast, grid)
    outs = pl.pallas_call(
        functools.partial(_diff_attn_kernel, latent=latent, tk=tk, lam_init=lam_init, n_cast=len(cast_args)),
        grid=grid,
        in_specs=in_specs + cast_in,
        out_specs=[pl.BlockSpec((None, tq, hps * LANES), lambda i, h, j: (i, j, h))] + cast_out,
        out_shape=[jax.ShapeDtypeStruct((b, lq, DIFF_HEADS * LANES), BF16)] + cast_shapes,
        compiler_params=_params("parallel", "parallel", "parallel"),
        name="diff_attn_lat" if latent else "diff_attn_ctx",
    )(*args, *cast_args)
    return outs[0], list(outs[1:])


def _conv_fill(first, last, prev_ref, cur_ref, next_ref, pad_ref, shift_ref):
    tr = cur_ref.shape[0]
    zero = jnp.zeros((HALO, cur_ref.shape[1]), F32)
    pad_ref[0:HALO, :] = jnp.where(first, zero, prev_ref[...])
    pad_ref[HALO:HALO + tr, :] = cur_ref[...]
    pad_ref[HALO + tr:, :] = jnp.where(last, zero, next_ref[...])
    span = shift_ref.shape[1]
    for phase in range(1, SUBLANES):
        shift_ref[phase] = pad_ref[phase:phase + span, :]


def _conv_rows(pad_ref, shift_ref, w_ref, b_ref, g_ref, beta_ref, r0, n):
    off = HALO - CONV_WIDTH // 2
    acc = jnp.zeros((n, pad_ref.shape[1]), F32) + b_ref[...]
    for tap in range(CONV_WIDTH):
        phase = (off + tap) % SUBLANES
        base = r0 + off + tap - phase
        win = pad_ref[base:base + n, :] if phase == 0 else shift_ref[phase, base:base + n, :]
        acc = acc + win * w_ref[tap:tap + 1, :]
    mu = jnp.mean(acc, axis=-1, keepdims=True)
    cen = acc - mu
    y = cen * lax.rsqrt(jnp.mean(cen * cen, axis=-1, keepdims=True) + NORM_EPS) * g_ref[...] + beta_ref[...]
    return (y * _sigmoid(y)).astype(BF16)


def _conv_kernel(prev_ref, cur_ref, next_ref, w_ref, b_ref, g_ref, beta_ref, o_ref, pad_ref, shift_ref):
    j = pl.program_id(1)
    _conv_fill(j == 0, j == pl.num_programs(1) - 1, prev_ref, cur_ref, next_ref, pad_ref, shift_ref)
    o_ref[...] = _conv_rows(pad_ref, shift_ref, w_ref, b_ref, g_ref, beta_ref, 0, cur_ref.shape[0])


def _conv(u, w, bias, g, beta, tr):
    b, l, ch = u.shape
    per = tr // HALO
    last = l // HALO - 1
    return pl.pallas_call(
        _conv_kernel,
        grid=(b, l // tr),
        in_specs=[pl.BlockSpec((None, HALO, ch), lambda i, j: (i, jnp.maximum(j * per - 1, 0), 0)),
                  pl.BlockSpec((None, tr, ch), lambda i, j: (i, j, 0)),
                  pl.BlockSpec((None, HALO, ch), lambda i, j: (i, jnp.minimum((j + 1) * per, last), 0)),
                  _const_spec(w.shape), _const_spec(bias.shape), _const_spec(g.shape), _const_spec(beta.shape)],
        out_specs=pl.BlockSpec((None, tr, ch), lambda i, j: (i, j, 0)),
        out_shape=jax.ShapeDtypeStruct((b, l, ch), BF16),
        scratch_shapes=[pltpu.VMEM((tr + 2 * HALO, ch), F32),
                        pltpu.VMEM((SUBLANES, tr + 2 * HALO - SUBLANES, ch), F32)],
        compiler_params=_params("parallel", "parallel"),
        name="conv_ln_silu",
    )(u, u, u, w, bias, g, beta)


def _route(logits):
    lane = lax.broadcasted_iota(jnp.int32, logits.shape, 1)
    neg = jnp.float32(-jnp.inf)
    big = jnp.int32(LANES)
    is_grp = (lane >= N_EXPERTS) & (lane < N_EXPERTS + N_GROUPS)
    gl = jnp.where(is_grp, logits, neg)
    gmax = jnp.max(gl, axis=-1, keepdims=True)
    gsel = jnp.min(jnp.where(gl == gmax, lane, big), axis=-1, keepdims=True) - N_EXPERTS
    p_g = 1.0 / jnp.sum(jnp.exp(gl - gmax), axis=-1, keepdims=True)
    in_grp = (lane >= gsel * EXPERTS_PER_GROUP) & (lane < (gsel + 1) * EXPERTS_PER_GROUP)
    e1 = jnp.where(in_grp, logits, neg)
    v1 = jnp.max(e1, axis=-1, keepdims=True)
    i1 = jnp.min(jnp.where(e1 == v1, lane, big), axis=-1, keepdims=True)
    e2 = jnp.where(lane == i1, neg, e1)
    v2 = jnp.max(e2, axis=-1, keepdims=True)
    i2 = jnp.min(jnp.where(e2 == v2, lane, big), axis=-1, keepdims=True)
    r = jnp.exp(v2 - v1)
    w1 = p_g / (1.0 + r)
    comb = jnp.where(lane == i1, w1, jnp.where(lane == i2, w1 * r, 0.0))
    return jnp.where(lane == GROUP_LANE, gsel.astype(F32), comb)


def _merge_kernel(*refs, tiles_per_seq):
    if tiles_per_seq is None:
        (x_ref, om_ref, oc_ref, od_ref, gates_ref, mod_ref, wm_ref, wc_ref, wd_ref, wout_ref,
         n2g_ref, wr_ref, br_ref, xn_ref, h2_ref, comb_ref) = refs
        conv_branch = lambda rows: oc_ref[rows, :]
    else:
        (x_ref, om_ref, up_ref, uc_ref, un_ref, cw_ref, cb_ref, cg_ref, cbeta_ref, od_ref, gates_ref, mod_ref,
         wm_ref, wc_ref, wd_ref, wout_ref, n2g_ref, wr_ref, br_ref, xn_ref, h2_ref, comb_ref,
         pad_ref, shift_ref) = refs
        ts = pl.program_id(0) % tiles_per_seq
        _conv_fill(ts == 0, ts == tiles_per_seq - 1, up_ref, uc_ref, un_ref, pad_ref, shift_ref)
        conv_branch = lambda rows: _conv_rows(pad_ref, shift_ref, cw_ref, cb_ref, cg_ref, cbeta_ref,
                                              rows.start, rows.stop - rows.start)
    tm, d = x_ref.shape
    parts = [slice(r, r + MERGE_PART) for r in range(0, tm, MERGE_PART)]

    def branch_sum(rows):
        gate = lambda j: gates_ref[rows, j * d:(j + 1) * d].astype(F32)
        return (gate(0) * jnp.dot(om_ref[rows, :], wm_ref[...], preferred_element_type=F32)
                + gate(1) * jnp.dot(conv_branch(rows), wc_ref[...], preferred_element_type=F32)
                + gate(2) * jnp.dot(od_ref[rows, :], wd_ref[...], preferred_element_type=F32))

    def residual_norm(rows, y):
        z = jnp.dot(y.astype(BF16), wout_ref[...], preferred_element_type=F32)
        xn = x_ref[rows, :] + mod_ref[2:3, :] * z
        xn_ref[rows, :] = xn
        h2 = _rms(xn, n2g_ref[...]) * (1.0 + mod_ref[4:5, :]) + mod_ref[3:4, :]
        h_hi = h2.astype(BF16)
        h2_ref[rows, :] = h_hi
        return h_hi, (h2 - h_hi.astype(F32)).astype(BF16)

    def router(rows, h_hi, h_lo):
        hw = jnp.dot(h_hi, wr_ref[...], preferred_element_type=F32)
        lw = jnp.dot(h_lo, wr_ref[:, 0:LANES], preferred_element_type=F32)
        comb_ref[rows, :] = _route(hw[:, 0:LANES] + hw[:, LANES:] + lw + br_ref[...])

    ys = [branch_sum(rows) for rows in parts]
    hs = [residual_norm(rows, y) for rows, y in zip(parts, ys)]
    for rows, (h_hi, h_lo) in zip(parts, hs):
        router(rows, h_hi, h_lo)


def _merge(x, om, oc, od, gates, mod, mod_row, wm, wc, wd, wout, n2g, wr, br, seq, tm, conv=None):
    t, d = x.shape
    tiles_per_seq = seq // tm
    row = lambda i: (i, 0)
    if mod_row is None:
        mod_map = lambda i: (i // tiles_per_seq, 0, 0)
    else:
        mod_map = lambda i: (mod_row, 0, 0)
    rows = lambda a: pl.BlockSpec((tm, a.shape[1]), row)
    scratch = []
    if conv is None:
        conv_specs, conv_args = [rows(oc)], [oc]
    else:
        ch = oc.shape[1]
        per = tm // HALO
        last = t // HALO - 1
        conv_specs = [pl.BlockSpec((HALO, ch), lambda i: (jnp.maximum(i * per - 1, 0), 0)), rows(oc),
                      pl.BlockSpec((HALO, ch), lambda i: (jnp.minimum((i + 1) * per, last), 0))]
        conv_specs += [_const_spec(a.shape) for a in conv]
        conv_args = [oc, oc, oc, *conv]
        scratch = [pltpu.VMEM((tm + 2 * HALO, ch), F32),
                   pltpu.VMEM((SUBLANES, tm + 2 * HALO - SUBLANES, ch), F32)]
    return pl.pallas_call(
        functools.partial(_merge_kernel, tiles_per_seq=None if conv is None else tiles_per_seq),
        grid=(t // tm,),
        in_specs=[rows(x), rows(om), *conv_specs, rows(od), rows(gates),
                  pl.BlockSpec((None, 6, d), mod_map)]
                 + [_const_spec(a.shape) for a in (wm, wc, wd, wout, n2g, wr, br)],
        out_specs=[pl.BlockSpec((tm, d), row), pl.BlockSpec((tm, d), row), pl.BlockSpec((tm, LANES), row)],
        out_shape=[jax.ShapeDtypeStruct((t, d), F32), jax.ShapeDtypeStruct((t, d), BF16),
                   jax.ShapeDtypeStruct((t, LANES), F32)],
        scratch_shapes=scratch,
        compiler_params=_params("parallel"),
        name="merge_route",
    )(x, om, *conv_args, od, gates, mod, wm, wc, wd, wout, n2g, wr, br)


def _moe_kernel(h_ref, comb_ref, xn_ref, mod_ref, tri_ref, wg_ref, wu_ref, wd_ref, fg_ref, o_ref,
                pos_ref, comb3_ref, *, final, mod_rows):
    g = pl.program_id(1)
    slabs, slab_rows, d = o_ref.shape
    tm = slabs * slab_rows

    @pl.when(g == 0)
    def _():
        o_ref[...] = jnp.zeros_like(o_ref)
        comb = comb_ref[...].reshape(tm, LANES)
        hi = comb.astype(BF16)
        r1 = comb - hi.astype(F32)
        mid = r1.astype(BF16)
        lo = (r1 - mid.astype(F32)).astype(BF16)
        comb3_ref[...] = jnp.concatenate([hi, mid, lo], axis=1)
        grow = comb.T[GROUP_LANE:GROUP_LANE + 1, :]
        gid = lax.broadcasted_iota(jnp.int32, (8, tm), 0).astype(F32)
        member = grow == gid
        rank = jnp.dot(jnp.where(member, 1.0, 0.0).astype(BF16), tri_ref[...], preferred_element_type=F32)
        pos_ref[...] = jnp.where(member, rank, -1.0)

    row = lax.broadcasted_iota(jnp.int32, (8, tm), 0)
    posg = jnp.sum(jnp.where(row == g, pos_ref[...], 0.0), axis=0, keepdims=True)
    count = jnp.max(posg).astype(jnp.int32) + 1

    def chunk(c, rows):
        slot = (lax.broadcasted_iota(jnp.int32, (rows, tm), 0) + c * rows).astype(F32)
        sel = jnp.where(posg == slot, 1.0, 0.0).astype(BF16)
        h = h_ref[...].reshape(tm, d)
        x = jnp.dot(sel, h, preferred_element_type=F32).astype(BF16)
        cw3 = jnp.dot(sel, comb3_ref[...], preferred_element_type=F32)
        cw = cw3[:, 0:LANES] + cw3[:, LANES:2 * LANES] + cw3[:, 2 * LANES:]
        lane = lax.broadcasted_iota(jnp.int32, cw.shape, 1)
        y = jnp.zeros((rows, d), F32)
        for e in range(EXPERTS_PER_GROUP):
            a = jnp.dot(x, wg_ref[e], preferred_element_type=F32)
            u = jnp.dot(x, wu_ref[e], preferred_element_type=F32)
            w = jnp.sum(jnp.where(lane == g * EXPERTS_PER_GROUP + e, cw, 0.0), axis=1, keepdims=True)
            hid = a * _sigmoid(a) * u * w
            y = y + jnp.dot(hid.astype(BF16), wd_ref[e], preferred_element_type=F32)
        back = lax.dot_general(sel, y.astype(BF16), (((0,), (0,)), ((), ())), preferred_element_type=F32)
        o_ref[...] += back.reshape(slabs, slab_rows, d)

    @pl.when((count > 0) & (count <= MOE_SMALL_CHUNK))
    def _():
        chunk(0, MOE_SMALL_CHUNK)

    @pl.when(count > MOE_SMALL_CHUNK)
    def _():
        def body(c, carry):
            chunk(c, MOE_CHUNK)
            return carry
        lax.fori_loop(0, (count + MOE_CHUNK - 1) // MOE_CHUNK, body, 0)

    @pl.when(g == pl.num_programs(1) - 1)
    def _():
        for sl, mod_row in enumerate(mod_rows):
            out = xn_ref[sl] + mod_ref[mod_row, 5:6, :] * o_ref[sl]
            o_ref[sl] = _rms(out, fg_ref[...]) if final else out


def _moe(h2, comb, xn, mod, mod_rows, wg, wu, wd, final_g, tm, final):
    t, d = xn.shape
    slabs = len(mod_rows)
    slab_rows = tm // slabs
    n_tiles = t // tm
    view = lambda a: a.reshape(slabs, n_tiles, slab_rows, a.shape[-1])
    tile = lambda a: pl.BlockSpec((slabs, None, slab_rows, a.shape[-1]), lambda i, g: (0, i, 0, 0))
    ff = wg.shape[2]
    idx = jnp.arange(tm)
    tri = (idx[:, None] < idx[None, :]).astype(BF16)
    grp = lambda i, g: (g, 0, 0)
    once = pl.Buffered(1)
    xn_spec = pl.BlockSpec((slabs, None, slab_rows, d), lambda i, g: (0, i, 0, 0))
    out = pl.pallas_call(
        functools.partial(_moe_kernel, final=final, mod_rows=tuple(mod_rows)),
        grid=(n_tiles, N_GROUPS),
        in_specs=[tile(h2), tile(comb), xn_spec,
                  _const_spec(mod.shape),
                  pl.BlockSpec((tm, tm), lambda i, g: (0, 0), pipeline_mode=once),
                  pl.BlockSpec((EXPERTS_PER_GROUP, d, ff), grp),
                  pl.BlockSpec((EXPERTS_PER_GROUP, d, ff), grp),
                  pl.BlockSpec((EXPERTS_PER_GROUP, ff, d), grp),
                  pl.BlockSpec(final_g.shape, lambda i, g: (0, 0))],
        out_specs=tile(xn),
        out_shape=jax.ShapeDtypeStruct((slabs, n_tiles, slab_rows, d), F32),
        scratch_shapes=[pltpu.VMEM((8, tm), F32), pltpu.VMEM((tm, 3 * LANES), BF16)],
        compiler_params=pltpu.CompilerParams(dimension_semantics=("parallel", "arbitrary"),
                                             vmem_limit_bytes=MOE_VMEM_LIMIT),
        name="moe",
    )(view(h2), view(comb), view(xn), mod, tri, wg, wu, wd, final_g)
    return out.reshape(t, d)


def _rope_tables(s):
    rows = s // GRID_W
    row = jnp.repeat(jnp.arange(rows, dtype=F32), GRID_W)
    col = jnp.tile(jnp.arange(GRID_W, dtype=F32), rows)

    def cos_sin(rot_dim):
        n = rot_dim // 4
        inv = ROPE_BASE ** (-jnp.arange(n, dtype=F32) / n)
        ang = jnp.concatenate([row[:, None] * inv, col[:, None] * inv], axis=-1)
        return jnp.cos(ang), jnp.sin(ang)

    ca, sa = cos_sin(MLA_ROPE)
    one = jnp.ones((s, MLA_NOPE), F32)
    pad1 = jnp.ones((s, LANES - MLA_NOPE - MLA_ROPE), F32)
    cos_a = jnp.concatenate([one, ca, ca, pad1], axis=-1)
    sin_a = jnp.concatenate([0 * one, -sa, sa, 0 * pad1], axis=-1)
    cd, sd = cos_sin(DIFF_HD)
    cos_d = jnp.concatenate([cd, cd, cd, cd], axis=-1)
    sin_d = jnp.concatenate([-sd, sd, -sd, sd], axis=-1)
    return cos_a, sin_a, cos_d, sin_d


def _layout_w_in(w_in):
    d = w_in.shape[0]
    o = 0
    parts = {}
    for name, width in (("cq", Q_LORA), ("ckv", KV_LORA), ("kr", MLA_ROPE), ("glu", 2 * CONV_CH),
                        ("dq", 2 * DIFF_HEADS * DIFF_HD), ("dk", 2 * DIFF_HEADS * DIFF_HD),
                        ("dv", 2 * DIFF_HEADS * DIFF_HD), ("gates", 3 * d)):
        parts[name] = w_in[:, o:o + width]
        o += width
    kr_blk = jnp.concatenate([jnp.zeros((d, MLA_NOPE), w_in.dtype), parts["kr"],
                              jnp.zeros((d, LANES - MLA_NOPE - MLA_ROPE), w_in.dtype)], axis=1)
    win = jnp.concatenate([parts["cq"], parts["ckv"], kr_blk, parts["glu"], parts["dq"], parts["dk"],
                           parts["gates"]], axis=1)
    return win.astype(BF16), parts["dv"].T.astype(BF16)


def _layout_w_uq(w_uq):
    r = w_uq.shape[0]
    w = w_uq.reshape(r, MLA_HEADS, MLA_NOPE + MLA_ROPE)
    w = jnp.pad(w, ((0, 0), (0, 0), (0, LANES - MLA_NOPE - MLA_ROPE)))
    return w.reshape(r, MLA_HEADS * LANES).astype(BF16)


def _layout_w_ukv(w_ukv):
    r = w_ukv.shape[0]
    w = w_ukv.reshape(r, MLA_HEADS, MLA_NOPE + MLA_V)
    wk = jnp.pad(w[:, :, :MLA_NOPE], ((0, 0), (0, 0), (0, LANES - MLA_NOPE))).reshape(r, MLA_HEADS * LANES)
    wvt = w[:, :, MLA_NOPE:].reshape(r, MLA_HEADS * MLA_V).T
    return wk.astype(BF16), wvt.astype(BF16)


def _layout_router(w_rg, b_rg, w_re, b_re):
    d = w_rg.shape[0]
    w = jnp.concatenate([w_re, w_rg, jnp.zeros((d, LANES - N_EXPERTS - N_GROUPS), F32)], axis=1)
    hi = w.astype(BF16)
    lo = (w - hi.astype(F32)).astype(BF16)
    b = jnp.concatenate([b_re, b_rg, jnp.zeros((LANES - N_EXPERTS - N_GROUPS,), F32)])[None, :]
    return jnp.concatenate([hi, lo], axis=1), b


def kernel(x, c, ctx, c_ctx, w_ada, b_ada, norm1_g, w_in, q_norm_g, w_uq, kv_norm_g, w_ukv, conv_w, conv_b,
           conv_ln_g, conv_ln_b, lam_q1, lam_k1, lam_q2, lam_k2, diff_subln_g, w_o_mla, w_o_conv, w_o_diff,
           w_out, norm2_g, w_rg, b_rg, w_re, b_re, w_gate, w_up, w_down, final_g):
    b, s, d = x.shape
    n_ctx = ctx.shape[1]
    depth = w_ada.shape[0]
    tm = 512
    tq = min(4 * Q_SUB, s)
    tq_c = min(tq, n_ctx)
    tk = min(256, s)
    tr = 256
    tmoe_l = min(1024, b * s)
    tmoe_c = min(1024, b * n_ctx)
    ctx_row = b

    rows = -(-(b + 1) // 8) * 8
    cc = jnp.zeros((rows, d), F32).at[:b].set(c).at[b].set(c_ctx)
    mod_all = _ada(cc, w_ada, b_ada).reshape(depth, rows, 6, d)
    tables = _rope_tables(s)
    row2 = lambda a: a[None, :]

    xl = x.reshape(b * s, d)
    xc = ctx.reshape(b * n_ctx, d)
    for l in range(depth):
        update_ctx = l < depth - 1
        final = l == depth - 1
        lam_init = 0.8 - 0.6 * math.exp(-0.3 * l)
        mod = mod_all[l]
        win, wdvt = _layout_w_in(w_in[l])
        wuq = _layout_w_uq(w_uq[l])
        wuk, wuvt = _layout_w_ukv(w_ukv[l])
        n1g, qng, kvg = row2(norm1_g[l]), row2(q_norm_g[l]), row2(kv_norm_g[l])
        lam_p = jnp.stack([lam_q1[l], lam_k1[l], lam_q2[l], lam_k2[l]])
        subln = diff_subln_g[l][:, None]

        pc = _proj(xc, mod, ctx_row, n1g, win, qng, wuq, kvg, wuk, wuvt, wdvt, None, n_ctx, tm)
        pl_ = _proj(xl, mod, None, n1g, win, qng, wuq, kvg, wuk, wuvt, wdvt, tables, s, tm)
        per_batch = lambda arrs, n: [a if i in (2, 6) else a.reshape(b, n, -1) for i, a in enumerate(arrs)]
        qc, kc, vc, uc, dqc, dkc, dvc, gc = per_batch(pc, n_ctx)
        ql, kl, vl, ul, dql, dkl, dvl, gl = per_batch(pl_, s)

        wm, wc, wd = w_o_mla[l].astype(BF16), w_o_conv[l].astype(BF16), w_o_diff[l].astype(BF16)
        wout = w_out[l].astype(BF16)
        wr, br = _layout_router(w_rg[l], b_rg[l], w_re[l], b_re[l])
        n2g = row2(norm2_g[l])
        cw, cb = conv_w[l], row2(conv_b[l])
        cg, cbeta = row2(conv_ln_g[l]), row2(conv_ln_b[l])
        fg = row2(final_g)

        def tail(xs, om, oc, od, gates, mod_row, seq, tmoe, is_final, conv=None):
            flat = lambda a: a.reshape(-1, a.shape[-1])
            xn, h2, comb = _merge(xs, flat(om), flat(oc), flat(od), flat(gates), mod, mod_row,
                                  wm, wc, wd, wout, n2g, wr, br, seq, tm, conv=conv)
            moe_rows = tuple(range(b)) if mod_row is None else (mod_row,)
            return _moe(h2, comb, xn, mod, moe_rows, wg, wu, wdn, fg, tmoe, is_final)

        om_l, (wg, wu) = _mla_attn(ql, kc, vc, kl, vl, tq, tk, cast=(l, [w_gate, w_up]))
        od_l, (wdn,) = _diff_attn(lam_p, subln, dql, dkc, dvc, dkl, dvl, tq, tk, lam_init, cast=(l, [w_down]))
        xl_new = tail(xl, om_l, ul, od_l, gl, None, s, tmoe_l, final, conv=(cw, cb, cg, cbeta))
        if update_ctx:
            om_c, _ = _mla_attn(qc, kc, vc, None, None, tq_c, tk)
            od_c, _ = _diff_attn(lam_p, subln, dqc, dkc, dvc, None, None, tq_c, tk, lam_init)
            oc_c = _conv(uc, cw, cb, cg, cbeta, tr)
            xc = tail(xc, om_c, oc_c, od_c, gc, ctx_row, b * n_ctx, tmoe_c, False)
        xl = xl_new
    return xl.reshape(b, s, d)
```

```python
import functools
import math

import jax
import jax.numpy as jnp
from jax import lax
from jax.experimental import pallas as pl
from jax.experimental.pallas import tpu as pltpu

F32 = jnp.float32
BF16 = jnp.bfloat16

GRID_W = 64
ROPE_BASE = 10000.0
NORM_EPS = 1e-6

MLA_HEADS = 8
MLA_NOPE = 64
MLA_ROPE = 32
MLA_V = 64
Q_LORA = 256
KV_LORA = 128
CONV_CH = 512
CONV_WIDTH = 31
DIFF_HEADS = 4
DIFF_HD = 64
N_GROUPS = 4
EXPERTS_PER_GROUP = 8
N_EXPERTS = N_GROUPS * EXPERTS_PER_GROUP
EXPERT_FF = 256

LANES = 128
SUBLANES = 8
HALO = 16
VMEM_LIMIT = 48 * 1024 * 1024

C_CQ = 0
C_CKV = C_CQ + Q_LORA
C_KR = C_CKV + KV_LORA
C_GA = C_KR + LANES
C_GG = C_GA + CONV_CH
C_DQ = C_GG + CONV_CH
C_DK = C_DQ + 2 * DIFF_HEADS * DIFF_HD
C_GATES = C_DK + 2 * DIFF_HEADS * DIFF_HD

ONES_ROWS = 16
VT_A = MLA_V + ONES_ROWS
VT_D = 2 * DIFF_HD + ONES_ROWS
LOG2E = 1.4426950408889634
MLA_REF_LANE = MLA_NOPE + MLA_ROPE
GROUP_LANE = N_EXPERTS
MOE_CHUNK = 320
MOE_SMALL_CHUNK = 256
MERGE_PART = 256
MOE_VMEM_LIMIT = 56 * 1024 * 1024
Q_SUB = 256
MLA_HEADS_PER_STEP = 4
DIFF_HEADS_PER_STEP = 2
MLA_LOOKAHEAD = 7
DIFF_LOOKAHEAD = 7


def _params(*sem):
    return pltpu.CompilerParams(dimension_semantics=sem, vmem_limit_bytes=VMEM_LIMIT)


def _sigmoid(x):
    return 0.5 * jnp.tanh(0.5 * x) + 0.5


def _rms(x, g):
    return x * lax.rsqrt(jnp.mean(x * x, axis=-1, keepdims=True) + NORM_EPS) * g


def _dot_nt(a, b):
    return lax.dot_general(a, b, (((1,), (1,)), ((), ())), preferred_element_type=F32)


def _const_spec(shape):
    return pl.BlockSpec(shape, lambda *_: (0,) * len(shape))


def _ada_kernel(c_ref, w_ref, b_ref, o_ref):
    c = c_ref[...]
    h = c * _sigmoid(c)
    o_ref[...] = jnp.dot(h, w_ref[...], preferred_element_type=F32,
                         precision=lax.Precision.HIGHEST) + b_ref[...]


def _ada(cc, w_ada, b_ada):
    depth, d, n = w_ada.shape
    bn = 1536
    rows = cc.shape[0]
    return pl.pallas_call(
        _ada_kernel,
        grid=(depth, n // bn),
        in_specs=[pl.BlockSpec((rows, d), lambda l, j: (0, 0)),
                  pl.BlockSpec((None, d, bn), lambda l, j: (l, 0, j)),
                  pl.BlockSpec((None, 1, bn), lambda l, j: (l, 0, j))],
        out_specs=pl.BlockSpec((None, rows, bn), lambda l, j: (l, 0, j)),
        out_shape=jax.ShapeDtypeStruct((depth, rows, n), F32),
        compiler_params=_params("parallel", "parallel"),
        name="adaln",
    )(cc, w_ada, b_ada.reshape(depth, 1, n))


def _swap_halves(t, half):
    n = t.shape[1]
    lane = lax.broadcasted_iota(jnp.int32, t.shape, 1) % (2 * half)
    return jnp.where(lane < half, pltpu.roll(t, n - half, 1), pltpu.roll(t, half, 1))


def _proj_kernel(*refs, rope, scale_a, scale_d):
    if rope:
        (x_ref, mod_ref, n1g_ref, win_ref, qng_ref, wuq_ref, kvg_ref, wuk_ref, wuvt_ref, wdvt_ref,
         ca_ref, sa_ref, cd_ref, sd_ref,
         q_ref, k_ref, vt_ref, u_ref, dq_ref, dk_ref, dvt_ref, g_ref) = refs
    else:
        (x_ref, mod_ref, n1g_ref, win_ref, qng_ref, wuq_ref, kvg_ref, wuk_ref, wuvt_ref, wdvt_ref,
         q_ref, k_ref, vt_ref, u_ref, dq_ref, dk_ref, dvt_ref, g_ref) = refs

    h = _rms(x_ref[...], n1g_ref[...]) * (1.0 + mod_ref[1:2, :]) + mod_ref[0:1, :]
    hb = h.astype(BF16)

    def proj(a, b):
        return jnp.dot(hb, win_ref[:, a:b], preferred_element_type=F32)

    def rope_a(t):
        return t * ca_ref[...] + _swap_halves(t, MLA_ROPE // 2) * sa_ref[...] if rope else t

    def rope_d(t):
        return t * cd_ref[...] + _swap_halves(t, DIFF_HD // 2) * sd_ref[...] if rope else t

    cq = _rms(proj(C_CQ, C_CKV), qng_ref[...]).astype(BF16)
    ckv = _rms(proj(C_CKV, C_KR), kvg_ref[...]).astype(BF16)
    kr = rope_a(proj(C_KR, C_GA))
    lane = lax.broadcasted_iota(jnp.int32, kr.shape, 1)
    kr = jnp.where(lane == MLA_REF_LANE, 1.0, kr)
    q = jnp.dot(cq, wuq_ref[...], preferred_element_type=F32)
    kn = jnp.dot(ckv, wuk_ref[...], preferred_element_type=F32)
    vt = _dot_nt(wuvt_ref[...], ckv)
    ones = jnp.ones((ONES_ROWS, vt.shape[1]), BF16)
    for hd in range(MLA_HEADS):
        blk = slice(hd * LANES, (hd + 1) * LANES)
        q_ref[:, blk] = (rope_a(q[:, blk]) * scale_a).astype(BF16)
        k_ref[:, blk] = (kn[:, blk] + kr).astype(BF16)
        vt_ref[hd * VT_A:hd * VT_A + MLA_V, :] = vt[hd * MLA_V:(hd + 1) * MLA_V, :].astype(BF16)
        vt_ref[hd * VT_A + MLA_V:(hd + 1) * VT_A, :] = ones

    u_ref[...] = proj(C_GA, C_GG) * _sigmoid(proj(C_GG, C_DQ))

    dq = proj(C_DQ, C_DK)
    dk = proj(C_DK, C_GATES)
    dvt = _dot_nt(wdvt_ref[...], hb)
    for hd in range(DIFF_HEADS):
        blk = slice(hd * LANES, (hd + 1) * LANES)
        qh = rope_d(dq[:, blk]) * scale_d
        kh = rope_d(dk[:, blk])
        low = lane < DIFF_HD
        for mp, (qm, km) in enumerate(((qh, kh), (pltpu.roll(qh, DIFF_HD, 1), pltpu.roll(kh, DIFF_HD, 1)))):
            mblk = slice((2 * hd + mp) * LANES, (2 * hd + mp + 1) * LANES)
            dq_ref[:, mblk] = jnp.where(low, qm, 0.0).astype(BF16)
            dk_ref[:, mblk] = jnp.where(low, km, jnp.where(lane == DIFF_HD, 1.0, 0.0)).astype(BF16)
        dvt_ref[hd * VT_D:hd * VT_D + LANES, :] = dvt[blk, :].astype(BF16)
        dvt_ref[hd * VT_D + LANES:(hd + 1) * VT_D, :] = ones

    d = x_ref.shape[1]
    for j in range(3):
        g_ref[:, j * d:(j + 1) * d] = _sigmoid(proj(C_GATES + j * d, C_GATES + (j + 1) * d)).astype(BF16)


def _proj(x, mod, mod_row, n1g, win, qng, wuq, kvg, wuk, wuvt, wdvt, tables, seq, tm):
    t, d = x.shape
    rope = tables is not None
    tiles_per_seq = seq // tm
    row = lambda i: (i, 0)
    col = lambda i: (0, i)
    if mod_row is None:
        mod_map = lambda i: (i // tiles_per_seq, 0, 0)
    else:
        mod_map = lambda i: (mod_row, 0, 0)
    consts = (n1g, win, qng, wuq, kvg, wuk, wuvt, wdvt)
    in_specs = [pl.BlockSpec((tm, d), row), pl.BlockSpec((None, 6, d), mod_map)]
    in_specs += [_const_spec(a.shape) for a in consts]
    args = [x, mod, *consts]
    if rope:
        in_specs += [pl.BlockSpec((tm, LANES), lambda i: (i % tiles_per_seq, 0))] * 4
        args += list(tables)
    outs = [((t, MLA_HEADS * LANES), (tm, MLA_HEADS * LANES), row, BF16),
            ((t, MLA_HEADS * LANES), (tm, MLA_HEADS * LANES), row, BF16),
            ((MLA_HEADS * VT_A, t), (MLA_HEADS * VT_A, tm), col, BF16),
            ((t, CONV_CH), (tm, CONV_CH), row, F32),
            ((t, 2 * DIFF_HEADS * LANES), (tm, 2 * DIFF_HEADS * LANES), row, BF16),
            ((t, 2 * DIFF_HEADS * LANES), (tm, 2 * DIFF_HEADS * LANES), row, BF16),
            ((DIFF_HEADS * VT_D, t), (DIFF_HEADS * VT_D, tm), col, BF16),
            ((t, 3 * d), (tm, 3 * d), row, BF16)]
    kern = functools.partial(_proj_kernel, rope=rope,
                             scale_a=LOG2E / math.sqrt(MLA_NOPE + MLA_ROPE), scale_d=LOG2E / math.sqrt(DIFF_HD))
    return pl.pallas_call(
        kern,
        grid=(t // tm,),
        in_specs=in_specs,
        out_specs=[pl.BlockSpec(blk, imap) for _, blk, imap, _ in outs],
        out_shape=[jax.ShapeDtypeStruct(shape, dt) for shape, _, _, dt in outs],
        compiler_params=_params("parallel"),
        name="proj_rope" if rope else "proj_ctx",
    )(*args)


def _flash_streams(streams, lookahead, ref_row=None):
    n_chunks = len(streams[0][1])
    items = [(sid, c) for c in range(n_chunks) for sid in range(len(streams))]
    m = [None] * len(streams)
    acc = [None] * len(streams)
    ref = [None] * len(streams)
    if ref_row is not None and n_chunks > 1:
        assert len(streams) > 1
        lookahead = min(lookahead, len(streams) - 1)

    def finish(sid, st, vt):
        cmax = jnp.max(st, axis=0, keepdims=True)
        m_new = cmax if m[sid] is None else jnp.maximum(m[sid], cmax)
        pt = jnp.exp2((st - m_new).astype(BF16))
        pv = jnp.dot(vt, pt, preferred_element_type=F32)
        acc[sid] = pv if m[sid] is None else acc[sid] * jnp.exp2(m[sid] - m_new) + pv
        m[sid] = m_new

    def finish_folded(sid, dt, vt):
        db = dt.astype(BF16)
        cmax = jnp.max(db, axis=0, keepdims=True)
        first = m[sid] is None
        delta = cmax if first else jnp.maximum(cmax, jnp.zeros_like(cmax))
        pt = jnp.exp2(db - delta)
        pv = jnp.dot(vt, pt, preferred_element_type=F32)
        m_new = delta.astype(F32) if first else ref[sid] + delta.astype(F32)
        acc[sid] = pv if first else acc[sid] * jnp.exp2(m[sid] - m_new) + pv
        m[sid] = m_new
        ref[sid] = m_new.astype(BF16).astype(F32)

    qts = [q.astype(F32).T.astype(BF16) for q, _ in streams]
    pending = []
    for sid, c in items:
        k, vt = streams[sid][1][c]
        qt = qts[sid]
        if ref_row is not None and ref[sid] is not None:
            row = lax.broadcasted_iota(jnp.int32, qt.shape, 0)
            qt = jnp.where(row == ref_row, jnp.broadcast_to((-ref[sid]).astype(BF16), qt.shape), qt)
        st = jnp.dot(k, qt, preferred_element_type=F32)
        if len(pending) == lookahead:
            (finish if ref_row is None else finish_folded)(*pending.pop(0))
        pending.append((sid, st, vt))
    for item in pending:
        (finish if ref_row is None else finish_folded)(*item)
    return acc


def _key_chunks(kc_ref, vtc_ref, kl_ref, vtl_ref, tk, kcols, vrows):
    chunks = [(kc_ref[:, kcols], vtc_ref[vrows, :])]
    if kl_ref is not None:
        for j in range(kl_ref.shape[0] // tk):
            keys = slice(j * tk, (j + 1) * tk)
            chunks.append((kl_ref[keys, kcols], vtl_ref[vrows, keys]))
    return chunks


def _cast_riders(cast, grid):
    if not cast:
        return [], [], [], []
    layer, arrs = cast
    steps = math.prod(grid)
    strides = [math.prod(grid[i + 1:]) for i in range(len(grid))]
    step = lambda *ids: sum(i * st for i, st in zip(ids, strides))
    in_specs, out_specs, shapes = [], [], []
    for a in arrs:
        _, e, r, c = a.shape
        if steps >= e:
            per = steps // e
            rows = r // per
            assert steps % e == 0 and r % per == 0 and rows % 16 == 0
            in_specs.append(pl.BlockSpec((None, None, rows, c),
                                         lambda *ids, per=per: (layer, step(*ids) // per, step(*ids) % per, 0)))
            out_specs.append(pl.BlockSpec((None, rows, c),
                                          lambda *ids, per=per: (step(*ids) // per, step(*ids) % per, 0)))
        else:
            assert e % steps == 0
            in_specs.append(pl.BlockSpec((None, e // steps, r, c), lambda *ids: (layer, step(*ids), 0, 0)))
            out_specs.append(pl.BlockSpec((e // steps, r, c), lambda *ids: (step(*ids), 0, 0)))
        shapes.append(jax.ShapeDtypeStruct((e, r, c), BF16))
    return list(arrs), in_specs, out_specs, shapes


def _run_casts(in_refs, out_refs):
    for w_ref, o_ref in zip(in_refs, out_refs):
        o_ref[...] = w_ref[...].astype(BF16)


def _mla_attn_kernel(*refs, latent, tk, n_cast=0):
    if n_cast:
        _run_casts(refs[5:5 + n_cast], refs[6 + n_cast:])
        refs = refs[:5] + refs[5 + n_cast:6 + n_cast]
    if latent:
        q_ref, kc_ref, vtc_ref, kl_ref, vtl_ref, o_ref = refs
    else:
        q_ref, kc_ref, vtc_ref, o_ref = refs
        kl_ref = vtl_ref = None
    subs = [slice(r, r + Q_SUB) for r in range(0, q_ref.shape[0], Q_SUB)]
    heads = q_ref.shape[1] // LANES
    streams = []
    for qs in subs:
        for hh in range(heads):
            cols = slice(hh * LANES, (hh + 1) * LANES)
            rows = slice(hh * VT_A, (hh + 1) * VT_A)
            streams.append((q_ref[qs, cols], _key_chunks(kc_ref, vtc_ref, kl_ref, vtl_ref, tk, cols, rows)))
    accs = _flash_streams(streams, MLA_LOOKAHEAD, ref_row=MLA_REF_LANE)
    outs = [acc[0:MLA_V, :] * (1.0 / acc[MLA_V:MLA_V + 1, :]) for acc in accs]
    for i, qs in enumerate(subs):
        o_ref[qs, :] = jnp.concatenate(outs[heads * i:heads * (i + 1)], axis=0).T.astype(BF16)


def _mla_attn(q, kc, vtc, kl, vtl, tq, tk, cast=()):
    b, lq, _ = q.shape
    c = kc.shape[1]
    latent = kl is not None
    hps = MLA_HEADS_PER_STEP if latent else 2
    grid = (b, MLA_HEADS // hps, lq // tq)
    in_specs = [pl.BlockSpec((None, tq, hps * LANES), lambda i, p, j: (i, j, p)),
                pl.BlockSpec((None, c, hps * LANES), lambda i, p, j: (i, 0, p)),
                pl.BlockSpec((hps * VT_A, c), lambda i, p, j: (p, i))]
    args = [q, kc, vtc]
    if latent:
        s = kl.shape[1]
        in_specs += [pl.BlockSpec((None, s, hps * LANES), lambda i, p, j: (i, 0, p)),
                     pl.BlockSpec((hps * VT_A, s), lambda i, p, j: (p, i))]
        args += [kl, vtl]
    cast_args, cast_in, cast_out, cast_shapes = _cast_riders(cast, grid)
    outs = pl.pallas_call(
        functools.partial(_mla_attn_kernel, latent=latent, tk=tk, n_cast=len(cast_args)),
        grid=grid,
        in_specs=in_specs + cast_in,
        out_specs=[pl.BlockSpec((None, tq, hps * MLA_V), lambda i, p, j: (i, j, p))] + cast_out,
        out_shape=[jax.ShapeDtypeStruct((b, lq, MLA_HEADS * MLA_V), BF16)] + cast_shapes,
        compiler_params=_params("parallel", "parallel", "parallel"),
        name="mla_attn_lat" if latent else "mla_attn_ctx",
    )(*args, *cast_args)
    return outs[0], list(outs[1:])


def _diff_attn_kernel(*refs, latent, tk, lam_init, n_cast=0):
    if n_cast:
        _run_casts(refs[7:7 + n_cast], refs[8 + n_cast:])
        refs = refs[:7] + refs[7 + n_cast:8 + n_cast]
    if latent:
        lam_ref, g_ref, q_ref, kc_ref, vtc_ref, kl_ref, vtl_ref, o_ref = refs
    else:
        lam_ref, g_ref, q_ref, kc_ref, vtc_ref, o_ref = refs
        kl_ref = vtl_ref = None
    subs = [slice(r, r + Q_SUB) for r in range(0, q_ref.shape[0], Q_SUB)]
    heads = q_ref.shape[1] // (2 * LANES)
    streams = []
    for qs in subs:
        for hh in range(heads):
            for mp in range(2):
                cols = slice((2 * hh + mp) * LANES, (2 * hh + mp + 1) * LANES)
                rows = slice(hh * VT_D, (hh + 1) * VT_D)
                streams.append((q_ref[qs, cols], _key_chunks(kc_ref, vtc_ref, kl_ref, vtl_ref, tk, cols, rows)))
    accs = _flash_streams(streams, DIFF_LOOKAHEAD, ref_row=DIFF_HD)
    outs = [acc[0:LANES, :] * (1.0 / acc[LANES:LANES + 1, :]) for acc in accs]
    lam = (jnp.exp(jnp.sum(lam_ref[0:1, :] * lam_ref[1:2, :], axis=-1, keepdims=True))
           - jnp.exp(jnp.sum(lam_ref[2:3, :] * lam_ref[3:4, :], axis=-1, keepdims=True)) + lam_init)
    for i, qs in enumerate(subs):
        for hh in range(heads):
            s0 = 2 * (heads * i + hh)
            o = outs[s0] - lam * outs[s0 + 1]
            o = o * lax.rsqrt(jnp.mean(o * o, axis=0, keepdims=True) + NORM_EPS) * (g_ref[...] * (1.0 - lam_init))
            o_ref[qs, hh * LANES:(hh + 1) * LANES] = o.T.astype(BF16)


def _diff_attn(lam_p, subln_g, q, kc, vtc, kl, vtl, tq, tk, lam_init, cast=()):
    b, lq, _ = q.shape
    c = kc.shape[1]
    latent = kl is not None
    hps = DIFF_HEADS_PER_STEP if latent else 1
    grid = (b, DIFF_HEADS // hps, lq // tq)
    kblk = lambda rows: pl.BlockSpec((None, rows, hps * 2 * LANES), lambda i, h, j: (i, 0, h))
    vblk = lambda keys: pl.BlockSpec((hps * VT_D, keys), lambda i, h, j: (h, i))
    in_specs = [_const_spec(lam_p.shape), _const_spec(subln_g.shape),
                pl.BlockSpec((None, tq, hps * 2 * LANES), lambda i, h, j: (i, j, h)), kblk(c), vblk(c)]
    args = [lam_p, subln_g, q, kc, vtc]
    if latent:
        in_specs += [kblk(kl.shape[1]), vblk(kl.shape[1])]
        args += [kl, vtl]
    cast_args, cast_in, cast_out, cast_shapes = _cast_riders(cast, grid)
    outs = pl.pallas_call(
        functools.partial(_diff_attn_kernel, latent=latent, tk=tk, lam_init=lam_init, n_cast=len(cast_args)),
        grid=grid,
        in_specs=in_specs + cast_in,
        out_specs=[pl.BlockSpec((None, tq, hps * LANES), lambda i, h, j: (i, j, h))] + cast_out,
        out_shape=[jax.ShapeDtypeStruct((b, lq, DIFF_HEADS * LANES), BF16)] + cast_shapes,
        compiler_params=_params("parallel", "parallel", "parallel"),
        name="diff_attn_lat" if latent else "diff_attn_ctx",
    )(*args, *cast_args)
    return outs[0], list(outs[1:])


def _conv_fill(first, last, prev_ref, cur_ref, next_ref, pad_ref, shift_ref):
    tr = cur_ref.shape[0]
    zero = jnp.zeros((HALO, cur_ref.shape[1]), F32)
    pad_ref[0:HALO, :] = jnp.where(first, zero, prev_ref[...])
    pad_ref[HALO:HALO + tr, :] = cur_ref[...]
    pad_ref[HALO + tr:, :] = jnp.where(last, zero, next_ref[...])
    span = shift_ref.shape[1]
    for phase in range(1, SUBLANES):
        shift_ref[phase] = pad_ref[phase:phase + span, :]


def _conv_rows(pad_ref, shift_ref, w_ref, b_ref, g_ref, beta_ref, r0, n):
    off = HALO - CONV_WIDTH // 2
    acc = jnp.zeros((n, pad_ref.shape[1]), F32) + b_ref[...]
    for tap in range(CONV_WIDTH):
        phase = (off + tap) % SUBLANES
        base = r0 + off + tap - phase
        win = pad_ref[base:base + n, :] if phase == 0 else shift_ref[phase, base:base + n, :]
        acc = acc + win * w_ref[tap:tap + 1, :]
    mu = jnp.mean(acc, axis=-1, keepdims=True)
    cen = acc - mu
    y = cen * lax.rsqrt(jnp.mean(cen * cen, axis=-1, keepdims=True) + NORM_EPS) * g_ref[...] + beta_ref[...]
    return (y * _sigmoid(y)).astype(BF16)


def _conv_kernel(prev_ref, cur_ref, next_ref, w_ref, b_ref, g_ref, beta_ref, o_ref, pad_ref, shift_ref):
    j = pl.program_id(1)
    _conv_fill(j == 0, j == pl.num_programs(1) - 1, prev_ref, cur_ref, next_ref, pad_ref, shift_ref)
    o_ref[...] = _conv_rows(pad_ref, shift_ref, w_ref, b_ref, g_ref, beta_ref, 0, cur_ref.shape[0])


def _conv(u, w, bias, g, beta, tr):
    b, l, ch = u.shape
    per = tr // HALO
    last = l // HALO - 1
    return pl.pallas_call(
        _conv_kernel,
        grid=(b, l // tr),
        in_specs=[pl.BlockSpec((None, HALO, ch), lambda i, j: (i, jnp.maximum(j * per - 1, 0), 0)),
                  pl.BlockSpec((None, tr, ch), lambda i, j: (i, j, 0)),
                  pl.BlockSpec((None, HALO, ch), lambda i, j: (i, jnp.minimum((j + 1) * per, last), 0)),
                  _const_spec(w.shape), _const_spec(bias.shape), _const_spec(g.shape), _const_spec(beta.shape)],
        out_specs=pl.BlockSpec((None, tr, ch), lambda i, j: (i, j, 0)),
        out_shape=jax.ShapeDtypeStruct((b, l, ch), BF16),
        scratch_shapes=[pltpu.VMEM((tr + 2 * HALO, ch), F32),
                        pltpu.VMEM((SUBLANES, tr + 2 * HALO - SUBLANES, ch), F32)],
        compiler_params=_params("parallel", "parallel"),
        name="conv_ln_silu",
    )(u, u, u, w, bias, g, beta)


def _route(logits):
    lane = lax.broadcasted_iota(jnp.int32, logits.shape, 1)
    neg = jnp.float32(-jnp.inf)
    big = jnp.int32(LANES)
    is_grp = (lane >= N_EXPERTS) & (lane < N_EXPERTS + N_GROUPS)
    gl = jnp.where(is_grp, logits, neg)
    gmax = jnp.max(gl, axis=-1, keepdims=True)
    gsel = jnp.min(jnp.where(gl == gmax, lane, big), axis=-1, keepdims=True) - N_EXPERTS
    p_g = 1.0 / jnp.sum(jnp.exp(gl - gmax), axis=-1, keepdims=True)
    in_grp = (lane >= gsel * EXPERTS_PER_GROUP) & (lane < (gsel + 1) * EXPERTS_PER_GROUP)
    e1 = jnp.where(in_grp, logits, neg)
    v1 = jnp.max(e1, axis=-1, keepdims=True)
    i1 = jnp.min(jnp.where(e1 == v1, lane, big), axis=-1, keepdims=True)
    e2 = jnp.where(lane == i1, neg, e1)
    v2 = jnp.max(e2, axis=-1, keepdims=True)
    i2 = jnp.min(jnp.where(e2 == v2, lane, big), axis=-1, keepdims=True)
    r = jnp.exp(v2 - v1)
    w1 = p_g / (1.0 + r)
    comb = jnp.where(lane == i1, w1, jnp.where(lane == i2, w1 * r, 0.0))
    return jnp.where(lane == GROUP_LANE, gsel.astype(F32), comb)


def _merge_kernel(*refs, tiles_per_seq):
    if tiles_per_seq is None:
        (x_ref, om_ref, oc_ref, od_ref, gates_ref, mod_ref, wm_ref, wc_ref, wd_ref, wout_ref,
         n2g_ref, wr_ref, br_ref, xn_ref, h2_ref, comb_ref) = refs
        conv_branch = lambda rows: oc_ref[rows, :]
    else:
        (x_ref, om_ref, up_ref, uc_ref, un_ref, cw_ref, cb_ref, cg_ref, cbeta_ref, od_ref, gates_ref, mod_ref,
         wm_ref, wc_ref, wd_ref, wout_ref, n2g_ref, wr_ref, br_ref, xn_ref, h2_ref, comb_ref,
         pad_ref, shift_ref) = refs
        ts = pl.program_id(0) % tiles_per_seq
        _conv_fill(ts == 0, ts == tiles_per_seq - 1, up_ref, uc_ref, un_ref, pad_ref, shift_ref)
        conv_branch = lambda rows: _conv_rows(pad_ref, shift_ref, cw_ref, cb_ref, cg_ref, cbeta_ref,
                                              rows.start, rows.stop - rows.start)
    tm, d = x_ref.shape
    parts = [slice(r, r + MERGE_PART) for r in range(0, tm, MERGE_PART)]

    def branch_sum(rows):
        gate = lambda j: gates_ref[rows, j * d:(j + 1) * d].astype(F32)
        return (gate(0) * jnp.dot(om_ref[rows, :], wm_ref[...], preferred_element_type=F32)
                + gate(1) * jnp.dot(conv_branch(rows), wc_ref[...], preferred_element_type=F32)
                + gate(2) * jnp.dot(od_ref[rows, :], wd_ref[...], preferred_element_type=F32))

    def residual_norm(rows, y):
        z = jnp.dot(y.astype(BF16), wout_ref[...], preferred_element_type=F32)
        xn = x_ref[rows, :] + mod_ref[2:3, :] * z
        xn_ref[rows, :] = xn
        h2 = _rms(xn, n2g_ref[...]) * (1.0 + mod_ref[4:5, :]) + mod_ref[3:4, :]
        h_hi = h2.astype(BF16)
        h2_ref[rows, :] = h_hi
        return h_hi, (h2 - h_hi.astype(F32)).astype(BF16)

    def router(rows, h_hi, h_lo):
        hw = jnp.dot(h_hi, wr_ref[...], preferred_element_type=F32)
        lw = jnp.dot(h_lo, wr_ref[:, 0:LANES], preferred_element_type=F32)
        comb_ref[rows, :] = _route(hw[:, 0:LANES] + hw[:, LANES:] + lw + br_ref[...])

    ys = [branch_sum(rows) for rows in parts]
    hs = [residual_norm(rows, y) for rows, y in zip(parts, ys)]
    for rows, (h_hi, h_lo) in zip(parts, hs):
        router(rows, h_hi, h_lo)


def _merge(x, om, oc, od, gates, mod, mod_row, wm, wc, wd, wout, n2g, wr, br, seq, tm, conv=None):
    t, d = x.shape
    tiles_per_seq = seq // tm
    row = lambda i: (i, 0)
    if mod_row is None:
        mod_map = lambda i: (i // tiles_per_seq, 0, 0)
    else:
        mod_map = lambda i: (mod_row, 0, 0)
    rows = lambda a: pl.BlockSpec((tm, a.shape[1]), row)
    scratch = []
    if conv is None:
        conv_specs, conv_args = [rows(oc)], [oc]
    else:
        ch = oc.shape[1]
        per = tm // HALO
        last = t // HALO - 1
        conv_specs = [pl.BlockSpec((HALO, ch), lambda i: (jnp.maximum(i * per - 1, 0), 0)), rows(oc),
                      pl.BlockSpec((HALO, ch), lambda i: (jnp.minimum((i + 1) * per, last), 0))]
        conv_specs += [_const_spec(a.shape) for a in conv]
        conv_args = [oc, oc, oc, *conv]
        scratch = [pltpu.VMEM((tm + 2 * HALO, ch), F32),
                   pltpu.VMEM((SUBLANES, tm + 2 * HALO - SUBLANES, ch), F32)]
    return pl.pallas_call(
        functools.partial(_merge_kernel, tiles_per_seq=None if conv is None else tiles_per_seq),
        grid=(t // tm,),
        in_specs=[rows(x), rows(om), *conv_specs, rows(od), rows(gates),
                  pl.BlockSpec((None, 6, d), mod_map)]
                 + [_const_spec(a.shape) for a in (wm, wc, wd, wout, n2g, wr, br)],
        out_specs=[pl.BlockSpec((tm, d), row), pl.BlockSpec((tm, d), row), pl.BlockSpec((tm, LANES), row)],
        out_shape=[jax.ShapeDtypeStruct((t, d), F32), jax.ShapeDtypeStruct((t, d), BF16),
                   jax.ShapeDtypeStruct((t, LANES), F32)],
        scratch_shapes=scratch,
        compiler_params=_params("parallel"),
        name="merge_route",
    )(x, om, *conv_args, od, gates, mod, wm, wc, wd, wout, n2g, wr, br)


def _moe_kernel(h_ref, comb_ref, xn_ref, mod_ref, tri_ref, wg_ref, wu_ref, wd_ref, fg_ref, o_ref,
                pos_ref, comb3_ref, *, final, mod_rows):
    g = pl.program_id(1)
    slabs, slab_rows, d = o_ref.shape
    tm = slabs * slab_rows

    @pl.when(g == 0)
    def _():
        o_ref[...] = jnp.zeros_like(o_ref)
        comb = comb_ref[...].reshape(tm, LANES)
        hi = comb.astype(BF16)
        r1 = comb - hi.astype(F32)
        mid = r1.astype(BF16)
        lo = (r1 - mid.astype(F32)).astype(BF16)
        comb3_ref[...] = jnp.concatenate([hi, mid, lo], axis=1)
        grow = comb.T[GROUP_LANE:GROUP_LANE + 1, :]
        gid = lax.broadcasted_iota(jnp.int32, (8, tm), 0).astype(F32)
        member = grow == gid
        rank = jnp.dot(jnp.where(member, 1.0, 0.0).astype(BF16), tri_ref[...], preferred_element_type=F32)
        pos_ref[...] = jnp.where(member, rank, -1.0)

    row = lax.broadcasted_iota(jnp.int32, (8, tm), 0)
    posg = jnp.sum(jnp.where(row == g, pos_ref[...], 0.0), axis=0, keepdims=True)
    count = jnp.max(posg).astype(jnp.int32) + 1

    def chunk(c, rows):
        slot = (lax.broadcasted_iota(jnp.int32, (rows, tm), 0) + c * rows).astype(F32)
        sel = jnp.where(posg == slot, 1.0, 0.0).astype(BF16)
        h = h_ref[...].reshape(tm, d)
        x = jnp.dot(sel, h, preferred_element_type=F32).astype(BF16)
        cw3 = jnp.dot(sel, comb3_ref[...], preferred_element_type=F32)
        cw = cw3[:, 0:LANES] + cw3[:, LANES:2 * LANES] + cw3[:, 2 * LANES:]
        lane = lax.broadcasted_iota(jnp.int32, cw.shape, 1)
        y = jnp.zeros((rows, d), F32)
        for e in range(EXPERTS_PER_GROUP):
            a = jnp.dot(x, wg_ref[e], preferred_element_type=F32)
            u = jnp.dot(x, wu_ref[e], preferred_element_type=F32)
            w = jnp.sum(jnp.where(lane == g * EXPERTS_PER_GROUP + e, cw, 0.0), axis=1, keepdims=True)
            hid = a * _sigmoid(a) * u * w
            y = y + jnp.dot(hid.astype(BF16), wd_ref[e], preferred_element_type=F32)
        back = lax.dot_general(sel, y.astype(BF16), (((0,), (0,)), ((), ())), preferred_element_type=F32)
        o_ref[...] += back.reshape(slabs, slab_rows, d)

    @pl.when((count > 0) & (count <= MOE_SMALL_CHUNK))
    def _():
        chunk(0, MOE_SMALL_CHUNK)

    @pl.when(count > MOE_SMALL_CHUNK)
    def _():
        def body(c, carry):
            chunk(c, MOE_CHUNK)
            return carry
        lax.fori_loop(0, (count + MOE_CHUNK - 1) // MOE_CHUNK, body, 0)

    @pl.when(g == pl.num_programs(1) - 1)
    def _():
        for sl, mod_row in enumerate(mod_rows):
            out = xn_ref[sl] + mod_ref[mod_row, 5:6, :] * o_ref[sl]
            o_ref[sl] = _rms(out, fg_ref[...]) if final else out


def _moe(h2, comb, xn, mod, mod_rows, wg, wu, wd, final_g, tm, final):
    t, d = xn.shape
    slabs = len(mod_rows)
    slab_rows = tm // slabs
    n_tiles = t // tm
    view = lambda a: a.reshape(slabs, n_tiles, slab_rows, a.shape[-1])
    tile = lambda a: pl.BlockSpec((slabs, None, slab_rows, a.shape[-1]), lambda i, g: (0, i, 0, 0))
    ff = wg.shape[2]
    idx = jnp.arange(tm)
    tri = (idx[:, None] < idx[None, :]).astype(BF16)
    grp = lambda i, g: (g, 0, 0)
    once = pl.Buffered(1)
    xn_spec = pl.BlockSpec((slabs, None, slab_rows, d), lambda i, g: (0, i, 0, 0))
    out = pl.pallas_call(
        functools.partial(_moe_kernel, final=final, mod_rows=tuple(mod_rows)),
        grid=(n_tiles, N_GROUPS),
        in_specs=[tile(h2), tile(comb), xn_spec,
                  _const_spec(mod.shape),
                  pl.BlockSpec((tm, tm), lambda i, g: (0, 0), pipeline_mode=once),
                  pl.BlockSpec((EXPERTS_PER_GROUP, d, ff), grp),
                  pl.BlockSpec((EXPERTS_PER_GROUP, d, ff), grp),
                  pl.BlockSpec((EXPERTS_PER_GROUP, ff, d), grp),
                  pl.BlockSpec(final_g.shape, lambda i, g: (0, 0))],
        out_specs=tile(xn),
        out_shape=jax.ShapeDtypeStruct((slabs, n_tiles, slab_rows, d), F32),
        scratch_shapes=[pltpu.VMEM((8, tm), F32), pltpu.VMEM((tm, 3 * LANES), BF16)],
        compiler_params=pltpu.CompilerParams(dimension_semantics=("parallel", "arbitrary"),
                                             vmem_limit_bytes=MOE_VMEM_LIMIT),
        name="moe",
    )(view(h2), view(comb), view(xn), mod, tri, wg, wu, wd, final_g)
    return out.reshape(t, d)


def _rope_tables(s):
    rows = s // GRID_W
    row = jnp.repeat(jnp.arange(rows, dtype=F32), GRID_W)
    col = jnp.tile(jnp.arange(GRID_W, dtype=F32), rows)

    def cos_sin(rot_dim):
        n = rot_dim // 4
        inv = ROPE_BASE ** (-jnp.arange(n, dtype=F32) / n)
        ang = jnp.concatenate([row[:, None] * inv, col[:, None] * inv], axis=-1)
        return jnp.cos(ang), jnp.sin(ang)

    ca, sa = cos_sin(MLA_ROPE)
    one = jnp.ones((s, MLA_NOPE), F32)
    pad1 = jnp.ones((s, LANES - MLA_NOPE - MLA_ROPE), F32)
    cos_a = jnp.concatenate([one, ca, ca, pad1], axis=-1)
    sin_a = jnp.concatenate([0 * one, -sa, sa, 0 * pad1], axis=-1)
    cd, sd = cos_sin(DIFF_HD)
    cos_d = jnp.concatenate([cd, cd, cd, cd], axis=-1)
    sin_d = jnp.concatenate([-sd, sd, -sd, sd], axis=-1)
    return cos_a, sin_a, cos_d, sin_d


def _layout_w_in(w_in):
    d = w_in.shape[0]
    o = 0
    parts = {}
    for name, width in (("cq", Q_LORA), ("ckv", KV_LORA), ("kr", MLA_ROPE), ("glu", 2 * CONV_CH),
                        ("dq", 2 * DIFF_HEADS * DIFF_HD), ("dk", 2 * DIFF_HEADS * DIFF_HD),
                        ("dv", 2 * DIFF_HEADS * DIFF_HD), ("gates", 3 * d)):
        parts[name] = w_in[:, o:o + width]
        o += width
    kr_blk = jnp.concatenate([jnp.zeros((d, MLA_NOPE), w_in.dtype), parts["kr"],
                              jnp.zeros((d, LANES - MLA_NOPE - MLA_ROPE), w_in.dtype)], axis=1)
    win = jnp.concatenate([parts["cq"], parts["ckv"], kr_blk, parts["glu"], parts["dq"], parts["dk"],
                           parts["gates"]], axis=1)
    return win.astype(BF16), parts["dv"].T.astype(BF16)


def _layout_w_uq(w_uq):
    r = w_uq.shape[0]
    w = w_uq.reshape(r, MLA_HEADS, MLA_NOPE + MLA_ROPE)
    w = jnp.pad(w, ((0, 0), (0, 0), (0, LANES - MLA_NOPE - MLA_ROPE)))
    return w.reshape(r, MLA_HEADS * LANES).astype(BF16)


def _layout_w_ukv(w_ukv):
    r = w_ukv.shape[0]
    w = w_ukv.reshape(r, MLA_HEADS, MLA_NOPE + MLA_V)
    wk = jnp.pad(w[:, :, :MLA_NOPE], ((0, 0), (0, 0), (0, LANES - MLA_NOPE))).reshape(r, MLA_HEADS * LANES)
    wvt = w[:, :, MLA_NOPE:].reshape(r, MLA_HEADS * MLA_V).T
    return wk.astype(BF16), wvt.astype(BF16)


def _layout_router(w_rg, b_rg, w_re, b_re):
    d = w_rg.shape[0]
    w = jnp.concatenate([w_re, w_rg, jnp.zeros((d, LANES - N_EXPERTS - N_GROUPS), F32)], axis=1)
    hi = w.astype(BF16)
    lo = (w - hi.astype(F32)).astype(BF16)
    b = jnp.concatenate([b_re, b_rg, jnp.zeros((LANES - N_EXPERTS - N_GROUPS,), F32)])[None, :]
    return jnp.concatenate([hi, lo], axis=1), b


def kernel(x, c, ctx, c_ctx, w_ada, b_ada, norm1_g, w_in, q_norm_g, w_uq, kv_norm_g, w_ukv, conv_w, conv_b,
           conv_ln_g, conv_ln_b, lam_q1, lam_k1, lam_q2, lam_k2, diff_subln_g, w_o_mla, w_o_conv, w_o_diff,
           w_out, norm2_g, w_rg, b_rg, w_re, b_re, w_gate, w_up, w_down, final_g):
    b, s, d = x.shape
    n_ctx = ctx.shape[1]
    depth = w_ada.shape[0]
    tm = 512
    tq = min(4 * Q_SUB, s)
    tq_c = min(tq, n_ctx)
    tk = min(256, s)
    tr = 256
    tmoe_l = min(1024, b * s)
    tmoe_c = min(1024, b * n_ctx)
    ctx_row = b

    rows = -(-(b + 1) // 8) * 8
    cc = jnp.zeros((rows, d), F32).at[:b].set(c).at[b].set(c_ctx)
    mod_all = _ada(cc, w_ada, b_ada).reshape(depth, rows, 6, d)
    tables = _rope_tables(s)
    row2 = lambda a: a[None, :]

    xl = x.reshape(b * s, d)
    xc = ctx.reshape(b * n_ctx, d)
    for l in range(depth):
        update_ctx = l < depth - 1
        final = l == depth - 1
        lam_init = 0.8 - 0.6 * math.exp(-0.3 * l)
        mod = mod_all[l]
        win, wdvt = _layout_w_in(w_in[l])
        wuq = _layout_w_uq(w_uq[l])
        wuk, wuvt = _layout_w_ukv(w_ukv[l])
        n1g, qng, kvg = row2(norm1_g[l]), row2(q_norm_g[l]), row2(kv_norm_g[l])
        lam_p = jnp.stack([lam_q1[l], lam_k1[l], lam_q2[l], lam_k2[l]])
        subln = diff_subln_g[l][:, None]

        pc = _proj(xc, mod, ctx_row, n1g, win, qng, wuq, kvg, wuk, wuvt, wdvt, None, n_ctx, tm)
        pl_ = _proj(xl, mod, None, n1g, win, qng, wuq, kvg, wuk, wuvt, wdvt, tables, s, tm)
        per_batch = lambda arrs, n: [a if i in (2, 6) else a.reshape(b, n, -1) for i, a in enumerate(arrs)]
        qc, kc, vc, uc, dqc, dkc, dvc, gc = per_batch(pc, n_ctx)
        ql, kl, vl, ul, dql, dkl, dvl, gl = per_batch(pl_, s)

        wm, wc, wd = w_o_mla[l].astype(BF16), w_o_conv[l].astype(BF16), w_o_diff[l].astype(BF16)
        wout = w_out[l].astype(BF16)
        wr, br = _layout_router(w_rg[l], b_rg[l], w_re[l], b_re[l])
        n2g = row2(norm2_g[l])
        cw, cb = conv_w[l], row2(conv_b[l])
        cg, cbeta = row2(conv_ln_g[l]), row2(conv_ln_b[l])
        fg = row2(final_g)

        def tail(xs, om, oc, od, gates, mod_row, seq, tmoe, is_final, conv=None):
            flat = lambda a: a.reshape(-1, a.shape[-1])
            xn, h2, comb = _merge(xs, flat(om), flat(oc), flat(od), flat(gates), mod, mod_row,
                                  wm, wc, wd, wout, n2g, wr, br, seq, tm, conv=conv)
            moe_rows = tuple(range(b)) if mod_row is None else (mod_row,)
            return _moe(h2, comb, xn, mod, moe_rows, wg, wu, wdn, fg, tmoe, is_final)

        om_l, (wg, wu) = _mla_attn(ql, kc, vc, kl, vl, tq, tk, cast=(l, [w_gate, w_up]))
        od_l, (wdn,) = _diff_attn(lam_p, subln, dql, dkc, dvc, dkl, dvl, tq, tk, lam_init, cast=(l, [w_down]))
        xl_new = tail(xl, om_l, ul, od_l, gl, None, s, tmoe_l, final, conv=(cw, cb, cg, cbeta))
        if update_ctx:
            om_c, _ = _mla_attn(qc, kc, vc, None, None, tq_c, tk)
            od_c, _ = _diff_attn(lam_p, subln, dqc, dkc, dvc, None, None, tq_c, tk, lam_init)
            oc_c = _conv(uc, cw, cb, cg, cbeta, tr)
            xc = tail(xc, om_c, oc_c, od_c, gc, ctx_row, b * n_ctx, tmoe_c, False)
        xl = xl_new
    return xl.reshape(b, s, d)
```
